```python
import jax, jax.numpy as jnp
from jax import lax
import numpy as np

D_MODEL = 1024
BATCH = 16
SEQ = 2048
DEPTH = 1

HEAD_DIM = 64
MOBA_HEADS = 8
MOBA_BLOCK = 256
MOBA_TOPK = 3
MOBA_Q_CHUNK = 8
DSA_HEADS = 8
DSA_TOPK = 256
DSA_Q_CHUNK = 16
IDX_HEADS = 8
IDX_DIM = 64
MEM_LEN = 256
X_HEADS = 4
X_HEAD_DIM = 128
ROPE_THETA = 10000.0
RMS_EPS = 1e-6
N_BRANCH = 3
MOBA_W = MOBA_HEADS * HEAD_DIM
DSA_W = DSA_HEADS * HEAD_DIM
X_W = X_HEADS * X_HEAD_DIM
IN_SIZES = (MOBA_W, MOBA_W, MOBA_W, MOBA_W, DSA_W, DSA_W, DSA_W, DSA_W,
            IDX_HEADS * IDX_DIM, IDX_DIM, IDX_HEADS, X_W, X_W, N_BRANCH * D_MODEL)
IN_WIDTH = 4 * MOBA_W + 4 * DSA_W + IDX_HEADS * IDX_DIM + IDX_DIM + IDX_HEADS + 2 * X_W + N_BRANCH * D_MODEL

kernel_name = 'hybrid_moba_dsa_xattn_gated_block'

NEG_INF = -jnp.inf


def rmsnorm(x, g):
    xf = x.astype(jnp.float32)
    y = xf * lax.rsqrt(jnp.mean(xf * xf, axis=-1, keepdims=True) + RMS_EPS)
    return (y * g.astype(jnp.float32)).astype(x.dtype)


def rope(x, pos):
    d = x.shape[-1]
    half = d // 2
    inv = jnp.power(ROPE_THETA, -jnp.arange(half, dtype=jnp.float32) * 2.0 / d)
    ang = pos.astype(jnp.float32)[:, None] * inv[None, :]
    cos = jnp.cos(ang)[None, :, None, :]
    sin = jnp.sin(ang)[None, :, None, :]
    xf = x.astype(jnp.float32)
    x1, x2 = xf[..., :half], xf[..., half:]
    return jnp.concatenate([x1 * cos - x2 * sin, x2 * cos + x1 * sin], axis=-1).astype(x.dtype)


def moba_attention(q, k, v):
    B, S, H, D = q.shape
    nb = -(-S // MOBA_BLOCK)
    sp = nb * MOBA_BLOCK
    padw = ((0, 0), (0, sp - S), (0, 0), (0, 0))
    qp, kp, vp = jnp.pad(q, padw), jnp.pad(k, padw), jnp.pad(v, padw)
    scale = D ** -0.5
    kbh = kp.reshape(B, nb, MOBA_BLOCK, H, D).transpose(0, 3, 1, 2, 4)
    vbh = vp.reshape(B, nb, MOBA_BLOCK, H, D).transpose(0, 3, 1, 2, 4)
    kmean = jnp.mean(kbh.astype(jnp.float32), axis=3)
    gate = jnp.einsum('bshd,bhnd->bshn', qp.astype(jnp.float32), kmean)
    qblk = jnp.arange(sp) // MOBA_BLOCK
    past = jnp.arange(nb)[None, :] < qblk[:, None]
    gate = jnp.where(past[None, :, None, :], gate, NEG_INF)
    n_sel = max(min(MOBA_TOPK, nb - 1), 1)
    top_val, top_idx = lax.top_k(gate, n_sel)
    top_ok = jnp.isfinite(top_val)
    nc = sp // MOBA_Q_CHUNK
    b_ix = jnp.arange(B)[:, None, None, None]
    h_ix = jnp.arange(H)[None, None, :, None]

    def to_chunks(a):
        return a.reshape((B, nc, MOBA_Q_CHUNK) + a.shape[2:]).swapaxes(0, 1)

    def step(args):
        ci, q_c, idx_c, ok_c = args
        t0 = ci * MOBA_Q_CHUNK
        blk_start = (t0 // MOBA_BLOCK) * MOBA_BLOCK
        k_own = lax.dynamic_slice_in_dim(kp, blk_start, MOBA_BLOCK, axis=1)
        v_own = lax.dynamic_slice_in_dim(vp, blk_start, MOBA_BLOCK, axis=1)
        qpos = t0 + jnp.arange(MOBA_Q_CHUNK)
        kpos = blk_start + jnp.arange(MOBA_BLOCK)
        s_own = jnp.einsum('bqhd,bkhd->bhqk', q_c, k_own).astype(jnp.float32) * scale
        s_own = jnp.where((kpos[None, :] <= qpos[:, None])[None, None], s_own, NEG_INF)
        k_sel = kbh[b_ix, h_ix, idx_c]
        v_sel = vbh[b_ix, h_ix, idx_c]
        s_past = jnp.einsum('bqhd,bqhnkd->bhqnk', q_c, k_sel).astype(jnp.float32) * scale
        s_past = jnp.where(ok_c.transpose(0, 2, 1, 3)[..., None], s_past, NEG_INF)
        n_past = n_sel * MOBA_BLOCK
        s_all = jnp.concatenate([s_past.reshape(B, H, MOBA_Q_CHUNK, n_past), s_own], axis=-1)
        p = jax.nn.softmax(s_all, axis=-1).astype(q.dtype)
        p_past = p[..., :n_past].reshape(B, H, MOBA_Q_CHUNK, n_sel, MOBA_BLOCK)
        p_own = p[..., n_past:]
        return (jnp.einsum('bhqnk,bqhnkd->bqhd', p_past, v_sel)
                + jnp.einsum('bhqk,bkhd->bqhd', p_own, v_own))

    outs = lax.map(step, (jnp.arange(nc), to_chunks(qp), to_chunks(top_idx), to_chunks(top_ok)))
    return outs.swapaxes(0, 1).reshape(B, sp, H, D)[:, :S]


def dsa_attention(q, k, v, iq, ik, iw):
    B, S, H, D = q.shape
    k_top = min(DSA_TOPK, S // 4)
    nc = S // DSA_Q_CHUNK
    scale = D ** -0.5
    iscale = IDX_DIM ** -0.5
    wscale = IDX_HEADS ** -0.5
    b_ix = jnp.arange(B)[:, None, None]
    kpos = jnp.arange(S)

    def to_chunks(a):
        return a.reshape((B, nc, DSA_Q_CHUNK) + a.shape[2:]).swapaxes(0, 1)

    def step(args):
        ci, q_c, iq_c, iw_c = args
        qpos = ci * DSA_Q_CHUNK + jnp.arange(DSA_Q_CHUNK)
        logits = jnp.einsum('bqgd,bsd->bqgs', iq_c, ik).astype(jnp.float32) * iscale
        score = jnp.einsum('bqgs,bqg->bqs', jax.nn.relu(logits), iw_c.astype(jnp.float32) * wscale)
        score = jnp.where((kpos[None, :] <= qpos[:, None])[None], score, NEG_INF)
        top_val, top_idx = lax.top_k(score, k_top)
        ok = jnp.isfinite(top_val)
        k_sel = k[b_ix, top_idx]
        v_sel = v[b_ix, top_idx]
        s = jnp.einsum('bqhd,bqkhd->bhqk', q_c, k_sel).astype(jnp.float32) * scale
        s = jnp.where(ok[:, None], s, NEG_INF)
        p = jax.nn.softmax(s, axis=-1).astype(q.dtype)
        return jnp.einsum('bhqk,bqkhd->bqhd', p, v_sel)

    outs = lax.map(step, (jnp.arange(nc), to_chunks(q), to_chunks(iq), to_chunks(iw)))
    return outs.swapaxes(0, 1).reshape(B, S, H, D)


def cross_attention(q, mk, mv):
    scale = q.shape[-1] ** -0.5
    s = jnp.einsum('bshd,bmhd->bhsm', q, mk).astype(jnp.float32) * scale
    p = jax.nn.softmax(s, axis=-1).astype(q.dtype)
    return jnp.einsum('bhsm,bmhd->bshd', p, mv)


def hybrid_layer(x, mem, g_in, w_in, b_merge, g_mem, w_mem_kv, w_up_moba, w_up_dsa, w_up_cross, w_out):
    B, S, _ = x.shape
    pos = jnp.arange(S)
    h = rmsnorm(x, g_in)
    z = h @ w_in
    offs = np.cumsum(IN_SIZES)[:-1].tolist()
    (mq, mk, mv, mg, dq, dk, dv, dg, iq, ik, iw, xq, xg, glog) = jnp.split(z, offs, axis=-1)
    heads = lambda a, n, d: a.reshape(B, S, n, d)
    mq = rope(heads(mq, MOBA_HEADS, HEAD_DIM), pos)
    mk = rope(heads(mk, MOBA_HEADS, HEAD_DIM), pos)
    ya = moba_attention(mq, mk, heads(mv, MOBA_HEADS, HEAD_DIM)).reshape(B, S, MOBA_W)
    ya = (ya * jax.nn.silu(mg)) @ w_up_moba
    dq = rope(heads(dq, DSA_HEADS, HEAD_DIM), pos)
    dk = rope(heads(dk, DSA_HEADS, HEAD_DIM), pos)
    iq = rope(heads(iq, IDX_HEADS, IDX_DIM), pos)
    ik = rope(ik[:, :, None, :], pos)[:, :, 0, :]
    yb = dsa_attention(dq, dk, heads(dv, DSA_HEADS, HEAD_DIM), iq, ik, iw).reshape(B, S, DSA_W)
    yb = (yb * jax.nn.silu(dg)) @ w_up_dsa
    M = mem.shape[1]
    kv = rmsnorm(mem, g_mem) @ w_mem_kv
    xk = kv[..., :X_W].reshape(B, M, X_HEADS, X_HEAD_DIM)
    xv = kv[..., X_W:].reshape(B, M, X_HEADS, X_HEAD_DIM)
    yc = cross_attention(heads(xq, X_HEADS, X_HEAD_DIM), xk, xv).reshape(B, S, X_W)
    yc = (yc * jax.nn.silu(xg)) @ w_up_cross
    gates = jax.nn.sigmoid(glog + b_merge).reshape(B, S, N_BRANCH, D_MODEL)
    u = gates[:, :, 0] * ya + gates[:, :, 1] * yb + gates[:, :, 2] * yc
    return x + u @ w_out


def setup_inputs(seed: int = 0) -> dict:
    key = jax.random.key(seed)
    ks = jax.random.split(key, 12)
    f32 = jnp.float32
    nrm = lambda k, shape, s: jax.random.normal(k, shape, f32) * s
    return {
        'x': nrm(ks[0], (BATCH, SEQ, D_MODEL), 1.0),
        'mem': nrm(ks[1], (BATCH, MEM_LEN, D_MODEL), 1.0),
        'g_in': 1.0 + nrm(ks[2], (DEPTH, D_MODEL), 0.02),
        'w_in': nrm(ks[3], (DEPTH, D_MODEL, IN_WIDTH), D_MODEL ** -0.5),
        'b_merge': nrm(ks[4], (DEPTH, N_BRANCH * D_MODEL), 0.1),
        'g_mem': 1.0 + nrm(ks[5], (DEPTH, D_MODEL), 0.02),
        'w_mem_kv': nrm(ks[6], (DEPTH, D_MODEL, 2 * X_W), D_MODEL ** -0.5),
        'w_up_moba': nrm(ks[7], (DEPTH, MOBA_W, D_MODEL), MOBA_W ** -0.5),
        'w_up_dsa': nrm(ks[8], (DEPTH, DSA_W, D_MODEL), DSA_W ** -0.5),
        'w_up_cross': nrm(ks[9], (DEPTH, X_W, D_MODEL), X_W ** -0.5),
        'w_out': nrm(ks[10], (DEPTH, D_MODEL, D_MODEL), D_MODEL ** -0.5),
        'g_final': 1.0 + nrm(ks[11], (D_MODEL,), 0.02),
    }


def reference(x, mem, g_in, w_in, b_merge, g_mem, w_mem_kv, w_up_moba, w_up_dsa, w_up_cross, w_out, g_final):
    for l in range(DEPTH):
        x = hybrid_layer(x, mem, g_in[l], w_in[l], b_merge[l], g_mem[l], w_mem_kv[l],
                         w_up_moba[l], w_up_dsa[l], w_up_cross[l], w_out[l])
    return rmsnorm(x, g_final)
```

```python
import functools

import jax
import jax.numpy as jnp
from jax import lax
from jax.experimental import pallas as pl
from jax.experimental.pallas import tpu as pltpu

D_MODEL = 1024
SEQ = 2048
HEAD_DIM = 64
N_HEADS = 8
ATT_W = N_HEADS * HEAD_DIM
BLK = 256
N_BLK = SEQ // BLK
MOBA_TOPK = 3
DSA_TOPK = 256
IDX_HEADS = 8
MEM_LEN = 256
X_HEADS = 4
X_HEAD_DIM = 128
X_W = X_HEADS * X_HEAD_DIM
N_BRANCH = 3
ROPE_THETA = 10000.0
RMS_EPS = 1e-6
LANES = 128
HEADS_PER_VREG = LANES // HEAD_DIM

IN_SIZES = (ATT_W, ATT_W, ATT_W, ATT_W, ATT_W, ATT_W, ATT_W, ATT_W,
            IDX_HEADS * HEAD_DIM, HEAD_DIM, IDX_HEADS, X_W, X_W, N_BRANCH * D_MODEL)

OFF_MQ, OFF_MK, OFF_MV, OFF_MG = 0, 512, 1024, 1536
OFF_DQ, OFF_DK, OFF_DV, OFF_DG = 2048, 2560, 3072, 3584
OFF_IQ = 4096
OFF_IK = 4608
OFF_IW = 4736
OFF_XQ = 4864
OFF_XG = 5376
OFF_GL = 5888
PACKED_W = OFF_GL + N_BRANCH * D_MODEL

VMEM_LIMIT = 56 * 1024 * 1024

F32 = jnp.float32
BF16 = jnp.bfloat16
NEG_INF = float("-inf")
INT_MIN = -2 ** 31


def _nt_dot(a, b):
    return lax.dot_general(a, b, (((1,), (1,)), ((), ())), preferred_element_type=F32)


def _rmsnorm(x, g):
    ms = jnp.mean(x * x, axis=-1, keepdims=True)
    return (x * lax.rsqrt(ms + RMS_EPS)) * g


def _in_proj_body(x_ref, g_ref, w_ref, cos_ref, sin_ref,
                  mq_ref, mk_ref, mv_ref, mg_ref, dq_ref, dk_ref, dv_ref, dg_ref,
                  iq_ref, ik_ref, iw_ref, xq_ref, xg_ref, gl_ref, km_ref):
    h = _rmsnorm(x_ref[...], g_ref[...]).astype(BF16)
    cos = cos_ref[...]
    sin = sin_ref[...]
    lane = lax.broadcasted_iota(jnp.int32, (BLK, LANES), 1)
    first_half = (lane & (HEAD_DIM // 2)) == 0

    def seg(off, width):
        return jnp.dot(h, w_ref[:, off:off + width], preferred_element_type=F32)

    def rope(z):
        partner = jnp.where(first_half, pltpu.roll(z, LANES - HEAD_DIM // 2, 1),
                            pltpu.roll(z, HEAD_DIM // 2, 1))
        return z * cos + partner * sin

    def store_roped(ref, off, width, scale, mean_ref=None):
        z = seg(off, width)
        for c in range(width // LANES):
            r = rope(z[:, c * LANES:(c + 1) * LANES])
            if mean_ref is not None:
                mean_ref[0, :, c * LANES:(c + 1) * LANES] = jnp.mean(r, axis=0, keepdims=True)
            if scale is not None:
                r = r * scale
            ref[:, c * LANES:(c + 1) * LANES] = r.astype(ref.dtype)

    qk_scale = HEAD_DIM ** -0.5
    store_roped(mq_ref, OFF_MQ, ATT_W, qk_scale)
    store_roped(mk_ref, OFF_MK, ATT_W, None, mean_ref=km_ref)
    mv_ref[...] = seg(OFF_MV, ATT_W).astype(BF16)
    mg_ref[...] = seg(OFF_MG, ATT_W)
    store_roped(dq_ref, OFF_DQ, ATT_W, qk_scale)
    store_roped(dk_ref, OFF_DK, ATT_W, None)
    dv_ref[...] = seg(OFF_DV, ATT_W).astype(BF16)
    dg_ref[...] = seg(OFF_DG, ATT_W)
    store_roped(iq_ref, OFF_IQ, ATT_W, qk_scale)
    store_roped(ik_ref, OFF_IK, LANES, None)
    iw_ref[...] = seg(OFF_IW, LANES)
    xq_ref[...] = seg(OFF_XQ, X_W).astype(BF16)
    xg_ref[...] = seg(OFF_XG, X_W)
    for c in range(N_BRANCH):
        gl_ref[:, c * D_MODEL:(c + 1) * D_MODEL] = seg(OFF_GL + c * D_MODEL, D_MODEL)


def _in_proj(x2, g_in, w_packed, cos_t, sin_t):
    m = x2.shape[0]
    n_tiles = m // BLK
    row = lambda w: pl.BlockSpec((BLK, w), lambda i: (i, 0))
    out_shape = [
        jax.ShapeDtypeStruct((m, ATT_W), BF16),
        jax.ShapeDtypeStruct((m, ATT_W), BF16),
        jax.ShapeDtypeStruct((m, ATT_W), BF16),
        jax.ShapeDtypeStruct((m, ATT_W), F32),
        jax.ShapeDtypeStruct((m, ATT_W), BF16),
        jax.ShapeDtypeStruct((m, ATT_W), BF16),
        jax.ShapeDtypeStruct((m, ATT_W), BF16),
        jax.ShapeDtypeStruct((m, ATT_W), F32),
        jax.ShapeDtypeStruct((m, ATT_W), BF16),
        jax.ShapeDtypeStruct((m, LANES), BF16),
        jax.ShapeDtypeStruct((m, LANES), F32),
        jax.ShapeDtypeStruct((m, X_W), BF16),
        jax.ShapeDtypeStruct((m, X_W), F32),
        jax.ShapeDtypeStruct((m, N_BRANCH * D_MODEL), F32),
        jax.ShapeDtypeStruct((n_tiles, 1, ATT_W), F32),
    ]
    out_specs = [row(ATT_W)] * 9 + [row(LANES), row(LANES), row(X_W), row(X_W),
                                    row(N_BRANCH * D_MODEL),
                                    pl.BlockSpec((1, 1, ATT_W), lambda i: (i, 0, 0))]
    return pl.pallas_call(
        _in_proj_body,
        grid=(n_tiles,),
        in_specs=[
            pl.BlockSpec((BLK, D_MODEL), lambda i: (i, 0)),
            pl.BlockSpec((1, D_MODEL), lambda i: (0, 0)),
            pl.BlockSpec((D_MODEL, PACKED_W), lambda i: (0, 0), pipeline_mode=pl.Buffered(1)),
            pl.BlockSpec((BLK, LANES), lambda i: (i % N_BLK, 0)),
            pl.BlockSpec((BLK, LANES), lambda i: (i % N_BLK, 0)),
        ],
        out_specs=out_specs,
        out_shape=out_shape,
        compiler_params=pltpu.CompilerParams(
            dimension_semantics=("arbitrary",), vmem_limit_bytes=VMEM_LIMIT),
        name="in_proj",
    )(x2, g_in, w_packed, cos_t, sin_t)


def _mem_proj_body(m_ref, g_ref, w_ref, k_ref, v_ref):
    h = _rmsnorm(m_ref[...], g_ref[...]).astype(BF16)
    k_ref[...] = jnp.dot(h, w_ref[:, 0:X_W], preferred_element_type=F32).astype(BF16)
    v_ref[...] = jnp.dot(h, w_ref[:, X_W:2 * X_W], preferred_element_type=F32).astype(BF16)


def _mem_proj(mem2, g_mem, w_kv):
    m = mem2.shape[0]
    return pl.pallas_call(
        _mem_proj_body,
        grid=(m // BLK,),
        in_specs=[
            pl.BlockSpec((BLK, D_MODEL), lambda i: (i, 0)),
            pl.BlockSpec((1, D_MODEL), lambda i: (0, 0)),
            pl.BlockSpec((D_MODEL, 2 * X_W), lambda i: (0, 0)),
        ],
        out_specs=[pl.BlockSpec((BLK, X_W), lambda i: (i, 0))] * 2,
        out_shape=[jax.ShapeDtypeStruct((m, X_W), BF16)] * 2,
        compiler_params=pltpu.CompilerParams(
            dimension_semantics=("arbitrary",), vmem_limit_bytes=VMEM_LIMIT),
        name="mem_proj",
    )(mem2, g_mem, w_kv)


def _head_lane_masks():
    lane = lax.broadcasted_iota(jnp.int32, (BLK, LANES), 1)
    return [(lane >= e * HEAD_DIM) & (lane < (e + 1) * HEAD_DIM) for e in range(HEADS_PER_VREG)]


def _causal_block_mask():
    r = lax.broadcasted_iota(jnp.int32, (BLK, BLK), 0)
    c = lax.broadcasted_iota(jnp.int32, (BLK, BLK), 1)
    return c <= r


def _softmax_pv(s, v):
    m = jnp.max(s, axis=1, keepdims=True)
    p = jnp.exp(s - m)
    l = jnp.sum(p, axis=1, keepdims=True)
    o = jnp.dot(p.astype(BF16), v, preferred_element_type=F32)
    return o / l


def _moba_select_t(q_e, km, n_past):
    rows = 2 * N_BLK
    km16 = jnp.concatenate([km, jnp.zeros((rows - N_BLK, LANES), F32)], axis=0).astype(BF16)
    gate = _nt_dot(km16, q_e)
    blk = lax.broadcasted_iota(jnp.int32, (rows, BLK), 0)
    past = blk < n_past
    gate = jnp.where(past, gate, NEG_INF)
    rank = jnp.zeros((rows, BLK), F32)
    for jp in range(n_past):
        other = gate[jp:jp + 1, :]
        beats = (other > gate) | ((other == gate) & (blk > jp))
        rank = rank + jnp.where(beats, 1.0, 0.0)
    sel = jnp.where(past & (rank < MOBA_TOPK), 1.0, 0.0)
    sel = jnp.concatenate([sel, jnp.zeros((LANES - rows, BLK), F32)], axis=0).astype(BF16)
    r = lax.broadcasted_iota(jnp.int32, (BLK, BLK), 0)
    c = lax.broadcasted_iota(jnp.int32, (BLK, BLK), 1)
    eye = jnp.where(r == c, 1.0, 0.0).astype(BF16)
    return _nt_dot(eye, sel)


def _moba_block(c, q_ref, k_ref, v_ref, km_ref, o_ref):
    nk = (c + 1) * BLK
    q2 = q_ref[...]
    k = k_ref[0:nk, :]
    v = v_ref[0:nk, :]
    causal = _causal_block_mask()
    out = jnp.zeros((BLK, LANES), F32)
    for e, in_head in enumerate(_head_lane_masks()):
        q_e = jnp.where(in_head, q2, jnp.zeros_like(q2))
        s = _nt_dot(q_e, k)
        parts = []
        gated = c > MOBA_TOPK
        if gated:
            sel_t = _moba_select_t(q_e, km_ref[0], c)
        for j in range(c):
            blk = s[:, j * BLK:(j + 1) * BLK]
            if gated:
                blk = jnp.where(sel_t[:, j:j + 1] > 0.5, blk, NEG_INF)
            parts.append(blk)
        parts.append(jnp.where(causal, s[:, c * BLK:nk], NEG_INF))
        s = jnp.concatenate(parts, axis=1) if c else parts[0]
        out = jnp.where(in_head, _softmax_pv(s, v), out)
    o_ref[...] = out


def _moba_body(q_ref, k_ref, v_ref, km_ref, o_ref):
    i = pl.program_id(2)
    for c in range(N_BLK):
        pl.when(i == c)(functools.partial(_moba_block, c, q_ref, k_ref, v_ref, km_ref, o_ref))


def _moba(mq, mk, mv, kmean, batch):
    n_pairs = ATT_W // LANES
    return pl.pallas_call(
        _moba_body,
        grid=(batch, n_pairs, N_BLK),
        in_specs=[
            pl.BlockSpec((BLK, LANES), lambda b, p, i: (b * N_BLK + i, p)),
            pl.BlockSpec((SEQ, LANES), lambda b, p, i: (b, p)),
            pl.BlockSpec((SEQ, LANES), lambda b, p, i: (b, p)),
            pl.BlockSpec((1, N_BLK, LANES), lambda b, p, i: (b, 0, p)),
        ],
        out_specs=pl.BlockSpec((BLK, LANES), lambda b, p, i: (b * N_BLK + i, p)),
        out_shape=jax.ShapeDtypeStruct((batch * SEQ, ATT_W), F32),
        compiler_params=pltpu.CompilerParams(
            dimension_semantics=("arbitrary", "arbitrary", "arbitrary"),
            vmem_limit_bytes=VMEM_LIMIT),
        name="moba",
    )(mq, mk, mv, kmean)


def _key_to_f32(key):
    bits = key ^ ((key >> 31) & 0x7FFFFFFF)
    return lax.bitcast_convert_type(bits, F32)


def _count_ge(sc, thr):
    return jnp.sum(jnp.where(sc >= thr, 1.0, 0.0), axis=1, keepdims=True)


def _dsa_topk_bias(sc, nk):
    kf = float(DSA_TOPK)
    cnt = _count_ge(sc, 0.0)
    t0 = jnp.where(cnt >= kf, 0, INT_MIN).astype(jnp.int32)

    def bit_step(it, t):
        cand = t | jnp.left_shift(jnp.int32(1), 30 - it)
        cnt = _count_ge(sc, _key_to_f32(cand))
        return jnp.where(cnt >= kf, cand, t)

    t = lax.fori_loop(0, 31, bit_step, t0)
    t_val = _key_to_f32(t)
    t_next = _key_to_f32(t + 1)
    need = kf - _count_ge(sc, t_next)
    r = lax.broadcasted_iota(jnp.int32, (BLK, BLK), 0)
    c = lax.broadcasted_iota(jnp.int32, (BLK, BLK), 1)
    strict_upper = jnp.where(r < c, 1.0, 0.0).astype(BF16)
    carry = jnp.zeros((BLK, 1), F32)
    parts = []
    for j in range(nk // BLK):
        sc_j = sc[:, j * BLK:(j + 1) * BLK]
        gt = sc_j >= t_next
        eq = (sc_j >= t_val) & jnp.logical_not(gt)
        eq_f = jnp.where(eq, 1.0, 0.0)
        before = jnp.dot(eq_f.astype(BF16), strict_upper, preferred_element_type=F32) + carry
        carry = carry + jnp.sum(eq_f, axis=1, keepdims=True)
        parts.append(jnp.where(gt | (eq & (before < need)), 0.0, NEG_INF))
    return jnp.concatenate(parts, axis=1)


def _dsa_block(c, iq_ref, ik_ref, iw_ref, q_ref, k_ref, v_ref, o_ref, bias_ref):
    nk = (c + 1) * BLK
    causal = _causal_block_mask()
    head_masks = _head_lane_masks()
    n_pairs = ATT_W // LANES
    if c == 0:
        bias_ref[:, 0:BLK] = jnp.where(causal, 0.0, NEG_INF)
    else:
        ik = ik_ref[0:nk, :]
        wscale = IDX_HEADS ** -0.5
        acc = jnp.zeros((BLK, nk), F32)
        for p in range(n_pairs):
            iq2 = iq_ref[:, p * LANES:(p + 1) * LANES]
            for e, in_head in enumerate(head_masks):
                g = p * HEADS_PER_VREG + e
                logits = _nt_dot(jnp.where(in_head, iq2, jnp.zeros_like(iq2)), ik)
                w = iw_ref[:, g:g + 1] * wscale
                acc = acc + jnp.maximum(logits, 0.0) * w
        own = jnp.where(causal, acc[:, c * BLK:nk], NEG_INF)
        sc = jnp.concatenate([acc[:, 0:c * BLK], own], axis=1)
        bias_ref[:, 0:nk] = _dsa_topk_bias(sc, nk)
    for p in range(n_pairs):
        sl = slice(p * LANES, (p + 1) * LANES)
        q2 = q_ref[:, sl]
        k = k_ref[0:nk, sl]
        v = v_ref[0:nk, sl]
        out = jnp.zeros((BLK, LANES), F32)
        for in_head in head_masks:
            s = _nt_dot(jnp.where(in_head, q2, jnp.zeros_like(q2)), k) + bias_ref[:, 0:nk]
            out = jnp.where(in_head, _softmax_pv(s, v), out)
        o_ref[:, sl] = out


def _dsa_body(iq_ref, ik_ref, iw_ref, q_ref, k_ref, v_ref, o_ref, bias_ref):
    i = pl.program_id(1)
    for c in range(N_BLK):
        pl.when(i == c)(functools.partial(
            _dsa_block, c, iq_ref, ik_ref, iw_ref, q_ref, k_ref, v_ref, o_ref, bias_ref))


def _dsa(iq, ik, iw, dq, dk, dv, batch):
    tile = lambda w: pl.BlockSpec((BLK, w), lambda b, i: (b * N_BLK + i, 0))
    whole = lambda w: pl.BlockSpec((SEQ, w), lambda b, i: (b, 0))
    return pl.pallas_call(
        _dsa_body,
        grid=(batch, N_BLK),
        in_specs=[tile(ATT_W), whole(LANES), tile(LANES), tile(ATT_W), whole(ATT_W), whole(ATT_W)],
        out_specs=tile(ATT_W),
        out_shape=jax.ShapeDtypeStruct((batch * SEQ, ATT_W), F32),
        scratch_shapes=[pltpu.VMEM((BLK, SEQ), F32)],
        compiler_params=pltpu.CompilerParams(
            dimension_semantics=("arbitrary", "arbitrary"), vmem_limit_bytes=VMEM_LIMIT),
        name="dsa",
    )(iq, ik, iw, dq, dk, dv)


def _merge_body(x_ref, ya_ref, mg_ref, yb_ref, dg_ref, xq_ref, xk_ref, xv_ref, xg_ref, gl_ref,
                bm_ref, wa_ref, wb_ref, wc_ref, wo_ref, gf_ref, o_ref):
    xscale = X_HEAD_DIM ** -0.5
    xk = xk_ref[0]
    xv = xv_ref[0]
    yc_parts = []
    for h in range(X_HEADS):
        sl = slice(h * X_HEAD_DIM, (h + 1) * X_HEAD_DIM)
        s = _nt_dot(xq_ref[:, sl], xk[:, sl]) * xscale
        yc_parts.append(_softmax_pv(s, xv[:, sl]))
    yc = jnp.concatenate(yc_parts, axis=1)

    def up(y, gate_ref, w_ref):
        return jnp.dot((y * jax.nn.silu(gate_ref[...])).astype(BF16), w_ref[...],
                       preferred_element_type=F32)

    ya = up(ya_ref[...], mg_ref, wa_ref)
    yb = up(yb_ref[...], dg_ref, wb_ref)
    yc = up(yc, xg_ref, wc_ref)
    u = jnp.zeros((BLK, D_MODEL), F32)
    for n, y in enumerate((ya, yb, yc)):
        sl = slice(n * D_MODEL, (n + 1) * D_MODEL)
        u = u + jax.nn.sigmoid(gl_ref[:, sl] + bm_ref[:, sl]) * y
    y = x_ref[...] + jnp.dot(u.astype(BF16), wo_ref[...], preferred_element_type=F32)
    o_ref[...] = _rmsnorm(y, gf_ref[...])


def _merge(x2, ya, mg, yb, dg, xq, xk, xv, xg, glog, b_merge, wa, wb, wc, wo, g_final, batch):
    tile = lambda w: pl.BlockSpec((BLK, w), lambda b, i: (b * N_BLK + i, 0))
    const = lambda r, w: pl.BlockSpec((r, w), lambda b, i: (0, 0))
    mem = pl.BlockSpec((1, MEM_LEN, X_W), lambda b, i: (b, 0, 0))
    return pl.pallas_call(
        _merge_body,
        grid=(batch, N_BLK),
        in_specs=[tile(D_MODEL), tile(ATT_W), tile(ATT_W), tile(ATT_W), tile(ATT_W), tile(X_W),
                  mem, mem, tile(X_W), tile(N_BRANCH * D_MODEL), const(1, N_BRANCH * D_MODEL),
                  const(ATT_W, D_MODEL), const(ATT_W, D_MODEL), const(X_W, D_MODEL),
                  const(D_MODEL, D_MODEL), const(1, D_MODEL)],
        out_specs=tile(D_MODEL),
        out_shape=jax.ShapeDtypeStruct((batch * SEQ, D_MODEL), F32),
        compiler_params=pltpu.CompilerParams(
            dimension_semantics=("arbitrary", "arbitrary"), vmem_limit_bytes=VMEM_LIMIT),
        name="merge",
    )(x2, ya, mg, yb, dg, xq, xk, xv, xg, glog, b_merge, wa, wb, wc, wo, g_final)


def _pack_w_in(w_in):
    offs = [0]
    for s in IN_SIZES:
        offs.append(offs[-1] + s)
    col = lambda n: w_in[:, offs[n]:offs[n + 1]]
    ik = col(9)
    iw = jnp.pad(col(10), ((0, 0), (0, LANES - IDX_HEADS)))
    parts = [col(n) for n in range(9)] + [ik, ik, iw, col(11), col(12), col(13)]
    return jnp.concatenate(parts, axis=1).astype(BF16)


def _rope_tables():
    half = HEAD_DIM // 2
    inv = jnp.power(ROPE_THETA, -jnp.arange(half, dtype=F32) * 2.0 / HEAD_DIM)
    ang = jnp.arange(SEQ).astype(F32)[:, None] * inv[None, :]
    cos = jnp.cos(ang)
    sin = jnp.sin(ang)
    cos_t = jnp.tile(cos, (1, LANES // half))
    sin_t = jnp.tile(jnp.concatenate([-sin, sin], axis=1), (1, HEADS_PER_VREG))
    return cos_t, sin_t


def _layer(x2, mem2, g_in, w_in, b_merge, g_mem, w_mem_kv, w_up_moba, w_up_dsa, w_up_cross, w_out,
           g_final, batch):
    cos_t, sin_t = _rope_tables()
    (mq, mk, mv, mg, dq, dk, dv, dg, iq, ik, iw, xq, xg, glog, kmean) = _in_proj(
        x2, g_in[None, :], _pack_w_in(w_in), cos_t, sin_t)
    xk, xv = _mem_proj(mem2, g_mem[None, :], w_mem_kv.astype(BF16))
    ya = _moba(mq, mk, mv, kmean.reshape(batch, N_BLK, ATT_W), batch)
    yb = _dsa(iq, ik, iw, dq, dk, dv, batch)
    return _merge(x2, ya, mg, yb, dg, xq, xk.reshape(batch, MEM_LEN, X_W),
                  xv.reshape(batch, MEM_LEN, X_W), xg, glog, b_merge[None, :],
                  w_up_moba.astype(BF16), w_up_dsa.astype(BF16), w_up_cross.astype(BF16),
                  w_out.astype(BF16), g_final, batch)


def kernel(x, mem, g_in, w_in, b_merge, g_mem, w_mem_kv, w_up_moba, w_up_dsa, w_up_cross, w_out,
           g_final):
    batch, seq, d = x.shape
    assert seq == SEQ and d == D_MODEL and mem.shape[1] == MEM_LEN
    assert g_in.shape[0] == 1
    out = _layer(x.reshape(batch * seq, d), mem.reshape(batch * MEM_LEN, d), g_in[0], w_in[0],
                 b_merge[0], g_mem[0], w_mem_kv[0], w_up_moba[0], w_up_dsa[0], w_up_cross[0],
                 w_out[0], g_final[None, :], batch)
    return out.reshape(batch, seq, d)
```

```python
import functools

import jax
import jax.numpy as jnp
from jax import lax
from jax.experimental import pallas as pl
from jax.experimental.pallas import tpu as pltpu

D_MODEL = 1024
SEQ = 2048
HEAD_DIM = 64
N_HEADS = 8
ATT_W = N_HEADS * HEAD_DIM
BLK = 256
N_BLK = SEQ // BLK
MOBA_TOPK = 3
DSA_TOPK = 256
IDX_HEADS = 8
MEM_LEN = 256
X_HEADS = 4
X_HEAD_DIM = 128
X_W = X_HEADS * X_HEAD_DIM
N_BRANCH = 3
ROPE_THETA = 10000.0
RMS_EPS = 1e-6
LANES = 128
HEADS_PER_VREG = LANES // HEAD_DIM

IN_SIZES = (ATT_W, ATT_W, ATT_W, ATT_W, ATT_W, ATT_W, ATT_W, ATT_W,
            IDX_HEADS * HEAD_DIM, HEAD_DIM, IDX_HEADS, X_W, X_W, N_BRANCH * D_MODEL)

OFF_MQ, OFF_MK, OFF_MV, OFF_MG = 0, 512, 1024, 1536
OFF_DQ, OFF_DK, OFF_DV, OFF_DG = 2048, 2560, 3072, 3584
OFF_IQ = 4096
OFF_IK = 4608
OFF_IW = 4736
OFF_XQ = 4864
OFF_XG = 5376
OFF_GL = 5888
PACKED_W = OFF_GL + N_BRANCH * D_MODEL

VMEM_LIMIT = 56 * 1024 * 1024

F32 = jnp.float32
BF16 = jnp.bfloat16
NEG_INF = float("-inf")
INT_MIN = -2 ** 31


def _nt_dot(a, b):
    return lax.dot_general(a, b, (((1,), (1,)), ((), ())), preferred_element_type=F32)


def _rmsnorm(x, g):
    ms = jnp.mean(x * x, axis=-1, keepdims=True)
    return (x * lax.rsqrt(ms + RMS_EPS)) * g


def _in_proj_body(x_ref, g_ref, w_ref, cos_ref, sin_ref,
                  mq_ref, mk_ref, mv_ref, mg_ref, dq_ref, dk_ref, dv_ref, dg_ref,
                  iq_ref, ik_ref, iw_ref, xq_ref, xg_ref, gl_ref, km_ref):
    h = _rmsnorm(x_ref[...], g_ref[...]).astype(BF16)
    cos = cos_ref[...]
    sin = sin_ref[...]
    lane = lax.broadcasted_iota(jnp.int32, (BLK, LANES), 1)
    first_half = (lane & (HEAD_DIM // 2)) == 0

    def seg(off, width):
        return jnp.dot(h, w_ref[:, off:off + width], preferred_element_type=F32)

    def rope(z):
        partner = jnp.where(first_half, pltpu.roll(z, LANES - HEAD_DIM // 2, 1),
                            pltpu.roll(z, HEAD_DIM // 2, 1))
        return z * cos + partner * sin

    def store_roped(ref, off, width, scale, mean_ref=None):
        z = seg(off, width)
        for c in range(width // LANES):
            r = rope(z[:, c * LANES:(c + 1) * LANES])
            if mean_ref is not None:
                mean_ref[0, :, c * LANES:(c + 1) * LANES] = jnp.mean(r, axis=0, keepdims=True)
            if scale is not None:
                r = r * scale
            ref[:, c * LANES:(c + 1) * LANES] = r.astype(ref.dtype)

    qk_scale = HEAD_DIM ** -0.5
    store_roped(mq_ref, OFF_MQ, ATT_W, qk_scale)
    store_roped(mk_ref, OFF_MK, ATT_W, None, mean_ref=km_ref)
    mv_ref[...] = seg(OFF_MV, ATT_W).astype(BF16)
    mg_ref[...] = seg(OFF_MG, ATT_W)
    store_roped(dq_ref, OFF_DQ, ATT_W, qk_scale)
    store_roped(dk_ref, OFF_DK, ATT_W, None)
    dv_ref[...] = seg(OFF_DV, ATT_W).astype(BF16)
    dg_ref[...] = seg(OFF_DG, ATT_W)
    store_roped(iq_ref, OFF_IQ, ATT_W, qk_scale)
    store_roped(ik_ref, OFF_IK, LANES, None)
    iw_ref[...] = seg(OFF_IW, LANES)
    xq_ref[...] = seg(OFF_XQ, X_W).astype(BF16)
    xg_ref[...] = seg(OFF_XG, X_W)
    for c in range(N_BRANCH):
        gl_ref[:, c * D_MODEL:(c + 1) * D_MODEL] = seg(OFF_GL + c * D_MODEL, D_MODEL)


def _in_proj(x2, g_in, w_packed, cos_t, sin_t):
    m = x2.shape[0]
    n_tiles = m // BLK
    row = lambda w: pl.BlockSpec((BLK, w), lambda i: (i, 0))
    out_shape = [
        jax.ShapeDtypeStruct((m, ATT_W), BF16),
        jax.ShapeDtypeStruct((m, ATT_W), BF16),
        jax.ShapeDtypeStruct((m, ATT_W), BF16),
        jax.ShapeDtypeStruct((m, ATT_W), F32),
        jax.ShapeDtypeStruct((m, ATT_W), BF16),
        jax.ShapeDtypeStruct((m, ATT_W), BF16),
        jax.ShapeDtypeStruct((m, ATT_W), BF16),
        jax.ShapeDtypeStruct((m, ATT_W), F32),
        jax.ShapeDtypeStruct((m, ATT_W), BF16),
        jax.ShapeDtypeStruct((m, LANES), BF16),
        jax.ShapeDtypeStruct((m, LANES), F32),
        jax.ShapeDtypeStruct((m, X_W), BF16),
        jax.ShapeDtypeStruct((m, X_W), F32),
        jax.ShapeDtypeStruct((m, N_BRANCH * D_MODEL), F32),
        jax.ShapeDtypeStruct((n_tiles, 1, ATT_W), F32),
    ]
    out_specs = [row(ATT_W)] * 9 + [row(LANES), row(LANES), row(X_W), row(X_W),
                                    row(N_BRANCH * D_MODEL),
                                    pl.BlockSpec((1, 1, ATT_W), lambda i: (i, 0, 0))]
    return pl.pallas_call(
        _in_proj_body,
        grid=(n_tiles,),
        in_specs=[
            pl.BlockSpec((BLK, D_MODEL), lambda i: (i, 0)),
            pl.BlockSpec((1, D_MODEL), lambda i: (0, 0)),
            pl.BlockSpec((D_MODEL, PACKED_W), lambda i: (0, 0), pipeline_mode=pl.Buffered(1)),
            pl.BlockSpec((BLK, LANES), lambda i: (i % N_BLK, 0)),
            pl.BlockSpec((BLK, LANES), lambda i: (i % N_BLK, 0)),
        ],
        out_specs=out_specs,
        out_shape=out_shape,
        compiler_params=pltpu.CompilerParams(
            dimension_semantics=("arbitrary",), vmem_limit_bytes=VMEM_LIMIT),
        name="in_proj",
    )(x2, g_in, w_packed, cos_t, sin_t)


def _mem_proj_body(m_ref, g_ref, w_ref, k_ref, v_ref):
    h = _rmsnorm(m_ref[...], g_ref[...]).astype(BF16)
    k_ref[...] = jnp.dot(h, w_ref[:, 0:X_W], preferred_element_type=F32).astype(BF16)
    v_ref[...] = jnp.dot(h, w_ref[:, X_W:2 * X_W], preferred_element_type=F32).astype(BF16)


def _mem_proj(mem2, g_mem, w_kv):
    m = mem2.shape[0]
    return pl.pallas_call(
        _mem_proj_body,
        grid=(m // BLK,),
        in_specs=[
            pl.BlockSpec((BLK, D_MODEL), lambda i: (i, 0)),
            pl.BlockSpec((1, D_MODEL), lambda i: (0, 0)),
            pl.BlockSpec((D_MODEL, 2 * X_W), lambda i: (0, 0)),
        ],
        out_specs=[pl.BlockSpec((BLK, X_W), lambda i: (i, 0))] * 2,
        out_shape=[jax.ShapeDtypeStruct((m, X_W), BF16)] * 2,
        compiler_params=pltpu.CompilerParams(
            dimension_semantics=("arbitrary",), vmem_limit_bytes=VMEM_LIMIT),
        name="mem_proj",
    )(mem2, g_mem, w_kv)


def _head_lane_masks():
    lane = lax.broadcasted_iota(jnp.int32, (BLK, LANES), 1)
    return [(lane >= e * HEAD_DIM) & (lane < (e + 1) * HEAD_DIM) for e in range(HEADS_PER_VREG)]


def _causal_block_mask():
    r = lax.broadcasted_iota(jnp.int32, (BLK, BLK), 0)
    c = lax.broadcasted_iota(jnp.int32, (BLK, BLK), 1)
    return c <= r


def _softmax_pv(s, v):
    m = jnp.max(s, axis=1, keepdims=True)
    p = jnp.exp(s - m)
    l = jnp.sum(p, axis=1, keepdims=True)
    o = jnp.dot(p.astype(BF16), v, preferred_element_type=F32)
    return o / l


def _moba_select_t(q_e, km, n_past):
    rows = 2 * N_BLK
    km16 = jnp.concatenate([km, jnp.zeros((rows - N_BLK, LANES), F32)], axis=0).astype(BF16)
    gate = _nt_dot(km16, q_e)
    blk = lax.broadcasted_iota(jnp.int32, (rows, BLK), 0)
    past = blk < n_past
    gate = jnp.where(past, gate, NEG_INF)
    rank = jnp.zeros((rows, BLK), F32)
    for jp in range(n_past):
        other = gate[jp:jp + 1, :]
        beats = (other > gate) | ((other == gate) & (blk > jp))
        rank = rank + jnp.where(beats, 1.0, 0.0)
    sel = jnp.where(past & (rank < MOBA_TOPK), 1.0, 0.0)
    sel = jnp.concatenate([sel, jnp.zeros((LANES - rows, BLK), F32)], axis=0).astype(BF16)
    r = lax.broadcasted_iota(jnp.int32, (BLK, BLK), 0)
    c = lax.broadcasted_iota(jnp.int32, (BLK, BLK), 1)
    eye = jnp.where(r == c, 1.0, 0.0).astype(BF16)
    return _nt_dot(eye, sel)


def _moba_block(c, q_ref, k_ref, v_ref, km_ref, o_ref):
    nk = (c + 1) * BLK
    q2 = q_ref[...]
    k = k_ref[0:nk, :]
    v = v_ref[0:nk, :]
    causal = _causal_block_mask()
    out = jnp.zeros((BLK, LANES), F32)
    for e, in_head in enumerate(_head_lane_masks()):
        q_e = jnp.where(in_head, q2, jnp.zeros_like(q2))
        s = _nt_dot(q_e, k)
        parts = []
        gated = c > MOBA_TOPK
        if gated:
            sel_t = _moba_select_t(q_e, km_ref[0], c)
        for j in range(c):
            blk = s[:, j * BLK:(j + 1) * BLK]
            if gated:
                blk = jnp.where(sel_t[:, j:j + 1] > 0.5, blk, NEG_INF)
            parts.append(blk)
        parts.append(jnp.where(causal, s[:, c * BLK:nk], NEG_INF))
        s = jnp.concatenate(parts, axis=1) if c else parts[0]
        out = jnp.where(in_head, _softmax_pv(s, v), out)
    o_ref[...] = out


def _moba_body(q_ref, k_ref, v_ref, km_ref, o_ref):
    i = pl.program_id(2)
    for c in range(N_BLK):
        pl.when(i == c)(functools.partial(_moba_block, c, q_ref, k_ref, v_ref, km_ref, o_ref))


def _moba(mq, mk, mv, kmean, batch):
    n_pairs = ATT_W // LANES
    return pl.pallas_call(
        _moba_body,
        grid=(batch, n_pairs, N_BLK),
        in_specs=[
            pl.BlockSpec((BLK, LANES), lambda b, p, i: (b * N_BLK + i, p)),
            pl.BlockSpec((SEQ, LANES), lambda b, p, i: (b, p)),
            pl.BlockSpec((SEQ, LANES), lambda b, p, i: (b, p)),
            pl.BlockSpec((1, N_BLK, LANES), lambda b, p, i: (b, 0, p)),
        ],
        out_specs=pl.BlockSpec((BLK, LANES), lambda b, p, i: (b * N_BLK + i, p)),
        out_shape=jax.ShapeDtypeStruct((batch * SEQ, ATT_W), F32),
        compiler_params=pltpu.CompilerParams(
            dimension_semantics=("arbitrary", "arbitrary", "arbitrary"),
            vmem_limit_bytes=VMEM_LIMIT),
        name="moba",
    )(mq, mk, mv, kmean)


def _key_to_f32(key):
    bits = key ^ ((key >> 31) & 0x7FFFFFFF)
    return lax.bitcast_convert_type(bits, F32)


def _count_ge(sc, thr):
    return jnp.sum(jnp.where(sc >= thr, 1.0, 0.0), axis=1, keepdims=True)


def _dsa_topk_bias(sc, nk):
    kf = float(DSA_TOPK)
    cnt = _count_ge(sc, 0.0)
    t0 = jnp.where(cnt >= kf, 0, INT_MIN).astype(jnp.int32)

    def bit_step(it, t):
        cand = t | jnp.left_shift(jnp.int32(1), 30 - it)
        cnt = _count_ge(sc, _key_to_f32(cand))
        return jnp.where(cnt >= kf, cand, t)

    t = lax.fori_loop(0, 31, bit_step, t0)
    t_val = _key_to_f32(t)
    t_next = _key_to_f32(t + 1)
    need = kf - _count_ge(sc, t_next)
    r = lax.broadcasted_iota(jnp.int32, (BLK, BLK), 0)
    c = lax.broadcasted_iota(jnp.int32, (BLK, BLK), 1)
    strict_upper = jnp.where(r < c, 1.0, 0.0).astype(BF16)
    carry = jnp.zeros((BLK, 1), F32)
    parts = []
    for j in range(nk // BLK):
        sc_j = sc[:, j * BLK:(j + 1) * BLK]
        gt = sc_j >= t_next
        eq = (sc_j >= t_val) & jnp.logical_not(gt)
        eq_f = jnp.where(eq, 1.0, 0.0)
        before = jnp.dot(eq_f.astype(BF16), strict_upper, preferred_element_type=F32) + carry
        carry = carry + jnp.sum(eq_f, axis=1, keepdims=True)
        parts.append(jnp.where(gt | (eq & (before < need)), 0.0, NEG_INF))
    return jnp.concatenate(parts, axis=1)


def _dsa_block(c, iq_ref, ik_ref, iw_ref, q_ref, k_ref, v_ref, o_ref, bias_ref):
    nk = (c + 1) * BLK
    causal = _causal_block_mask()
    head_masks = _head_lane_masks()
    n_pairs = ATT_W // LANES
    if c == 0:
        bias_ref[:, 0:BLK] = jnp.where(causal, 0.0, NEG_INF)
    else:
        ik = ik_ref[0:nk, :]
        wscale = IDX_HEADS ** -0.5
        acc = jnp.zeros((BLK, nk), F32)
        for p in range(n_pairs):
            iq2 = iq_ref[:, p * LANES:(p + 1) * LANES]
            for e, in_head in enumerate(head_masks):
                g = p * HEADS_PER_VREG + e
                logits = _nt_dot(jnp.where(in_head, iq2, jnp.zeros_like(iq2)), ik)
                w = iw_ref[:, g:g + 1] * wscale
                acc = acc + jnp.maximum(logits, 0.0) * w
        own = jnp.where(causal, acc[:, c * BLK:nk], NEG_INF)
        sc = jnp.concatenate([acc[:, 0:c * BLK], own], axis=1)
        bias_ref[:, 0:nk] = _dsa_topk_bias(sc, nk)
    for p in range(n_pairs):
        sl = slice(p * LANES, (p + 1) * LANES)
        q2 = q_ref[:, sl]
        k = k_ref[0:nk, sl]
        v = v_ref[0:nk, sl]
        out = jnp.zeros((BLK, LANES), F32)
        for in_head in head_masks:
            s = _nt_dot(jnp.where(in_head, q2, jnp.zeros_like(q2)), k) + bias_ref[:, 0:nk]
            out = jnp.where(in_head, _softmax_pv(s, v), out)
        o_ref[:, sl] = out


def _dsa_body(iq_ref, ik_ref, iw_ref, q_ref, k_ref, v_ref, o_ref, bias_ref):
    i = pl.program_id(0)
    for c in range(N_BLK):
        pl.when(i == c)(functools.partial(
            _dsa_block, c, iq_ref, ik_ref, iw_ref, q_ref, k_ref, v_ref, o_ref, bias_ref))


def _dsa(iq, ik, iw, dq, dk, dv, batch):
    tile = lambda w: pl.BlockSpec((BLK, w), lambda i, b: (b * N_BLK + i, 0))
    whole = lambda w: pl.BlockSpec((SEQ, w), lambda i, b: (b, 0))
    return pl.pallas_call(
        _dsa_body,
        grid=(N_BLK, batch),
        in_specs=[tile(ATT_W), whole(LANES), tile(LANES), tile(ATT_W), whole(ATT_W), whole(ATT_W)],
        out_specs=tile(ATT_W),
        out_shape=jax.ShapeDtypeStruct((batch * SEQ, ATT_W), F32),
        scratch_shapes=[pltpu.VMEM((BLK, SEQ), F32)],
        compiler_params=pltpu.CompilerParams(
            dimension_semantics=("arbitrary", "arbitrary"), vmem_limit_bytes=VMEM_LIMIT),
        name="dsa",
    )(iq, ik, iw, dq, dk, dv)


def _merge_body(x_ref, ya_ref, mg_ref, yb_ref, dg_ref, xq_ref, xk_ref, xv_ref, xg_ref, gl_ref,
                bm_ref, wa_ref, wb_ref, wc_ref, wo_ref, gf_ref, o_ref):
    xscale = X_HEAD_DIM ** -0.5
    xk = xk_ref[0]
    xv = xv_ref[0]
    yc_parts = []
    for h in range(X_HEADS):
        sl = slice(h * X_HEAD_DIM, (h + 1) * X_HEAD_DIM)
        s = _nt_dot(xq_ref[:, sl], xk[:, sl]) * xscale
        yc_parts.append(_softmax_pv(s, xv[:, sl]))
    yc = jnp.concatenate(yc_parts, axis=1)

    def up(y, gate_ref, w_ref):
        return jnp.dot((y * jax.nn.silu(gate_ref[...])).astype(BF16), w_ref[...],
                       preferred_element_type=F32)

    ya = up(ya_ref[...], mg_ref, wa_ref)
    yb = up(yb_ref[...], dg_ref, wb_ref)
    yc = up(yc, xg_ref, wc_ref)
    u = jnp.zeros((BLK, D_MODEL), F32)
    for n, y in enumerate((ya, yb, yc)):
        sl = slice(n * D_MODEL, (n + 1) * D_MODEL)
        u = u + jax.nn.sigmoid(gl_ref[:, sl] + bm_ref[:, sl]) * y
    y = x_ref[...] + jnp.dot(u.astype(BF16), wo_ref[...], preferred_element_type=F32)
    o_ref[...] = _rmsnorm(y, gf_ref[...])


def _merge(x2, ya, mg, yb, dg, xq, xk, xv, xg, glog, b_merge, wa, wb, wc, wo, g_final, batch):
    tile = lambda w: pl.BlockSpec((BLK, w), lambda b, i: (b * N_BLK + i, 0))
    const = lambda r, w: pl.BlockSpec((r, w), lambda b, i: (0, 0))
    mem = pl.BlockSpec((1, MEM_LEN, X_W), lambda b, i: (b, 0, 0))
    return pl.pallas_call(
        _merge_body,
        grid=(batch, N_BLK),
        in_specs=[tile(D_MODEL), tile(ATT_W), tile(ATT_W), tile(ATT_W), tile(ATT_W), tile(X_W),
                  mem, mem, tile(X_W), tile(N_BRANCH * D_MODEL), const(1, N_BRANCH * D_MODEL),
                  const(ATT_W, D_MODEL), const(ATT_W, D_MODEL), const(X_W, D_MODEL),
                  const(D_MODEL, D_MODEL), const(1, D_MODEL)],
        out_specs=tile(D_MODEL),
        out_shape=jax.ShapeDtypeStruct((batch * SEQ, D_MODEL), F32),
        compiler_params=pltpu.CompilerParams(
            dimension_semantics=("arbitrary", "arbitrary"), vmem_limit_bytes=VMEM_LIMIT),
        name="merge",
    )(x2, ya, mg, yb, dg, xq, xk, xv, xg, glog, b_merge, wa, wb, wc, wo, g_final)


def _pack_w_in(w_in):
    offs = [0]
    for s in IN_SIZES:
        offs.append(offs[-1] + s)
    col = lambda n: w_in[:, offs[n]:offs[n + 1]]
    ik = col(9)
    iw = jnp.pad(col(10), ((0, 0), (0, LANES - IDX_HEADS)))
    parts = [col(n) for n in range(9)] + [ik, ik, iw, col(11), col(12), col(13)]
    return jnp.concatenate(parts, axis=1).astype(BF16)


def _rope_tables():
    half = HEAD_DIM // 2
    inv = jnp.power(ROPE_THETA, -jnp.arange(half, dtype=F32) * 2.0 / HEAD_DIM)
    ang = jnp.arange(SEQ).astype(F32)[:, None] * inv[None, :]
    cos = jnp.cos(ang)
    sin = jnp.sin(ang)
    cos_t = jnp.tile(cos, (1, LANES // half))
    sin_t = jnp.tile(jnp.concatenate([-sin, sin], axis=1), (1, HEADS_PER_VREG))
    return cos_t, sin_t


def _layer(x2, mem2, g_in, w_in, b_merge, g_mem, w_mem_kv, w_up_moba, w_up_dsa, w_up_cross, w_out,
           g_final, batch):
    cos_t, sin_t = _rope_tables()
    (mq, mk, mv, mg, dq, dk, dv, dg, iq, ik, iw, xq, xg, glog, kmean) = _in_proj(
        x2, g_in[None, :], _pack_w_in(w_in), cos_t, sin_t)
    xk, xv = _mem_proj(mem2, g_mem[None, :], w_mem_kv.astype(BF16))
    ya = _moba(mq, mk, mv, kmean.reshape(batch, N_BLK, ATT_W), batch)
    yb = _dsa(iq, ik, iw, dq, dk, dv, batch)
    return _merge(x2, ya, mg, yb, dg, xq, xk.reshape(batch, MEM_LEN, X_W),
                  xv.reshape(batch, MEM_LEN, X_W), xg, glog, b_merge[None, :],
                  w_up_moba.astype(BF16), w_up_dsa.astype(BF16), w_up_cross.astype(BF16),
                  w_out.astype(BF16), g_final, batch)


def kernel(x, mem, g_in, w_in, b_merge, g_mem, w_mem_kv, w_up_moba, w_up_dsa, w_up_cross, w_out,
           g_final):
    batch, seq, d = x.shape
    assert seq == SEQ and d == D_MODEL and mem.shape[1] == MEM_LEN
    assert g_in.shape[0] == 1
    out = _layer(x.reshape(batch * seq, d), mem.reshape(batch * MEM_LEN, d), g_in[0], w_in[0],
                 b_merge[0], g_mem[0], w_mem_kv[0], w_up_moba[0], w_up_dsa[0], w_up_cross[0],
                 w_out[0], g_final[None, :], batch)
    return out.reshape(batch, seq, d)
```

```python
import functools

import jax
import jax.numpy as jnp
from jax import lax
from jax.experimental import pallas as pl
from jax.experimental.pallas import tpu as pltpu

D_MODEL = 1024
SEQ = 2048
HEAD_DIM = 64
N_HEADS = 8
ATT_W = N_HEADS * HEAD_DIM
BLK = 256
N_BLK = SEQ // BLK
MOBA_TOPK = 3
DSA_TOPK = 256
IDX_HEADS = 8
MEM_LEN = 256
X_HEADS = 4
X_HEAD_DIM = 128
X_W = X_HEADS * X_HEAD_DIM
N_BRANCH = 3
ROPE_THETA = 10000.0
RMS_EPS = 1e-6
LANES = 128
HEADS_PER_VREG = LANES // HEAD_DIM

IN_SIZES = (ATT_W, ATT_W, ATT_W, ATT_W, ATT_W, ATT_W, ATT_W, ATT_W,
            IDX_HEADS * HEAD_DIM, HEAD_DIM, IDX_HEADS, X_W, X_W, N_BRANCH * D_MODEL)

OFF_MQ, OFF_MK, OFF_MV, OFF_MG = 0, 512, 1024, 1536
OFF_DQ, OFF_DK, OFF_DV, OFF_DG = 2048, 2560, 3072, 3584
OFF_IQ = 4096
OFF_IK = 4608
OFF_IW = 4736
OFF_XQ = 4864
OFF_XG = 5376
OFF_GL = 5888
PACKED_W = OFF_GL + N_BRANCH * D_MODEL

VMEM_LIMIT = 56 * 1024 * 1024

F32 = jnp.float32
BF16 = jnp.bfloat16
NEG_INF = float("-inf")
INT_MIN = -2 ** 31


def _nt_dot(a, b):
    return lax.dot_general(a, b, (((1,), (1,)), ((), ())), preferred_element_type=F32)


def _rmsnorm(x, g):
    ms = jnp.mean(x * x, axis=-1, keepdims=True)
    return (x * lax.rsqrt(ms + RMS_EPS)) * g


def _in_proj_body(x_ref, g_ref, w_ref, cos_ref, sin_ref,
                  mq_ref, mk_ref, mv_ref, mg_ref, dq_ref, dk_ref, dv_ref, dg_ref,
                  iq_ref, ik_ref, iw_ref, xq_ref, xg_ref, gl_ref, km_ref):
    h = _rmsnorm(x_ref[...], g_ref[...]).astype(BF16)
    cos = cos_ref[...]
    sin = sin_ref[...]
    lane = lax.broadcasted_iota(jnp.int32, (BLK, LANES), 1)
    first_half = (lane & (HEAD_DIM // 2)) == 0

    def seg(off, width):
        return jnp.dot(h, w_ref[:, off:off + width], preferred_element_type=F32)

    def rope(z):
        partner = jnp.where(first_half, pltpu.roll(z, LANES - HEAD_DIM // 2, 1),
                            pltpu.roll(z, HEAD_DIM // 2, 1))
        return z * cos + partner * sin

    def store_roped(ref, off, width, scale, mean_ref=None):
        z = seg(off, width)
        for c in range(width // LANES):
            r = rope(z[:, c * LANES:(c + 1) * LANES])
            if mean_ref is not None:
                mean_ref[0, :, c * LANES:(c + 1) * LANES] = jnp.mean(r, axis=0, keepdims=True)
            if scale is not None:
                r = r * scale
            ref[:, c * LANES:(c + 1) * LANES] = r.astype(ref.dtype)

    qk_scale = HEAD_DIM ** -0.5
    store_roped(mq_ref, OFF_MQ, ATT_W, qk_scale)
    store_roped(mk_ref, OFF_MK, ATT_W, None, mean_ref=km_ref)
    mv_ref[...] = seg(OFF_MV, ATT_W).astype(BF16)
    mg_ref[...] = seg(OFF_MG, ATT_W)
    store_roped(dq_ref, OFF_DQ, ATT_W, qk_scale)
    store_roped(dk_ref, OFF_DK, ATT_W, None)
    dv_ref[...] = seg(OFF_DV, ATT_W).astype(BF16)
    dg_ref[...] = seg(OFF_DG, ATT_W)
    store_roped(iq_ref, OFF_IQ, ATT_W, qk_scale)
    store_roped(ik_ref, OFF_IK, LANES, None)
    iw_ref[...] = seg(OFF_IW, LANES)
    xq_ref[...] = seg(OFF_XQ, X_W).astype(BF16)
    xg_ref[...] = seg(OFF_XG, X_W)
    for c in range(N_BRANCH):
        gl_ref[:, c * D_MODEL:(c + 1) * D_MODEL] = seg(OFF_GL + c * D_MODEL, D_MODEL)


def _in_proj(x2, g_in, w_packed, cos_t, sin_t):
    m = x2.shape[0]
    n_tiles = m // BLK
    row = lambda w: pl.BlockSpec((BLK, w), lambda i: (i, 0))
    out_shape = [
        jax.ShapeDtypeStruct((m, ATT_W), BF16),
        jax.ShapeDtypeStruct((m, ATT_W), BF16),
        jax.ShapeDtypeStruct((m, ATT_W), BF16),
        jax.ShapeDtypeStruct((m, ATT_W), F32),
        jax.ShapeDtypeStruct((m, ATT_W), BF16),
        jax.ShapeDtypeStruct((m, ATT_W), BF16),
        jax.ShapeDtypeStruct((m, ATT_W), BF16),
        jax.ShapeDtypeStruct((m, ATT_W), F32),
        jax.ShapeDtypeStruct((m, ATT_W), BF16),
        jax.ShapeDtypeStruct((m, LANES), BF16),
        jax.ShapeDtypeStruct((m, LANES), F32),
        jax.ShapeDtypeStruct((m, X_W), BF16),
        jax.ShapeDtypeStruct((m, X_W), F32),
        jax.ShapeDtypeStruct((m, N_BRANCH * D_MODEL), F32),
        jax.ShapeDtypeStruct((n_tiles, 1, ATT_W), F32),
    ]
    out_specs = [row(ATT_W)] * 9 + [row(LANES), row(LANES), row(X_W), row(X_W),
                                    row(N_BRANCH * D_MODEL),
                                    pl.BlockSpec((1, 1, ATT_W), lambda i: (i, 0, 0))]
    return pl.pallas_call(
        _in_proj_body,
        grid=(n_tiles,),
        in_specs=[
            pl.BlockSpec((BLK, D_MODEL), lambda i: (i, 0)),
            pl.BlockSpec((1, D_MODEL), lambda i: (0, 0)),
            pl.BlockSpec((D_MODEL, PACKED_W), lambda i: (0, 0), pipeline_mode=pl.Buffered(1)),
            pl.BlockSpec((BLK, LANES), lambda i: (i % N_BLK, 0)),
            pl.BlockSpec((BLK, LANES), lambda i: (i % N_BLK, 0)),
        ],
        out_specs=out_specs,
        out_shape=out_shape,
        compiler_params=pltpu.CompilerParams(
            dimension_semantics=("arbitrary",), vmem_limit_bytes=VMEM_LIMIT),
        name="in_proj",
    )(x2, g_in, w_packed, cos_t, sin_t)


def _mem_proj_body(m_ref, g_ref, w_ref, k_ref, v_ref):
    h = _rmsnorm(m_ref[...], g_ref[...]).astype(BF16)
    k_ref[...] = jnp.dot(h, w_ref[:, 0:X_W], preferred_element_type=F32).astype(BF16)
    v_ref[...] = jnp.dot(h, w_ref[:, X_W:2 * X_W], preferred_element_type=F32).astype(BF16)


def _mem_proj(mem2, g_mem, w_kv):
    m = mem2.shape[0]
    return pl.pallas_call(
        _mem_proj_body,
        grid=(m // BLK,),
        in_specs=[
            pl.BlockSpec((BLK, D_MODEL), lambda i: (i, 0)),
            pl.BlockSpec((1, D_MODEL), lambda i: (0, 0)),
            pl.BlockSpec((D_MODEL, 2 * X_W), lambda i: (0, 0)),
        ],
        out_specs=[pl.BlockSpec((BLK, X_W), lambda i: (i, 0))] * 2,
        out_shape=[jax.ShapeDtypeStruct((m, X_W), BF16)] * 2,
        compiler_params=pltpu.CompilerParams(
            dimension_semantics=("arbitrary",), vmem_limit_bytes=VMEM_LIMIT),
        name="mem_proj",
    )(mem2, g_mem, w_kv)


def _head_lane_masks():
    lane = lax.broadcasted_iota(jnp.int32, (BLK, LANES), 1)
    return [(lane >= e * HEAD_DIM) & (lane < (e + 1) * HEAD_DIM) for e in range(HEADS_PER_VREG)]


def _causal_block_mask():
    r = lax.broadcasted_iota(jnp.int32, (BLK, BLK), 0)
    c = lax.broadcasted_iota(jnp.int32, (BLK, BLK), 1)
    return c <= r


def _softmax_pv(s, v):
    m = jnp.max(s, axis=1, keepdims=True)
    p = jnp.exp(s - m)
    l = jnp.sum(p, axis=1, keepdims=True)
    o = jnp.dot(p.astype(BF16), v, preferred_element_type=F32)
    return o / l


def _moba_select_t(q_e, km, n_past):
    rows = 2 * N_BLK
    km16 = jnp.concatenate([km, jnp.zeros((rows - N_BLK, LANES), F32)], axis=0).astype(BF16)
    gate = _nt_dot(km16, q_e)
    blk = lax.broadcasted_iota(jnp.int32, (rows, BLK), 0)
    past = blk < n_past
    gate = jnp.where(past, gate, NEG_INF)
    rank = jnp.zeros((rows, BLK), F32)
    for jp in range(n_past):
        other = gate[jp:jp + 1, :]
        beats = (other > gate) | ((other == gate) & (blk > jp))
        rank = rank + jnp.where(beats, 1.0, 0.0)
    sel = jnp.where(past & (rank < MOBA_TOPK), 1.0, 0.0)
    sel = jnp.concatenate([sel, jnp.zeros((LANES - rows, BLK), F32)], axis=0).astype(BF16)
    r = lax.broadcasted_iota(jnp.int32, (BLK, BLK), 0)
    c = lax.broadcasted_iota(jnp.int32, (BLK, BLK), 1)
    eye = jnp.where(r == c, 1.0, 0.0).astype(BF16)
    return _nt_dot(eye, sel)


def _moba_block(c, q_ref, k_ref, v_ref, km_ref, o_ref):
    nk = (c + 1) * BLK
    q2 = q_ref[...]
    k = k_ref[0:nk, :]
    v = v_ref[0:nk, :]
    causal = _causal_block_mask()
    out = jnp.zeros((BLK, LANES), F32)
    for e, in_head in enumerate(_head_lane_masks()):
        q_e = jnp.where(in_head, q2, jnp.zeros_like(q2))
        s = _nt_dot(q_e, k)
        parts = []
        gated = c > MOBA_TOPK
        if gated:
            sel_t = _moba_select_t(q_e, km_ref[...], c)
        for j in range(c):
            blk = s[:, j * BLK:(j + 1) * BLK]
            if gated:
                blk = jnp.where(sel_t[:, j:j + 1] > 0.5, blk, NEG_INF)
            parts.append(blk)
        parts.append(jnp.where(causal, s[:, c * BLK:nk], NEG_INF))
        s = jnp.concatenate(parts, axis=1) if c else parts[0]
        out = jnp.where(in_head, _softmax_pv(s, v), out)
    o_ref[...] = out


def _moba_qblock(c, mq, mk, mv, kmean, batch):
    nk = (c + 1) * BLK
    tile = pl.BlockSpec((None, BLK, LANES), lambda b, p: (b, c, p))
    keys = pl.BlockSpec((None, nk, LANES), lambda b, p: (b, 0, p))
    return pl.pallas_call(
        functools.partial(_moba_block, c),
        grid=(batch, ATT_W // LANES),
        in_specs=[tile, keys, keys, pl.BlockSpec((None, N_BLK, LANES), lambda b, p: (b, 0, p))],
        out_specs=pl.BlockSpec((None, BLK, LANES), lambda b, p: (b, 0, p)),
        out_shape=jax.ShapeDtypeStruct((batch, BLK, ATT_W), F32),
        compiler_params=pltpu.CompilerParams(
            dimension_semantics=("arbitrary", "arbitrary"), vmem_limit_bytes=VMEM_LIMIT),
        name=f"moba_q{c}",
    )(mq, mk, mv, kmean)


def _moba(mq, mk, mv, kmean, batch):
    as3d = lambda a: a.reshape(batch, SEQ, ATT_W)
    blocks = [_moba_qblock(c, as3d(mq), as3d(mk), as3d(mv), kmean, batch) for c in range(N_BLK)]
    return jnp.stack(blocks, axis=1).reshape(batch * SEQ, ATT_W)


def _key_to_f32(key):
    bits = key ^ ((key >> 31) & 0x7FFFFFFF)
    return lax.bitcast_convert_type(bits, F32)


def _count_ge(sc, thr):
    return jnp.sum(jnp.where(sc >= thr, 1.0, 0.0), axis=1, keepdims=True)


def _dsa_topk_bias(sc, nk):
    kf = float(DSA_TOPK)
    cnt = _count_ge(sc, 0.0)
    t0 = jnp.where(cnt >= kf, 0, INT_MIN).astype(jnp.int32)

    def bit_step(it, t):
        cand = t | jnp.left_shift(jnp.int32(1), 30 - it)
        cnt = _count_ge(sc, _key_to_f32(cand))
        return jnp.where(cnt >= kf, cand, t)

    t = lax.fori_loop(0, 31, bit_step, t0)
    t_val = _key_to_f32(t)
    t_next = _key_to_f32(t + 1)
    need = kf - _count_ge(sc, t_next)
    r = lax.broadcasted_iota(jnp.int32, (BLK, BLK), 0)
    c = lax.broadcasted_iota(jnp.int32, (BLK, BLK), 1)
    strict_upper = jnp.where(r < c, 1.0, 0.0).astype(BF16)
    carry = jnp.zeros((BLK, 1), F32)
    parts = []
    for j in range(nk // BLK):
        sc_j = sc[:, j * BLK:(j + 1) * BLK]
        gt = sc_j >= t_next
        eq = (sc_j >= t_val) & jnp.logical_not(gt)
        eq_f = jnp.where(eq, 1.0, 0.0)
        before = jnp.dot(eq_f.astype(BF16), strict_upper, preferred_element_type=F32) + carry
        carry = carry + jnp.sum(eq_f, axis=1, keepdims=True)
        parts.append(jnp.where(gt | (eq & (before < need)), 0.0, NEG_INF))
    return jnp.concatenate(parts, axis=1)


def _dsa_block(c, iq_ref, ik_ref, iw_ref, q_ref, k_ref, v_ref, o_ref, bias_ref):
    nk = (c + 1) * BLK
    causal = _causal_block_mask()
    head_masks = _head_lane_masks()
    n_pairs = ATT_W // LANES
    if c == 0:
        bias_ref[:, 0:BLK] = jnp.where(causal, 0.0, NEG_INF)
    else:
        ik = ik_ref[0:nk, :]
        wscale = IDX_HEADS ** -0.5
        acc = jnp.zeros((BLK, nk), F32)
        for p in range(n_pairs):
            iq2 = iq_ref[:, p * LANES:(p + 1) * LANES]
            for e, in_head in enumerate(head_masks):
                g = p * HEADS_PER_VREG + e
                logits = _nt_dot(jnp.where(in_head, iq2, jnp.zeros_like(iq2)), ik)
                w = iw_ref[:, g:g + 1] * wscale
                acc = acc + jnp.maximum(logits, 0.0) * w
        own = jnp.where(causal, acc[:, c * BLK:nk], NEG_INF)
        sc = jnp.concatenate([acc[:, 0:c * BLK], own], axis=1)
        bias_ref[:, 0:nk] = _dsa_topk_bias(sc, nk)
    for p in range(n_pairs):
        sl = slice(p * LANES, (p + 1) * LANES)
        q2 = q_ref[:, sl]
        k = k_ref[0:nk, sl]
        v = v_ref[0:nk, sl]
        out = jnp.zeros((BLK, LANES), F32)
        for in_head in head_masks:
            s = _nt_dot(jnp.where(in_head, q2, jnp.zeros_like(q2)), k) + bias_ref[:, 0:nk]
            out = jnp.where(in_head, _softmax_pv(s, v), out)
        o_ref[:, sl] = out


def _dsa_qblock(c, iq, ik, iw, dq, dk, dv, batch):
    nk = (c + 1) * BLK
    tile = lambda w: pl.BlockSpec((None, BLK, w), lambda b: (b, c, 0))
    keys = lambda w: pl.BlockSpec((None, nk, w), lambda b: (b, 0, 0))
    return pl.pallas_call(
        functools.partial(_dsa_block, c),
        grid=(batch,),
        in_specs=[tile(ATT_W), keys(LANES), tile(LANES), tile(ATT_W), keys(ATT_W), keys(ATT_W)],
        out_specs=pl.BlockSpec((None, BLK, ATT_W), lambda b: (b, 0, 0)),
        out_shape=jax.ShapeDtypeStruct((batch, BLK, ATT_W), F32),
        scratch_shapes=[pltpu.VMEM((BLK, nk), F32)],
        compiler_params=pltpu.CompilerParams(
            dimension_semantics=("arbitrary",), vmem_limit_bytes=VMEM_LIMIT),
        name=f"dsa_q{c}",
    )(iq, ik, iw, dq, dk, dv)


def _dsa(iq, ik, iw, dq, dk, dv, batch):
    as3d = lambda a: a.reshape(batch, SEQ, a.shape[-1])
    args = [as3d(a) for a in (iq, ik, iw, dq, dk, dv)]
    blocks = [_dsa_qblock(c, *args, batch) for c in range(N_BLK)]
    return jnp.stack(blocks, axis=1).reshape(batch * SEQ, ATT_W)


def _merge_body(x_ref, ya_ref, mg_ref, yb_ref, dg_ref, xq_ref, xk_ref, xv_ref, xg_ref, gl_ref,
                bm_ref, wa_ref, wb_ref, wc_ref, wo_ref, gf_ref, o_ref):
    xscale = X_HEAD_DIM ** -0.5
    xk = xk_ref[0]
    xv = xv_ref[0]
    yc_parts = []
    for h in range(X_HEADS):
        sl = slice(h * X_HEAD_DIM, (h + 1) * X_HEAD_DIM)
        s = _nt_dot(xq_ref[:, sl], xk[:, sl]) * xscale
        yc_parts.append(_softmax_pv(s, xv[:, sl]))
    yc = jnp.concatenate(yc_parts, axis=1)

    def up(y, gate_ref, w_ref):
        return jnp.dot((y * jax.nn.silu(gate_ref[...])).astype(BF16), w_ref[...],
                       preferred_element_type=F32)

    ya = up(ya_ref[...], mg_ref, wa_ref)
    yb = up(yb_ref[...], dg_ref, wb_ref)
    yc = up(yc, xg_ref, wc_ref)
    u = jnp.zeros((BLK, D_MODEL), F32)
    for n, y in enumerate((ya, yb, yc)):
        sl = slice(n * D_MODEL, (n + 1) * D_MODEL)
        u = u + jax.nn.sigmoid(gl_ref[:, sl] + bm_ref[:, sl]) * y
    y = x_ref[...] + jnp.dot(u.astype(BF16), wo_ref[...], preferred_element_type=F32)
    o_ref[...] = _rmsnorm(y, gf_ref[...])


def _merge(x2, ya, mg, yb, dg, xq, xk, xv, xg, glog, b_merge, wa, wb, wc, wo, g_final, batch):
    tile = lambda w: pl.BlockSpec((BLK, w), lambda b, i: (b * N_BLK + i, 0))
    const = lambda r, w: pl.BlockSpec((r, w), lambda b, i: (0, 0))
    mem = pl.BlockSpec((1, MEM_LEN, X_W), lambda b, i: (b, 0, 0))
    return pl.pallas_call(
        _merge_body,
        grid=(batch, N_BLK),
        in_specs=[tile(D_MODEL), tile(ATT_W), tile(ATT_W), tile(ATT_W), tile(ATT_W), tile(X_W),
                  mem, mem, tile(X_W), tile(N_BRANCH * D_MODEL), const(1, N_BRANCH * D_MODEL),
                  const(ATT_W, D_MODEL), const(ATT_W, D_MODEL), const(X_W, D_MODEL),
                  const(D_MODEL, D_MODEL), const(1, D_MODEL)],
        out_specs=tile(D_MODEL),
        out_shape=jax.ShapeDtypeStruct((batch * SEQ, D_MODEL), F32),
        compiler_params=pltpu.CompilerParams(
            dimension_semantics=("arbitrary", "arbitrary"), vmem_limit_bytes=VMEM_LIMIT),
        name="merge",
    )(x2, ya, mg, yb, dg, xq, xk, xv, xg, glog, b_merge, wa, wb, wc, wo, g_final)


def _pack_w_in(w_in):
    offs = [0]
    for s in IN_SIZES:
        offs.append(offs[-1] + s)
    col = lambda n: w_in[:, offs[n]:offs[n + 1]]
    ik = col(9)
    iw = jnp.pad(col(10), ((0, 0), (0, LANES - IDX_HEADS)))
    parts = [col(n) for n in range(9)] + [ik, ik, iw, col(11), col(12), col(13)]
    return jnp.concatenate(parts, axis=1).astype(BF16)


def _rope_tables():
    half = HEAD_DIM // 2
    inv = jnp.power(ROPE_THETA, -jnp.arange(half, dtype=F32) * 2.0 / HEAD_DIM)
    ang = jnp.arange(SEQ).astype(F32)[:, None] * inv[None, :]
    cos = jnp.cos(ang)
    sin = jnp.sin(ang)
    cos_t = jnp.tile(cos, (1, LANES // half))
    sin_t = jnp.tile(jnp.concatenate([-sin, sin], axis=1), (1, HEADS_PER_VREG))
    return cos_t, sin_t


def _layer(x2, mem2, g_in, w_in, b_merge, g_mem, w_mem_kv, w_up_moba, w_up_dsa, w_up_cross, w_out,
           g_final, batch):
    cos_t, sin_t = _rope_tables()
    (mq, mk, mv, mg, dq, dk, dv, dg, iq, ik, iw, xq, xg, glog, kmean) = _in_proj(
        x2, g_in[None, :], _pack_w_in(w_in), cos_t, sin_t)
    xk, xv = _mem_proj(mem2, g_mem[None, :], w_mem_kv.astype(BF16))
    ya = _moba(mq, mk, mv, kmean.reshape(batch, N_BLK, ATT_W), batch)
    yb = _dsa(iq, ik, iw, dq, dk, dv, batch)
    return _merge(x2, ya, mg, yb, dg, xq, xk.reshape(batch, MEM_LEN, X_W),
                  xv.reshape(batch, MEM_LEN, X_W), xg, glog, b_merge[None, :],
                  w_up_moba.astype(BF16), w_up_dsa.astype(BF16), w_up_cross.astype(BF16),
                  w_out.astype(BF16), g_final, batch)


def kernel(x, mem, g_in, w_in, b_merge, g_mem, w_mem_kv, w_up_moba, w_up_dsa, w_up_cross, w_out,
           g_final):
    batch, seq, d = x.shape
    assert seq == SEQ and d == D_MODEL and mem.shape[1] == MEM_LEN
    assert g_in.shape[0] == 1
    out = _layer(x.reshape(batch * seq, d), mem.reshape(batch * MEM_LEN, d), g_in[0], w_in[0],
                 b_merge[0], g_mem[0], w_mem_kv[0], w_up_moba[0], w_up_dsa[0], w_up_cross[0],
                 w_out[0], g_final[None, :], batch)
    return out.reshape(batch, seq, d)
```

```python
import functools

import jax
import jax.numpy as jnp
from jax import lax
from jax.experimental import pallas as pl
from jax.experimental.pallas import tpu as pltpu

D_MODEL = 1024
SEQ = 2048
HEAD_DIM = 64
HALF = HEAD_DIM // 2
N_HEADS = 8
ATT_W = N_HEADS * HEAD_DIM
BLK = 256
N_BLK = SEQ // BLK
MOBA_TOPK = 3
DSA_TOPK = 256
IDX_HEADS = 8
MEM_LEN = 256
X_HEADS = 4
X_HEAD_DIM = 128
X_W = X_HEADS * X_HEAD_DIM
N_BRANCH = 3
ROPE_THETA = 10000.0
RMS_EPS = 1e-6
LANES = 128
HEADS_PER_PAIR = LANES // HEAD_DIM
N_PAIRS = ATT_W // LANES
BF16_ROWS = 16

IN_SIZES = (ATT_W, ATT_W, ATT_W, ATT_W, ATT_W, ATT_W, ATT_W, ATT_W,
            IDX_HEADS * HEAD_DIM, HEAD_DIM, IDX_HEADS, X_W, X_W, N_BRANCH * D_MODEL)

COL_MK, COL_DK, COL_IK = 0, 512, 1024
COL_MG, COL_DG, COL_XQ, COL_XG, COL_GL = 1152, 1664, 2176, 2688, 3200
N_COLS = COL_GL + N_BRANCH * D_MODEL
ROW_MQ, ROW_DQ, ROW_IQ, ROW_MV, ROW_DV, ROW_IW = 0, 512, 1024, 1536, 2048, 2560
N_ROWS = ROW_IW + BF16_ROWS

VMEM_LIMIT = 56 * 1024 * 1024

F32 = jnp.float32
BF16 = jnp.bfloat16
NEG_INF = float("-inf")
INT_MIN = -2 ** 31


def _dot(a, b):
    return jnp.dot(a, b, preferred_element_type=F32)


def _nt_dot(a, b):
    return lax.dot_general(a, b, (((1,), (1,)), ((), ())), preferred_element_type=F32)


def _rmsnorm(x, g):
    ms = jnp.mean(x * x, axis=-1, keepdims=True)
    return (x * lax.rsqrt(ms + RMS_EPS)) * g


def _in_proj_body(x_ref, g_ref, wc_ref, wr_ref, cos_ref, sin_ref, cos_t_ref, sin_t_ref,
                  mqT_ref, dqT_ref, iqT_ref, mvT_ref, dvT_ref, iwT_ref,
                  mk_ref, dk_ref, ik_ref, mg_ref, dg_ref, xq_ref, xg_ref, gl_ref, km_ref):
    h = _rmsnorm(x_ref[...], g_ref[...]).astype(BF16)
    qk_scale = HEAD_DIM ** -0.5

    zt = _nt_dot(wr_ref[...], h)
    cos_t = cos_t_ref[...]
    sin_t = sin_t_ref[...]

    def store_roped_t(ref, row0):
        for hd in range(N_HEADS):
            r = row0 + hd * HEAD_DIM
            x1 = zt[r:r + HALF]
            x2 = zt[r + HALF:r + HEAD_DIM]
            o = hd * HEAD_DIM
            ref[o:o + HALF, :] = ((x1 * cos_t - x2 * sin_t) * qk_scale).astype(BF16)
            ref[o + HALF:o + HEAD_DIM, :] = ((x2 * cos_t + x1 * sin_t) * qk_scale).astype(BF16)

    store_roped_t(mqT_ref, ROW_MQ)
    store_roped_t(dqT_ref, ROW_DQ)
    store_roped_t(iqT_ref, ROW_IQ)
    mvT_ref[...] = zt[ROW_MV:ROW_MV + ATT_W].astype(BF16)
    dvT_ref[...] = zt[ROW_DV:ROW_DV + ATT_W].astype(BF16)
    iwT_ref[...] = zt[ROW_IW:ROW_IW + BF16_ROWS]

    cos = cos_ref[...]
    sin = sin_ref[...]
    lane = lax.broadcasted_iota(jnp.int32, (BLK, LANES), 1)
    first_half = (lane & HALF) == 0

    def seg(off, width):
        return _dot(h, wc_ref[:, off:off + width])

    def store_roped(ref, off, width, mean_ref=None):
        z = seg(off, width)
        for c in range(width // LANES):
            zc = z[:, c * LANES:(c + 1) * LANES]
            partner = jnp.where(first_half, pltpu.roll(zc, LANES - HALF, 1),
                                pltpu.roll(zc, HALF, 1))
            r = zc * cos + partner * sin
            if mean_ref is not None:
                mean_ref[0, :, c * LANES:(c + 1) * LANES] = jnp.mean(r, axis=0, keepdims=True)
            ref[:, c * LANES:(c + 1) * LANES] = r.astype(ref.dtype)

    store_roped(mk_ref, COL_MK, ATT_W, mean_ref=km_ref)
    store_roped(dk_ref, COL_DK, ATT_W)
    store_roped(ik_ref, COL_IK, LANES)
    mg_ref[...] = seg(COL_MG, ATT_W)
    dg_ref[...] = seg(COL_DG, ATT_W)
    xq_ref[...] = seg(COL_XQ, X_W).astype(BF16)
    xg_ref[...] = seg(COL_XG, X_W)
    for c in range(N_BRANCH):
        gl_ref[:, c * D_MODEL:(c + 1) * D_MODEL] = seg(COL_GL + c * D_MODEL, D_MODEL)


def _in_proj(x2, g_in, w_cols, w_rows, tables, batch):
    m = x2.shape[0]
    n_tiles = m // BLK
    row = lambda w: pl.BlockSpec((BLK, w), lambda i: (i, 0))
    feat = lambda r: pl.BlockSpec((None, r, BLK), lambda i: (i // N_BLK, 0, i % N_BLK))
    const = lambda shape: pl.BlockSpec(shape, lambda i: (0, 0), pipeline_mode=pl.Buffered(1))
    tm = lambda w, dt: jax.ShapeDtypeStruct((m, w), dt)
    fm = lambda r, dt: jax.ShapeDtypeStruct((batch, r, SEQ), dt)
    out_shape = [
        fm(ATT_W, BF16), fm(ATT_W, BF16), fm(ATT_W, BF16),
        fm(ATT_W, BF16), fm(ATT_W, BF16), fm(BF16_ROWS, F32),
        tm(ATT_W, BF16), tm(ATT_W, BF16), tm(LANES, BF16),
        tm(ATT_W, F32), tm(ATT_W, F32), tm(X_W, BF16), tm(X_W, F32),
        tm(N_BRANCH * D_MODEL, F32),
        jax.ShapeDtypeStruct((n_tiles, 1, ATT_W), F32),
    ]
    out_specs = [feat(ATT_W)] * 5 + [feat(BF16_ROWS)] + [
        row(ATT_W), row(ATT_W), row(LANES), row(ATT_W), row(ATT_W), row(X_W), row(X_W),
        row(N_BRANCH * D_MODEL), pl.BlockSpec((1, 1, ATT_W), lambda i: (i, 0, 0))]
    cos, sin, cos_t, sin_t = tables
    return pl.pallas_call(
        _in_proj_body,
        grid=(n_tiles,),
        in_specs=[
            pl.BlockSpec((BLK, D_MODEL), lambda i: (i, 0)),
            pl.BlockSpec((1, D_MODEL), lambda i: (0, 0)),
            const((D_MODEL, N_COLS)),
            const((N_ROWS, D_MODEL)),
            pl.BlockSpec((BLK, LANES), lambda i: (i % N_BLK, 0)),
            pl.BlockSpec((BLK, LANES), lambda i: (i % N_BLK, 0)),
            pl.BlockSpec((HALF, BLK), lambda i: (0, i % N_BLK)),
            pl.BlockSpec((HALF, BLK), lambda i: (0, i % N_BLK)),
        ],
        out_specs=out_specs,
        out_shape=out_shape,
        compiler_params=pltpu.CompilerParams(
            dimension_semantics=("arbitrary",), vmem_limit_bytes=VMEM_LIMIT),
        name="in_proj",
    )(x2, g_in, w_cols, w_rows, cos, sin, cos_t, sin_t)


def _mem_proj_body(m_ref, g_ref, w_ref, k_ref, v_ref):
    h = _rmsnorm(m_ref[...], g_ref[...]).astype(BF16)
    k_ref[...] = _dot(h, w_ref[:, 0:X_W]).astype(BF16)
    v_ref[...] = _dot(h, w_ref[:, X_W:2 * X_W]).astype(BF16)


def _mem_proj(mem2, g_mem, w_kv):
    m = mem2.shape[0]
    return pl.pallas_call(
        _mem_proj_body,
        grid=(m // BLK,),
        in_specs=[
            pl.BlockSpec((BLK, D_MODEL), lambda i: (i, 0)),
            pl.BlockSpec((1, D_MODEL), lambda i: (0, 0)),
            pl.BlockSpec((D_MODEL, 2 * X_W), lambda i: (0, 0)),
        ],
        out_specs=[pl.BlockSpec((BLK, X_W), lambda i: (i, 0))] * 2,
        out_shape=[jax.ShapeDtypeStruct((m, X_W), BF16)] * 2,
        compiler_params=pltpu.CompilerParams(
            dimension_semantics=("arbitrary",), vmem_limit_bytes=VMEM_LIMIT),
        name="mem_proj",
    )(mem2, g_mem, w_kv)


def _pair_head_masks():
    r = lax.broadcasted_iota(jnp.int32, (LANES, BLK), 0)
    return [(r >= e * HEAD_DIM) & (r < (e + 1) * HEAD_DIM) for e in range(HEADS_PER_PAIR)]


def _causal_block_mask_t():
    kr = lax.broadcasted_iota(jnp.int32, (BLK, BLK), 0)
    qc = lax.broadcasted_iota(jnp.int32, (BLK, BLK), 1)
    return kr <= qc


def _softmax_pv_t(st, vt):
    m = jnp.max(st, axis=0, keepdims=True)
    p = jnp.exp(st - m)
    l = jnp.sum(p, axis=0, keepdims=True)
    return _dot(vt, p.astype(BF16)) / l


def _store_pair(o_ref, p, halves):
    o_ref[:, p * LANES:(p + 1) * LANES] = jnp.concatenate(halves, axis=0).T


def _moba_select(qm, km, n_past):
    km16 = jnp.concatenate([km, jnp.zeros((BF16_ROWS - N_BLK, LANES), F32)], axis=0).astype(BF16)
    gate = _dot(km16, qm)
    blk = lax.broadcasted_iota(jnp.int32, (BF16_ROWS, BLK), 0)
    past = blk < n_past
    gate = jnp.where(past, gate, NEG_INF)
    rank = jnp.zeros((BF16_ROWS, BLK), F32)
    for jp in range(n_past):
        other = gate[jp:jp + 1, :]
        beats = (other > gate) | ((other == gate) & (blk > jp))
        rank = rank + jnp.where(beats, 1.0, 0.0)
    return jnp.where(past & (rank < MOBA_TOPK), 1.0, 0.0)


def _moba_block(c, qT_ref, k_ref, vT_ref, km_ref, o_ref):
    nk = (c + 1) * BLK
    causal = _causal_block_mask_t()
    head_masks = _pair_head_masks()
    gated = c > MOBA_TOPK

    def scores(h):
        p, e = divmod(h, HEADS_PER_PAIR)
        feats = slice(p * LANES, (p + 1) * LANES)
        q2 = qT_ref[feats, :]
        qm = jnp.where(head_masks[e], q2, jnp.zeros_like(q2))
        sel = _moba_select(qm, km_ref[:, feats], c) if gated else None
        return _dot(k_ref[:, feats], qm), sel

    def attend(h, st, sel):
        parts = []
        for j in range(c):
            blk = st[j * BLK:(j + 1) * BLK]
            if gated:
                blk = jnp.where(sel[j:j + 1, :] > 0.5, blk, NEG_INF)
            parts.append(blk)
        parts.append(jnp.where(causal, st[c * BLK:nk], NEG_INF))
        st = jnp.concatenate(parts, axis=0) if c else parts[0]
        return _softmax_pv_t(st, vT_ref[h * HEAD_DIM:(h + 1) * HEAD_DIM, :])

    _pair_pipeline(
        N_PAIRS,
        lambda p: [scores(p * HEADS_PER_PAIR + e) for e in range(HEADS_PER_PAIR)],
        lambda p, cur: _store_pair(o_ref, p, [attend(p * HEADS_PER_PAIR + e, *cur[e])
                                              for e in range(HEADS_PER_PAIR)]))


def _moba_qblock(c, mqT, mk, mvT, kmean, batch):
    nk = (c + 1) * BLK
    return pl.pallas_call(
        functools.partial(_moba_block, c),
        grid=(batch,),
        in_specs=[pl.BlockSpec((None, ATT_W, BLK), lambda b: (b, 0, c)),
                  pl.BlockSpec((None, nk, ATT_W), lambda b: (b, 0, 0)),
                  pl.BlockSpec((None, ATT_W, nk), lambda b: (b, 0, 0)),
                  pl.BlockSpec((None, N_BLK, ATT_W), lambda b: (b, 0, 0))],
        out_specs=pl.BlockSpec((None, BLK, ATT_W), lambda b: (b, 0, 0)),
        out_shape=jax.ShapeDtypeStruct((batch, BLK, ATT_W), F32),
        compiler_params=pltpu.CompilerParams(
            dimension_semantics=("arbitrary",), vmem_limit_bytes=VMEM_LIMIT),
        name=f"moba_q{c}",
    )(mqT, mk, mvT, kmean)


def _moba(mqT, mk, mvT, kmean, batch):
    mk3 = mk.reshape(batch, SEQ, ATT_W)
    blocks = [_moba_qblock(c, mqT, mk3, mvT, kmean, batch) for c in range(N_BLK)]
    return jnp.stack(blocks, axis=1).reshape(batch * SEQ, ATT_W)


def _key_to_f32(key):
    bits = key ^ ((key >> 31) & 0x7FFFFFFF)
    return lax.bitcast_convert_type(bits, F32)


def _count_ge(sc, thr):
    return jnp.sum(jnp.where(sc >= thr, 1.0, 0.0), axis=0, keepdims=True)


POS_INF = float("inf")


def _dsa_topk_cap(sc, nk):
    kf = float(DSA_TOPK)
    cnt = _count_ge(sc, 0.0)
    t0 = jnp.where(cnt >= kf, 0, INT_MIN).astype(jnp.int32)

    def bit_step(it, t):
        cand = t | jnp.left_shift(jnp.int32(1), 30 - it)
        cnt = _count_ge(sc, _key_to_f32(cand))
        return jnp.where(cnt >= kf, cand, t)

    t = lax.fori_loop(0, 31, bit_step, t0)
    t_val = _key_to_f32(t)
    t_next = _key_to_f32(t + 1)
    need = kf - _count_ge(sc, t_next)
    r = lax.broadcasted_iota(jnp.int32, (BLK, BLK), 0)
    c = lax.broadcasted_iota(jnp.int32, (BLK, BLK), 1)
    strict_lower = jnp.where(c < r, 1.0, 0.0).astype(BF16)
    carry = jnp.zeros((1, BLK), F32)
    parts = []
    for j in range(nk // BLK):
        sc_j = sc[j * BLK:(j + 1) * BLK]
        gt = sc_j >= t_next
        eq = (sc_j >= t_val) & jnp.logical_not(gt)
        eq_f = jnp.where(eq, 1.0, 0.0)
        before = _dot(strict_lower, eq_f.astype(BF16)) + carry
        carry = carry + jnp.sum(eq_f, axis=0, keepdims=True)
        parts.append(jnp.where(gt | (eq & (before < need)), POS_INF, NEG_INF))
    return jnp.concatenate(parts, axis=0)


def _pair_pipeline(n_pairs, issue, consume):
    nxt = issue(0)
    for p in range(n_pairs):
        cur = nxt
        if p + 1 < n_pairs:
            nxt = issue(p + 1)
        consume(p, cur)


def _dsa_block(c, iqT_ref, ik_ref, iwT_ref, qT_ref, k_ref, vT_ref, o_ref):
    nk = (c + 1) * BLK
    causal = _causal_block_mask_t()
    head_masks = _pair_head_masks()

    def masked_q_dots(keys, qT2):
        return [_dot(keys, jnp.where(m, qT2, jnp.zeros_like(qT2))) for m in head_masks]

    if c == 0:
        cap = jnp.where(causal, POS_INF, NEG_INF)
    else:
        wscale = IDX_HEADS ** -0.5
        acc = [jnp.zeros((nk, BLK), F32)]

        def accumulate(p, logits):
            for e in range(HEADS_PER_PAIR):
                w = iwT_ref[p * HEADS_PER_PAIR + e:p * HEADS_PER_PAIR + e + 1, :] * wscale
                acc[0] = acc[0] + jnp.maximum(logits[e], 0.0) * w

        _pair_pipeline(
            N_PAIRS,
            lambda p: masked_q_dots(ik_ref[...], iqT_ref[p * LANES:(p + 1) * LANES, :]),
            accumulate)
        own = jnp.where(causal, acc[0][c * BLK:nk], NEG_INF)
        sc = jnp.concatenate([acc[0][0:c * BLK], own], axis=0)
        cap = _dsa_topk_cap(sc, nk)

    def attend(p, scores):
        _store_pair(o_ref, p, [
            _softmax_pv_t(jnp.minimum(scores[e], cap),
                          vT_ref[p * LANES + e * HEAD_DIM:p * LANES + (e + 1) * HEAD_DIM, :])
            for e in range(HEADS_PER_PAIR)])

    _pair_pipeline(
        N_PAIRS,
        lambda p: masked_q_dots(k_ref[:, p * LANES:(p + 1) * LANES],
                                qT_ref[p * LANES:(p + 1) * LANES, :]),
        attend)


def _dsa_qblock(c, iqT, ik, iwT, dqT, dk, dvT, batch):
    nk = (c + 1) * BLK
    qtile = lambda r: pl.BlockSpec((None, r, BLK), lambda b: (b, 0, c))
    keys = lambda w: pl.BlockSpec((None, nk, w), lambda b: (b, 0, 0))
    return pl.pallas_call(
        functools.partial(_dsa_block, c),
        grid=(batch,),
        in_specs=[qtile(ATT_W), keys(LANES), qtile(BF16_ROWS), qtile(ATT_W), keys(ATT_W),
                  pl.BlockSpec((None, ATT_W, nk), lambda b: (b, 0, 0))],
        out_specs=pl.BlockSpec((None, BLK, ATT_W), lambda b: (b, 0, 0)),
        out_shape=jax.ShapeDtypeStruct((batch, BLK, ATT_W), F32),
        compiler_params=pltpu.CompilerParams(
            dimension_semantics=("arbitrary",), vmem_limit_bytes=VMEM_LIMIT),
        name=f"dsa_q{c}",
    )(iqT, ik, iwT, dqT, dk, dvT)


def _dsa(iqT, ik, iwT, dqT, dk, dvT, batch):
    ik3 = ik.reshape(batch, SEQ, LANES)
    dk3 = dk.reshape(batch, SEQ, ATT_W)
    blocks = [_dsa_qblock(c, iqT, ik3, iwT, dqT, dk3, dvT, batch) for c in range(N_BLK)]
    return jnp.stack(blocks, axis=1).reshape(batch * SEQ, ATT_W)


def _softmax_pv(s, v):
    m = jnp.max(s, axis=1, keepdims=True)
    p = jnp.exp(s - m)
    l = jnp.sum(p, axis=1, keepdims=True)
    return _dot(p.astype(BF16), v) / l


def _merge_body(x_ref, ya_ref, mg_ref, yb_ref, dg_ref, xq_ref, xk_ref, xv_ref, xg_ref, gl_ref,
                bm_ref, wa_ref, wb_ref, wc_ref, wo_ref, gf_ref, o_ref):
    xscale = X_HEAD_DIM ** -0.5
    xk = xk_ref[0]
    xv = xv_ref[0]
    yc_parts = []
    for h in range(X_HEADS):
        sl = slice(h * X_HEAD_DIM, (h + 1) * X_HEAD_DIM)
        s = _nt_dot(xq_ref[:, sl], xk[:, sl]) * xscale
        yc_parts.append(_softmax_pv(s, xv[:, sl]))
    yc = jnp.concatenate(yc_parts, axis=1)

    def up(y, gate_ref, w_ref):
        return _dot((y * jax.nn.silu(gate_ref[...])).astype(BF16), w_ref[...])

    ya = up(ya_ref[...], mg_ref, wa_ref)
    yb = up(yb_ref[...], dg_ref, wb_ref)
    yc = up(yc, xg_ref, wc_ref)
    u = jnp.zeros((BLK, D_MODEL), F32)
    for n, y in enumerate((ya, yb, yc)):
        sl = slice(n * D_MODEL, (n + 1) * D_MODEL)
        u = u + jax.nn.sigmoid(gl_ref[:, sl] + bm_ref[:, sl]) * y
    y = x_ref[...] + _dot(u.astype(BF16), wo_ref[...])
    o_ref[...] = _rmsnorm(y, gf_ref[...])


def _merge(x2, ya, mg, yb, dg, xq, xk, xv, xg, glog, b_merge, wa, wb, wc, wo, g_final, batch):
    tile = lambda w: pl.BlockSpec((BLK, w), lambda b, i: (b * N_BLK + i, 0))
    const = lambda r, w: pl.BlockSpec((r, w), lambda b, i: (0, 0))
    mem = pl.BlockSpec((1, MEM_LEN, X_W), lambda b, i: (b, 0, 0))
    return pl.pallas_call(
        _merge_body,
        grid=(batch, N_BLK),
        in_specs=[tile(D_MODEL), tile(ATT_W), tile(ATT_W), tile(ATT_W), tile(ATT_W), tile(X_W),
                  mem, mem, tile(X_W), tile(N_BRANCH * D_MODEL), const(1, N_BRANCH * D_MODEL),
                  const(ATT_W, D_MODEL), const(ATT_W, D_MODEL), const(X_W, D_MODEL),
                  const(D_MODEL, D_MODEL), const(1, D_MODEL)],
        out_specs=tile(D_MODEL),
        out_shape=jax.ShapeDtypeStruct((batch * SEQ, D_MODEL), F32),
        compiler_params=pltpu.CompilerParams(
            dimension_semantics=("arbitrary", "arbitrary"), vmem_limit_bytes=VMEM_LIMIT),
        name="merge",
    )(x2, ya, mg, yb, dg, xq, xk, xv, xg, glog, b_merge, wa, wb, wc, wo, g_final)


def _pack_w_in(w_in):
    offs = [0]
    for s in IN_SIZES:
        offs.append(offs[-1] + s)
    col = lambda n: w_in[:, offs[n]:offs[n + 1]]
    w_cols = jnp.concatenate([col(1), col(5), col(9), col(9), col(3), col(7), col(11), col(12),
                              col(13)], axis=1)
    iw = jnp.pad(col(10), ((0, 0), (0, BF16_ROWS - IDX_HEADS)))
    w_rows = jnp.concatenate([col(0), col(4), col(8), col(2), col(6), iw], axis=1).T
    return w_cols.astype(BF16), w_rows.astype(BF16)


def _rope_tables():
    inv = jnp.power(ROPE_THETA, -jnp.arange(HALF, dtype=F32) * 2.0 / HEAD_DIM)
    ang = jnp.arange(SEQ).astype(F32)[:, None] * inv[None, :]
    cos = jnp.cos(ang)
    sin = jnp.sin(ang)
    cos_lanes = jnp.tile(cos, (1, LANES // HALF))
    sin_lanes = jnp.tile(jnp.concatenate([-sin, sin], axis=1), (1, HEADS_PER_PAIR))
    return cos_lanes, sin_lanes, cos.T, sin.T


def _layer(x2, mem2, g_in, w_in, b_merge, g_mem, w_mem_kv, w_up_moba, w_up_dsa, w_up_cross, w_out,
           g_final, batch):
    w_cols, w_rows = _pack_w_in(w_in)
    (mqT, dqT, iqT, mvT, dvT, iwT, mk, dk, ik, mg, dg, xq, xg, glog, kmean) = _in_proj(
        x2, g_in[None, :], w_cols, w_rows, _rope_tables(), batch)
    xk, xv = _mem_proj(mem2, g_mem[None, :], w_mem_kv.astype(BF16))
    ya = _moba(mqT, mk, mvT, kmean.reshape(batch, N_BLK, ATT_W), batch)
    yb = _dsa(iqT, ik, iwT, dqT, dk, dvT, batch)
    return _merge(x2, ya, mg, yb, dg, xq, xk.reshape(batch, MEM_LEN, X_W),
                  xv.reshape(batch, MEM_LEN, X_W), xg, glog, b_merge[None, :],
                  w_up_moba.astype(BF16), w_up_dsa.astype(BF16), w_up_cross.astype(BF16),
                  w_out.astype(BF16), g_final, batch)


def kernel(x, mem, g_in, w_in, b_merge, g_mem, w_mem_kv, w_up_moba, w_up_dsa, w_up_cross, w_out,
           g_final):
    batch, seq, d = x.shape
    assert seq == SEQ and d == D_MODEL and mem.shape[1] == MEM_LEN
    assert g_in.shape[0] == 1
    out = _layer(x.reshape(batch * seq, d), mem.reshape(batch * MEM_LEN, d), g_in[0], w_in[0],
                 b_merge[0], g_mem[0], w_mem_kv[0], w_up_moba[0], w_up_dsa[0], w_up_cross[0],
                 w_out[0], g_final[None, :], batch)
    return out.reshape(batch, seq, d)
```

```python
import functools

import jax
import jax.numpy as jnp
from jax import lax
from jax.experimental import pallas as pl
from jax.experimental.pallas import tpu as pltpu

D_MODEL = 1024
SEQ = 2048
HEAD_DIM = 64
HALF = HEAD_DIM // 2
N_HEADS = 8
ATT_W = N_HEADS * HEAD_DIM
BLK = 256
N_BLK = SEQ // BLK
MOBA_TOPK = 3
DSA_TOPK = 256
IDX_HEADS = 8
MEM_LEN = 256
X_HEADS = 4
X_HEAD_DIM = 128
X_W = X_HEADS * X_HEAD_DIM
N_BRANCH = 3
ROPE_THETA = 10000.0
RMS_EPS = 1e-6
LANES = 128
HEADS_PER_PAIR = LANES // HEAD_DIM
N_PAIRS = ATT_W // LANES
BF16_ROWS = 16

IN_SIZES = (ATT_W, ATT_W, ATT_W, ATT_W, ATT_W, ATT_W, ATT_W, ATT_W,
            IDX_HEADS * HEAD_DIM, HEAD_DIM, IDX_HEADS, X_W, X_W, N_BRANCH * D_MODEL)

COL_MK, COL_DK, COL_IK = 0, 512, 1024
COL_MG, COL_DG, COL_XQ, COL_XG, COL_GL = 1152, 1664, 2176, 2688, 3200
N_COLS = COL_GL + N_BRANCH * D_MODEL
ROW_MQ, ROW_DQ, ROW_IQ, ROW_MV, ROW_DV, ROW_IW = 0, 512, 1024, 1536, 2048, 2560
N_ROWS = ROW_IW + BF16_ROWS

VMEM_LIMIT = 56 * 1024 * 1024

F32 = jnp.float32
BF16 = jnp.bfloat16
NEG_INF = float("-inf")
INT_MIN = -2 ** 31
LOG2E = 1.4426950408889634


def _dot(a, b):
    return jnp.dot(a, b, preferred_element_type=F32)


def _nt_dot(a, b):
    return lax.dot_general(a, b, (((1,), (1,)), ((), ())), preferred_element_type=F32)


def _rmsnorm(x, g):
    ms = jnp.mean(x * x, axis=-1, keepdims=True)
    return (x * lax.rsqrt(ms + RMS_EPS)) * g


def _in_proj_body(x_ref, g_ref, wc_ref, wr_ref, cos_ref, sin_ref, cos_t_ref, sin_t_ref,
                  mqT_ref, dqT_ref, iqT_ref, mvT_ref, dvT_ref, iwT_ref,
                  mk_ref, dk_ref, ik_ref, mg_ref, dg_ref, xq_ref, xg_ref, gl_ref, km_ref):
    h = _rmsnorm(x_ref[...], g_ref[...]).astype(BF16)
    qk_scale = HEAD_DIM ** -0.5
    att_scale = qk_scale * LOG2E

    zt = _nt_dot(wr_ref[...], h)
    cos_t = cos_t_ref[...]
    sin_t = sin_t_ref[...]

    def store_roped_t(ref, row0, scale):
        for hd in range(N_HEADS):
            r = row0 + hd * HEAD_DIM
            x1 = zt[r:r + HALF]
            x2 = zt[r + HALF:r + HEAD_DIM]
            o = hd * HEAD_DIM
            ref[o:o + HALF, :] = ((x1 * cos_t - x2 * sin_t) * scale).astype(BF16)
            ref[o + HALF:o + HEAD_DIM, :] = ((x2 * cos_t + x1 * sin_t) * scale).astype(BF16)

    store_roped_t(mqT_ref, ROW_MQ, att_scale)
    store_roped_t(dqT_ref, ROW_DQ, att_scale)
    store_roped_t(iqT_ref, ROW_IQ, qk_scale)
    mvT_ref[...] = zt[ROW_MV:ROW_MV + ATT_W].astype(BF16)
    dvT_ref[...] = zt[ROW_DV:ROW_DV + ATT_W].astype(BF16)
    iwT_ref[...] = zt[ROW_IW:ROW_IW + BF16_ROWS]

    cos = cos_ref[...]
    sin = sin_ref[...]
    lane = lax.broadcasted_iota(jnp.int32, (BLK, LANES), 1)
    first_half = (lane & HALF) == 0

    def seg(off, width):
        return _dot(h, wc_ref[:, off:off + width])

    def store_roped(ref, off, width, mean_ref=None):
        z = seg(off, width)
        for c in range(width // LANES):
            zc = z[:, c * LANES:(c + 1) * LANES]
            partner = jnp.where(first_half, pltpu.roll(zc, LANES - HALF, 1),
                                pltpu.roll(zc, HALF, 1))
            r = zc * cos + partner * sin
            if mean_ref is not None:
                mean_ref[0, :, c * LANES:(c + 1) * LANES] = jnp.mean(r, axis=0, keepdims=True)
            ref[:, c * LANES:(c + 1) * LANES] = r.astype(ref.dtype)

    store_roped(mk_ref, COL_MK, ATT_W, mean_ref=km_ref)
    store_roped(dk_ref, COL_DK, ATT_W)
    store_roped(ik_ref, COL_IK, LANES)
    mg_ref[...] = seg(COL_MG, ATT_W).astype(BF16)
    dg_ref[...] = seg(COL_DG, ATT_W).astype(BF16)
    xq_ref[...] = seg(COL_XQ, X_W).astype(BF16)
    xg_ref[...] = seg(COL_XG, X_W).astype(BF16)
    for c in range(N_BRANCH):
        gl_ref[:, c * D_MODEL:(c + 1) * D_MODEL] = seg(COL_GL + c * D_MODEL, D_MODEL).astype(BF16)


def _in_proj(x2, g_in, w_cols, w_rows, tables, batch):
    m = x2.shape[0]
    n_tiles = m // BLK
    row = lambda w: pl.BlockSpec((BLK, w), lambda i: (i, 0))
    feat = lambda r: pl.BlockSpec((None, r, BLK), lambda i: (i // N_BLK, 0, i % N_BLK))
    const = lambda shape: pl.BlockSpec(shape, lambda i: (0, 0), pipeline_mode=pl.Buffered(1))
    tm = lambda w, dt: jax.ShapeDtypeStruct((m, w), dt)
    fm = lambda r, dt: jax.ShapeDtypeStruct((batch, r, SEQ), dt)
    out_shape = [
        fm(ATT_W, BF16), fm(ATT_W, BF16), fm(ATT_W, BF16),
        fm(ATT_W, BF16), fm(ATT_W, BF16), fm(BF16_ROWS, F32),
        tm(ATT_W, BF16), tm(ATT_W, BF16), tm(LANES, BF16),
        tm(ATT_W, BF16), tm(ATT_W, BF16), tm(X_W, BF16), tm(X_W, BF16),
        tm(N_BRANCH * D_MODEL, BF16),
        jax.ShapeDtypeStruct((n_tiles, 1, ATT_W), F32),
    ]
    out_specs = [feat(ATT_W)] * 5 + [feat(BF16_ROWS)] + [
        row(ATT_W), row(ATT_W), row(LANES), row(ATT_W), row(ATT_W), row(X_W), row(X_W),
        row(N_BRANCH * D_MODEL), pl.BlockSpec((1, 1, ATT_W), lambda i: (i, 0, 0))]
    cos, sin, cos_t, sin_t = tables
    return pl.pallas_call(
        _in_proj_body,
        grid=(n_tiles,),
        in_specs=[
            pl.BlockSpec((BLK, D_MODEL), lambda i: (i, 0)),
            pl.BlockSpec((1, D_MODEL), lambda i: (0, 0)),
            const((D_MODEL, N_COLS)),
            const((N_ROWS, D_MODEL)),
            pl.BlockSpec((BLK, LANES), lambda i: (i % N_BLK, 0)),
            pl.BlockSpec((BLK, LANES), lambda i: (i % N_BLK, 0)),
            pl.BlockSpec((HALF, BLK), lambda i: (0, i % N_BLK)),
            pl.BlockSpec((HALF, BLK), lambda i: (0, i % N_BLK)),
        ],
        out_specs=out_specs,
        out_shape=out_shape,
        compiler_params=pltpu.CompilerParams(
            dimension_semantics=("arbitrary",), vmem_limit_bytes=VMEM_LIMIT),
        name="in_proj",
    )(x2, g_in, w_cols, w_rows, cos, sin, cos_t, sin_t)


def _mem_proj_body(m_ref, g_ref, w_ref, k_ref, v_ref):
    h = _rmsnorm(m_ref[...], g_ref[...]).astype(BF16)
    k_ref[...] = _dot(h, w_ref[:, 0:X_W]).astype(BF16)
    v_ref[...] = _dot(h, w_ref[:, X_W:2 * X_W]).astype(BF16)


def _mem_proj(mem2, g_mem, w_kv):
    m = mem2.shape[0]
    return pl.pallas_call(
        _mem_proj_body,
        grid=(m // BLK,),
        in_specs=[
            pl.BlockSpec((BLK, D_MODEL), lambda i: (i, 0)),
            pl.BlockSpec((1, D_MODEL), lambda i: (0, 0)),
            pl.BlockSpec((D_MODEL, 2 * X_W), lambda i: (0, 0)),
        ],
        out_specs=[pl.BlockSpec((BLK, X_W), lambda i: (i, 0))] * 2,
        out_shape=[jax.ShapeDtypeStruct((m, X_W), BF16)] * 2,
        compiler_params=pltpu.CompilerParams(
            dimension_semantics=("arbitrary",), vmem_limit_bytes=VMEM_LIMIT),
        name="mem_proj",
    )(mem2, g_mem, w_kv)


def _pair_head_masks():
    r = lax.broadcasted_iota(jnp.int32, (LANES, BLK), 0)
    return [(r >= e * HEAD_DIM) & (r < (e + 1) * HEAD_DIM) for e in range(HEADS_PER_PAIR)]


def _causal_block_mask_t():
    kr = lax.broadcasted_iota(jnp.int32, (BLK, BLK), 0)
    qc = lax.broadcasted_iota(jnp.int32, (BLK, BLK), 1)
    return kr <= qc


CHAIN_ROWS = 64


def _col_reduce(x, combine, finish):
    rows = x.shape[0]
    acc = x[0:CHAIN_ROWS]
    for r in range(1, rows // CHAIN_ROWS):
        acc = combine(acc, x[r * CHAIN_ROWS:(r + 1) * CHAIN_ROWS])
    return finish(acc, axis=0, keepdims=True)


def _softmax_pv_t(st, vt):
    m = _col_reduce(st, jnp.maximum, jnp.max)
    p = jnp.exp2(st - m)
    l = _col_reduce(p, jnp.add, jnp.sum)
    return _dot(vt, p.astype(BF16)) / l


def _store_pair(o_ref, p, halves):
    o_ref[:, p * LANES:(p + 1) * LANES] = jnp.concatenate(halves, axis=0).T.astype(o_ref.dtype)


def _qblock_call(body, c, name, in_specs, args, acc, batch):
    n_in = len(args)
    return pl.pallas_call(
        body,
        grid=(batch,),
        in_specs=in_specs + [pl.BlockSpec(memory_space=pl.ANY)],
        out_specs=pl.BlockSpec((None, BLK, ATT_W), lambda b: (b, c, 0)),
        out_shape=jax.ShapeDtypeStruct(acc.shape, acc.dtype),
        input_output_aliases={n_in: 0},
        compiler_params=pltpu.CompilerParams(
            dimension_semantics=("arbitrary",), vmem_limit_bytes=VMEM_LIMIT),
        name=name,
    )(*args, acc)


def _pair_pipeline(n_pairs, issue, consume):
    nxt = issue(0)
    for p in range(n_pairs):
        cur = nxt
        if p + 1 < n_pairs:
            nxt = issue(p + 1)
        consume(p, cur)


def _moba_select(qm, km, n_past):
    km16 = jnp.concatenate([km, jnp.zeros((BF16_ROWS - N_BLK, LANES), F32)], axis=0).astype(BF16)
    gate = _dot(km16, qm)
    blk = lax.broadcasted_iota(jnp.int32, (BF16_ROWS, BLK), 0)
    past = blk < n_past
    gate = jnp.where(past, gate, NEG_INF)
    rank = jnp.zeros((BF16_ROWS, BLK), F32)
    for jp in range(n_past):
        other = gate[jp:jp + 1, :]
        beats = (other > gate) | ((other == gate) & (blk > jp))
        rank = rank + jnp.where(beats, 1.0, 0.0)
    return jnp.where(past & (rank < MOBA_TOPK), 1.0, 0.0)


def _moba_block(c, qT_ref, k_ref, vT_ref, km_ref, acc_ref, o_ref):
    del acc_ref
    nk = (c + 1) * BLK
    causal = _causal_block_mask_t()
    head_masks = _pair_head_masks()
    gated = c > MOBA_TOPK

    def scores(h):
        p, e = divmod(h, HEADS_PER_PAIR)
        feats = slice(p * LANES, (p + 1) * LANES)
        q2 = qT_ref[feats, :]
        qm = jnp.where(head_masks[e], q2, jnp.zeros_like(q2))
        sel = _moba_select(qm, km_ref[:, feats], c) if gated else None
        return _dot(k_ref[:, feats], qm), sel

    def attend(h, st, sel):
        parts = []
        for j in range(c):
            blk = st[j * BLK:(j + 1) * BLK]
            if gated:
                blk = jnp.where(sel[j:j + 1, :] > 0.5, blk, NEG_INF)
            parts.append(blk)
        parts.append(jnp.where(causal, st[c * BLK:nk], NEG_INF))
        st = jnp.concatenate(parts, axis=0) if c else parts[0]
        return _softmax_pv_t(st, vT_ref[h * HEAD_DIM:(h + 1) * HEAD_DIM, :])

    _pair_pipeline(
        N_PAIRS,
        lambda p: [scores(p * HEADS_PER_PAIR + e) for e in range(HEADS_PER_PAIR)],
        lambda p, cur: _store_pair(o_ref, p, [attend(p * HEADS_PER_PAIR + e, *cur[e])
                                              for e in range(HEADS_PER_PAIR)]))


def _moba(mqT, mk, mvT, kmean, batch):
    mk3 = mk.reshape(batch, SEQ, ATT_W)
    acc = jnp.zeros((batch, SEQ, ATT_W), BF16)
    for c in range(N_BLK):
        nk = (c + 1) * BLK
        in_specs = [pl.BlockSpec((None, ATT_W, BLK), lambda b, c=c: (b, 0, c)),
                    pl.BlockSpec((None, nk, ATT_W), lambda b: (b, 0, 0)),
                    pl.BlockSpec((None, ATT_W, nk), lambda b: (b, 0, 0)),
                    pl.BlockSpec((None, N_BLK, ATT_W), lambda b: (b, 0, 0))]
        acc = _qblock_call(functools.partial(_moba_block, c), c, f"moba_q{c}", in_specs,
                           (mqT, mk3, mvT, kmean), acc, batch)
    return acc.reshape(batch * SEQ, ATT_W)


def _key_to_f32(key):
    bits = key ^ ((key >> 31) & 0x7FFFFFFF)
    return lax.bitcast_convert_type(bits, F32)


def _count_ge(sc, thr):
    acc = jnp.zeros((CHAIN_ROWS, BLK), F32)
    for r in range(sc.shape[0] // CHAIN_ROWS):
        acc = acc + jnp.where(sc[r * CHAIN_ROWS:(r + 1) * CHAIN_ROWS] >= thr, 1.0, 0.0)
    return jnp.sum(acc, axis=0, keepdims=True)


POS_INF = float("inf")


def _dsa_topk_cap(sc, nk):
    kf = float(DSA_TOPK)
    cnt = _count_ge(sc, 0.0)
    t0 = jnp.where(cnt >= kf, 0, INT_MIN).astype(jnp.int32)

    def bit_step(it, t):
        cand = t | jnp.left_shift(jnp.int32(1), 30 - it)
        cnt = _count_ge(sc, _key_to_f32(cand))
        return jnp.where(cnt >= kf, cand, t)

    t = lax.fori_loop(0, 31, bit_step, t0)
    t_val = _key_to_f32(t)
    t_next = _key_to_f32(t + 1)
    need = kf - _count_ge(sc, t_next)
    r = lax.broadcasted_iota(jnp.int32, (BLK, BLK), 0)
    c = lax.broadcasted_iota(jnp.int32, (BLK, BLK), 1)
    strict_lower = jnp.where(c < r, 1.0, 0.0).astype(BF16)
    carry = jnp.zeros((1, BLK), F32)
    parts = []
    for j in range(nk // BLK):
        sc_j = sc[j * BLK:(j + 1) * BLK]
        gt = sc_j >= t_next
        eq = (sc_j >= t_val) & jnp.logical_not(gt)
        eq_f = jnp.where(eq, 1.0, 0.0)
        before = _dot(strict_lower, eq_f.astype(BF16)) + carry
        carry = carry + _col_reduce(eq_f, jnp.add, jnp.sum)
        parts.append(jnp.where(gt | (eq & (before < need)), POS_INF, NEG_INF))
    return jnp.concatenate(parts, axis=0)


def _dsa_block(c, iqT_ref, ik_ref, iwT_ref, qT_ref, k_ref, vT_ref, acc_ref, o_ref):
    del acc_ref
    nk = (c + 1) * BLK
    causal = _causal_block_mask_t()
    head_masks = _pair_head_masks()

    def masked_q_dots(keys, qT2):
        return [_dot(keys, jnp.where(m, qT2, jnp.zeros_like(qT2))) for m in head_masks]

    if c == 0:
        cap = jnp.where(causal, POS_INF, NEG_INF)
    else:
        wscale = IDX_HEADS ** -0.5
        acc = [jnp.zeros((nk, BLK), F32)]

        def accumulate(p, logits):
            for e in range(HEADS_PER_PAIR):
                w = iwT_ref[p * HEADS_PER_PAIR + e:p * HEADS_PER_PAIR + e + 1, :] * wscale
                acc[0] = acc[0] + jnp.maximum(logits[e], 0.0) * w

        _pair_pipeline(
            N_PAIRS,
            lambda p: masked_q_dots(ik_ref[...], iqT_ref[p * LANES:(p + 1) * LANES, :]),
            accumulate)
        own = jnp.where(causal, acc[0][c * BLK:nk], NEG_INF)
        sc = jnp.concatenate([acc[0][0:c * BLK], own], axis=0)
        cap = _dsa_topk_cap(sc, nk)

    def attend(p, scores):
        _store_pair(o_ref, p, [
            _softmax_pv_t(jnp.minimum(scores[e], cap),
                          vT_ref[p * LANES + e * HEAD_DIM:p * LANES + (e + 1) * HEAD_DIM, :])
            for e in range(HEADS_PER_PAIR)])

    _pair_pipeline(
        N_PAIRS,
        lambda p: masked_q_dots(k_ref[:, p * LANES:(p + 1) * LANES],
                                qT_ref[p * LANES:(p + 1) * LANES, :]),
        attend)


def _dsa(iqT, ik, iwT, dqT, dk, dvT, batch):
    ik3 = ik.reshape(batch, SEQ, LANES)
    dk3 = dk.reshape(batch, SEQ, ATT_W)
    acc = jnp.zeros((batch, SEQ, ATT_W), BF16)
    for c in range(N_BLK):
        nk = (c + 1) * BLK
        qtile = lambda r, c=c: pl.BlockSpec((None, r, BLK), lambda b: (b, 0, c))
        keys = lambda w, nk=nk: pl.BlockSpec((None, nk, w), lambda b: (b, 0, 0))
        in_specs = [qtile(ATT_W), keys(LANES), qtile(BF16_ROWS), qtile(ATT_W), keys(ATT_W),
                    pl.BlockSpec((None, ATT_W, nk), lambda b: (b, 0, 0))]
        acc = _qblock_call(functools.partial(_dsa_block, c), c, f"dsa_q{c}", in_specs,
                           (iqT, ik3, iwT, dqT, dk3, dvT), acc, batch)
    return acc.reshape(batch * SEQ, ATT_W)


def _softmax_pv(s, v):
    m = jnp.max(s, axis=1, keepdims=True)
    p = jnp.exp2(s - m)
    l = jnp.sum(p, axis=1, keepdims=True)
    return _dot(p.astype(BF16), v) / l


def _merge_body(x_ref, ya_ref, mg_ref, yb_ref, dg_ref, xq_ref, xk_ref, xv_ref, xg_ref, gl_ref,
                bm_ref, wa_ref, wb_ref, wc_ref, wo_ref, gf_ref, o_ref):
    xscale = X_HEAD_DIM ** -0.5 * LOG2E
    xk = xk_ref[0]
    xv = xv_ref[0]
    yc_parts = []
    for h in range(X_HEADS):
        sl = slice(h * X_HEAD_DIM, (h + 1) * X_HEAD_DIM)
        s = _nt_dot(xq_ref[:, sl], xk[:, sl]) * xscale
        yc_parts.append(_softmax_pv(s, xv[:, sl]))
    yc = jnp.concatenate(yc_parts, axis=1)

    def up(y, gate_ref, w_ref):
        gate = gate_ref[...].astype(F32)
        return _dot((y.astype(F32) * jax.nn.silu(gate)).astype(BF16), w_ref[...])

    ya = up(ya_ref[...], mg_ref, wa_ref)
    yb = up(yb_ref[...], dg_ref, wb_ref)
    yc = up(yc, xg_ref, wc_ref)
    u = jnp.zeros((BLK, D_MODEL), F32)
    for n, y in enumerate((ya, yb, yc)):
        sl = slice(n * D_MODEL, (n + 1) * D_MODEL)
        u = u + jax.nn.sigmoid(gl_ref[:, sl].astype(F32) + bm_ref[:, sl]) * y
    y = x_ref[...] + _dot(u.astype(BF16), wo_ref[...])
    o_ref[...] = _rmsnorm(y, gf_ref[...])


def _merge(x2, ya, mg, yb, dg, xq, xk, xv, xg, glog, b_merge, wa, wb, wc, wo, g_final, batch):
    tile = lambda w: pl.BlockSpec((BLK, w), lambda b, i: (b * N_BLK + i, 0))
    const = lambda r, w: pl.BlockSpec((r, w), lambda b, i: (0, 0))
    mem = pl.BlockSpec((1, MEM_LEN, X_W), lambda b, i: (b, 0, 0))
    return pl.pallas_call(
        _merge_body,
        grid=(batch, N_BLK),
        in_specs=[tile(D_MODEL), tile(ATT_W), tile(ATT_W), tile(ATT_W), tile(ATT_W), tile(X_W),
                  mem, mem, tile(X_W), tile(N_BRANCH * D_MODEL), const(1, N_BRANCH * D_MODEL),
                  const(ATT_W, D_MODEL), const(ATT_W, D_MODEL), const(X_W, D_MODEL),
                  const(D_MODEL, D_MODEL), const(1, D_MODEL)],
        out_specs=tile(D_MODEL),
        out_shape=jax.ShapeDtypeStruct((batch * SEQ, D_MODEL), F32),
        compiler_params=pltpu.CompilerParams(
            dimension_semantics=("arbitrary", "arbitrary"), vmem_limit_bytes=VMEM_LIMIT),
        name="merge",
    )(x2, ya, mg, yb, dg, xq, xk, xv, xg, glog, b_merge, wa, wb, wc, wo, g_final)


def _pack_w_in(w_in):
    offs = [0]
    for s in IN_SIZES:
        offs.append(offs[-1] + s)
    col = lambda n: w_in[:, offs[n]:offs[n + 1]]
    w_cols = jnp.concatenate([col(1), col(5), col(9), col(9), col(3), col(7), col(11), col(12),
                              col(13)], axis=1)
    iw = jnp.pad(col(10), ((0, 0), (0, BF16_ROWS - IDX_HEADS)))
    w_rows = jnp.concatenate([col(0), col(4), col(8), col(2), col(6), iw], axis=1).T
    return w_cols.astype(BF16), w_rows.astype(BF16)


def _rope_tables():
    inv = jnp.power(ROPE_THETA, -jnp.arange(HALF, dtype=F32) * 2.0 / HEAD_DIM)
    ang = jnp.arange(SEQ).astype(F32)[:, None] * inv[None, :]
    cos = jnp.cos(ang)
    sin = jnp.sin(ang)
    cos_lanes = jnp.tile(cos, (1, LANES // HALF))
    sin_lanes = jnp.tile(jnp.concatenate([-sin, sin], axis=1), (1, HEADS_PER_PAIR))
    return cos_lanes, sin_lanes, cos.T, sin.T


def _layer(x2, mem2, g_in, w_in, b_merge, g_mem, w_mem_kv, w_up_moba, w_up_dsa, w_up_cross, w_out,
           g_final, batch):
    w_cols, w_rows = _pack_w_in(w_in)
    (mqT, dqT, iqT, mvT, dvT, iwT, mk, dk, ik, mg, dg, xq, xg, glog, kmean) = _in_proj(
        x2, g_in[None, :], w_cols, w_rows, _rope_tables(), batch)
    xk, xv = _mem_proj(mem2, g_mem[None, :], w_mem_kv.astype(BF16))
    ya = _moba(mqT, mk, mvT, kmean.reshape(batch, N_BLK, ATT_W), batch)
    yb = _dsa(iqT, ik, iwT, dqT, dk, dvT, batch)
    return _merge(x2, ya, mg, yb, dg, xq, xk.reshape(batch, MEM_LEN, X_W),
                  xv.reshape(batch, MEM_LEN, X_W), xg, glog, b_merge[None, :],
                  w_up_moba.astype(BF16), w_up_dsa.astype(BF16), w_up_cross.astype(BF16),
                  w_out.astype(BF16), g_final, batch)


def kernel(x, mem, g_in, w_in, b_merge, g_mem, w_mem_kv, w_up_moba, w_up_dsa, w_up_cross, w_out,
           g_final):
    batch, seq, d = x.shape
    assert seq == SEQ and d == D_MODEL and mem.shape[1] == MEM_LEN
    assert g_in.shape[0] == 1
    out = _layer(x.reshape(batch * seq, d), mem.reshape(batch * MEM_LEN, d), g_in[0], w_in[0],
                 b_merge[0], g_mem[0], w_mem_kv[0], w_up_moba[0], w_up_dsa[0], w_up_cross[0],
                 w_out[0], g_final[None, :], batch)
    return out.reshape(batch, seq, d)
```

```python
import functools

import jax
import jax.numpy as jnp
from jax import lax
from jax.experimental import pallas as pl
from jax.experimental.pallas import tpu as pltpu

D_MODEL = 1024
SEQ = 2048
HEAD_DIM = 64
HALF = HEAD_DIM // 2
N_HEADS = 8
ATT_W = N_HEADS * HEAD_DIM
BLK = 256
N_BLK = SEQ // BLK
PROJ_TILE = 2 * BLK
MOBA_TOPK = 3
DSA_TOPK = 256
IDX_HEADS = 8
MEM_LEN = 256
X_HEADS = 4
X_HEAD_DIM = 128
X_W = X_HEADS * X_HEAD_DIM
N_BRANCH = 3
ROPE_THETA = 10000.0
RMS_EPS = 1e-6
LANES = 128
HEADS_PER_PAIR = LANES // HEAD_DIM
N_PAIRS = ATT_W // LANES
BF16_ROWS = 16

IN_SIZES = (ATT_W, ATT_W, ATT_W, ATT_W, ATT_W, ATT_W, ATT_W, ATT_W,
            IDX_HEADS * HEAD_DIM, HEAD_DIM, IDX_HEADS, X_W, X_W, N_BRANCH * D_MODEL)

COL_MK, COL_DK, COL_IK = 0, 512, 1024
COL_MG, COL_DG, COL_XQ, COL_XG, COL_GL = 1152, 1664, 2176, 2688, 3200
N_COLS = COL_GL + N_BRANCH * D_MODEL
ROW_MQ, ROW_DQ, ROW_IQ, ROW_MV, ROW_DV, ROW_IW = 0, 512, 1024, 1536, 2048, 2560
N_ROWS = ROW_IW + BF16_ROWS

VMEM_LIMIT = 56 * 1024 * 1024

F32 = jnp.float32
BF16 = jnp.bfloat16
NEG_INF = float("-inf")
INT_MIN = -2 ** 31
LOG2E = 1.4426950408889634


def _dot(a, b):
    return jnp.dot(a, b, preferred_element_type=F32)


def _nt_dot(a, b):
    return lax.dot_general(a, b, (((1,), (1,)), ((), ())), preferred_element_type=F32)


def _rmsnorm(x, g):
    ms = jnp.mean(x * x, axis=-1, keepdims=True)
    return (x * lax.rsqrt(ms + RMS_EPS)) * g


def _softmax_pv(s, v):
    m = jnp.max(s, axis=1, keepdims=True)
    p = jnp.exp2(s - m)
    l = jnp.sum(p, axis=1, keepdims=True)
    return _dot(p.astype(BF16), v) / l


def _in_proj_body(x_ref, g_ref, wc_ref, wr_ref, cos_ref, sin_ref, cos_t_ref, sin_t_ref,
                  bm_ref, xk_ref, xv_ref,
                  mqT_ref, dqT_ref, iqT_ref, mvT_ref, dvT_ref, iwT_ref,
                  mk_ref, dk_ref, ik_ref, mg_ref, dg_ref, yc_ref, gl_ref, km_ref):
    h = _rmsnorm(x_ref[...], g_ref[...]).astype(BF16)
    qk_scale = HEAD_DIM ** -0.5
    att_scale = qk_scale * LOG2E

    def seg(off, width):
        return lambda: _dot(h, wc_ref[:, off:off + width])

    kept = {}

    def keep_silu_xg(z):
        kept["xg"] = jax.nn.silu(z)

    def cross_attention(z):
        xq = z.astype(BF16)
        xscale = X_HEAD_DIM ** -0.5 * LOG2E
        yc_parts = []
        for hd in range(X_HEADS):
            sl = slice(hd * X_HEAD_DIM, (hd + 1) * X_HEAD_DIM)
            s = _nt_dot(xq[:, sl], xk_ref[:, sl]) * xscale
            yc_parts.append(_softmax_pv(s, xv_ref[:, sl]))
        yc_ref[...] = (jnp.concatenate(yc_parts, axis=1) * kept["xg"]).astype(BF16)

    def store_merge_gate(c):
        sl = slice(c * D_MODEL, (c + 1) * D_MODEL)

        def epilogue(z):
            gl_ref[:, sl] = jax.nn.sigmoid(z + bm_ref[:, sl]).astype(BF16)
        return epilogue

    def store_silu(ref):
        def epilogue(z):
            ref[...] = jax.nn.silu(z).astype(BF16)
        return epilogue

    def store_feature_major(zt):
        cos_t = cos_t_ref[...]
        sin_t = sin_t_ref[...]
        for ref, row0, scale in ((mqT_ref, ROW_MQ, att_scale), (dqT_ref, ROW_DQ, att_scale),
                                 (iqT_ref, ROW_IQ, qk_scale)):
            for hd in range(N_HEADS):
                r = row0 + hd * HEAD_DIM
                x1 = zt[r:r + HALF]
                x2 = zt[r + HALF:r + HEAD_DIM]
                o = hd * HEAD_DIM
                ref[o:o + HALF, :] = ((x1 * cos_t - x2 * sin_t) * scale).astype(BF16)
                ref[o + HALF:o + HEAD_DIM, :] = ((x2 * cos_t + x1 * sin_t) * scale).astype(BF16)
        mvT_ref[...] = zt[ROW_MV:ROW_MV + ATT_W].astype(BF16)
        dvT_ref[...] = zt[ROW_DV:ROW_DV + ATT_W].astype(BF16)
        iwT_ref[...] = zt[ROW_IW:ROW_IW + BF16_ROWS]

    def store_roped(ref, mean_ref=None):
        def epilogue(z):
            cos = cos_ref[...]
            sin = sin_ref[...]
            lane = lax.broadcasted_iota(jnp.int32, (PROJ_TILE, LANES), 1)
            first_half = (lane & HALF) == 0
            for c in range(z.shape[1] // LANES):
                lanes = slice(c * LANES, (c + 1) * LANES)
                zc = z[:, lanes]
                partner = jnp.where(first_half, pltpu.roll(zc, LANES - HALF, 1),
                                    pltpu.roll(zc, HALF, 1))
                r = zc * cos + partner * sin
                if mean_ref is not None:
                    for blk in range(PROJ_TILE // BLK):
                        mean_ref[blk, :, lanes] = jnp.mean(r[blk * BLK:(blk + 1) * BLK], axis=0,
                                                           keepdims=True)
                ref[:, lanes] = r.astype(ref.dtype)
        return epilogue

    stages = [(seg(COL_XG, X_W), keep_silu_xg), (seg(COL_XQ, X_W), cross_attention)]
    stages += [(seg(COL_GL + c * D_MODEL, D_MODEL), store_merge_gate(c)) for c in range(N_BRANCH)]
    stages += [(seg(COL_MG, ATT_W), store_silu(mg_ref)), (seg(COL_DG, ATT_W), store_silu(dg_ref)),
               (lambda: _nt_dot(wr_ref[...], h), store_feature_major),
               (seg(COL_MK, ATT_W), store_roped(mk_ref, mean_ref=km_ref)),
               (seg(COL_DK, ATT_W), store_roped(dk_ref)),
               (seg(COL_IK, LANES), store_roped(ik_ref))]
    nxt = stages[0][0]()
    for i, (_, epilogue) in enumerate(stages):
        cur = nxt
        if i + 1 < len(stages):
            nxt = stages[i + 1][0]()
        epilogue(cur)


def _in_proj(x2, g_in, w_cols, w_rows, tables, b_merge, xk, xv, batch):
    m = x2.shape[0]
    n_tiles = m // PROJ_TILE
    per_seq = SEQ // PROJ_TILE
    row = lambda w: pl.BlockSpec((PROJ_TILE, w), lambda i: (i, 0))
    feat = lambda r: pl.BlockSpec((None, r, PROJ_TILE), lambda i: (i // per_seq, 0, i % per_seq))
    const = lambda shape: pl.BlockSpec(shape, lambda i: (0, 0), pipeline_mode=pl.Buffered(1))
    tm = lambda w, dt: jax.ShapeDtypeStruct((m, w), dt)
    fm = lambda r, dt: jax.ShapeDtypeStruct((batch, r, SEQ), dt)
    out_shape = [
        fm(ATT_W, BF16), fm(ATT_W, BF16), fm(ATT_W, BF16),
        fm(ATT_W, BF16), fm(ATT_W, BF16), fm(BF16_ROWS, F32),
        tm(ATT_W, BF16), tm(ATT_W, BF16), tm(LANES, BF16),
        tm(ATT_W, BF16), tm(ATT_W, BF16),
        tm(X_W, BF16),
        tm(N_BRANCH * D_MODEL, BF16),
        jax.ShapeDtypeStruct((m // BLK, 1, ATT_W), F32),
    ]
    out_specs = [feat(ATT_W)] * 5 + [feat(BF16_ROWS)] + [
        row(ATT_W), row(ATT_W), row(LANES), row(ATT_W), row(ATT_W), row(X_W),
        row(N_BRANCH * D_MODEL),
        pl.BlockSpec((PROJ_TILE // BLK, 1, ATT_W), lambda i: (i, 0, 0))]
    mem = pl.BlockSpec((None, MEM_LEN, X_W), lambda i: (i // per_seq, 0, 0))
    cos, sin, cos_t, sin_t = tables
    return pl.pallas_call(
        _in_proj_body,
        grid=(n_tiles,),
        in_specs=[
            pl.BlockSpec((PROJ_TILE, D_MODEL), lambda i: (i, 0)),
            pl.BlockSpec((1, D_MODEL), lambda i: (0, 0)),
            const((D_MODEL, N_COLS)),
            const((N_ROWS, D_MODEL)),
            pl.BlockSpec((PROJ_TILE, LANES), lambda i: (i % per_seq, 0)),
            pl.BlockSpec((PROJ_TILE, LANES), lambda i: (i % per_seq, 0)),
            pl.BlockSpec((HALF, PROJ_TILE), lambda i: (0, i % per_seq)),
            pl.BlockSpec((HALF, PROJ_TILE), lambda i: (0, i % per_seq)),
            pl.BlockSpec((1, N_BRANCH * D_MODEL), lambda i: (0, 0)),
            mem, mem,
        ],
        out_specs=out_specs,
        out_shape=out_shape,
        compiler_params=pltpu.CompilerParams(
            dimension_semantics=("arbitrary",), vmem_limit_bytes=VMEM_LIMIT),
        name="in_proj",
    )(x2, g_in, w_cols, w_rows, cos, sin, cos_t, sin_t, b_merge, xk, xv)


def _mem_proj_body(m_ref, g_ref, w_ref, k_ref, v_ref):
    h = _rmsnorm(m_ref[...], g_ref[...]).astype(BF16)
    k_ref[...] = _dot(h, w_ref[:, 0:X_W]).astype(BF16)
    v_ref[...] = _dot(h, w_ref[:, X_W:2 * X_W]).astype(BF16)


def _mem_proj(mem2, g_mem, w_kv):
    m = mem2.shape[0]
    return pl.pallas_call(
        _mem_proj_body,
        grid=(m // BLK,),
        in_specs=[
            pl.BlockSpec((BLK, D_MODEL), lambda i: (i, 0)),
            pl.BlockSpec((1, D_MODEL), lambda i: (0, 0)),
            pl.BlockSpec((D_MODEL, 2 * X_W), lambda i: (0, 0)),
        ],
        out_specs=[pl.BlockSpec((BLK, X_W), lambda i: (i, 0))] * 2,
        out_shape=[jax.ShapeDtypeStruct((m, X_W), BF16)] * 2,
        compiler_params=pltpu.CompilerParams(
            dimension_semantics=("arbitrary",), vmem_limit_bytes=VMEM_LIMIT),
        name="mem_proj",
    )(mem2, g_mem, w_kv)


def _pair_head_masks():
    r = lax.broadcasted_iota(jnp.int32, (LANES, BLK), 0)
    return [(r >= e * HEAD_DIM) & (r < (e + 1) * HEAD_DIM) for e in range(HEADS_PER_PAIR)]


def _causal_block_mask_t():
    kr = lax.broadcasted_iota(jnp.int32, (BLK, BLK), 0)
    qc = lax.broadcasted_iota(jnp.int32, (BLK, BLK), 1)
    return kr <= qc


CHAIN_ROWS = 64


def _col_reduce(x, combine, finish):
    rows = x.shape[0]
    acc = x[0:CHAIN_ROWS]
    for r in range(1, rows // CHAIN_ROWS):
        acc = combine(acc, x[r * CHAIN_ROWS:(r + 1) * CHAIN_ROWS])
    return finish(acc, axis=0, keepdims=True)


def _softmax_pv_t(st, vt):
    m = _col_reduce(st, jnp.maximum, jnp.max)
    p = jnp.exp2(st - m)
    l = _col_reduce(p, jnp.add, jnp.sum)
    return _dot(vt, p.astype(BF16)) / l


def _store_pair(o_ref, gate_ref, p, halves):
    lanes = slice(p * LANES, (p + 1) * LANES)
    y = jnp.concatenate(halves, axis=0).T
    o_ref[:, lanes] = (y * gate_ref[:, lanes].astype(F32)).astype(o_ref.dtype)


def _qblock_call(body, c, name, in_specs, args, gate, acc, batch):
    out_tile = pl.BlockSpec((None, BLK, ATT_W), lambda b: (b, c, 0))
    return pl.pallas_call(
        body,
        grid=(batch,),
        in_specs=in_specs + [out_tile, pl.BlockSpec(memory_space=pl.ANY)],
        out_specs=out_tile,
        out_shape=jax.ShapeDtypeStruct(acc.shape, acc.dtype),
        input_output_aliases={len(args) + 1: 0},
        compiler_params=pltpu.CompilerParams(
            dimension_semantics=("arbitrary",), vmem_limit_bytes=VMEM_LIMIT),
        name=name,
    )(*args, gate, acc)


def _pair_pipeline(n_pairs, issue, consume):
    nxt = issue(0)
    for p in range(n_pairs):
        cur = nxt
        if p + 1 < n_pairs:
            nxt = issue(p + 1)
        consume(p, cur)


def _moba_select(qm, km, n_past):
    km16 = jnp.concatenate([km, jnp.zeros((BF16_ROWS - N_BLK, LANES), F32)], axis=0).astype(BF16)
    gate = _dot(km16, qm)
    blk = lax.broadcasted_iota(jnp.int32, (BF16_ROWS, BLK), 0)
    past = blk < n_past
    gate = jnp.where(past, gate, NEG_INF)
    rank = jnp.zeros((BF16_ROWS, BLK), F32)
    for jp in range(n_past):
        other = gate[jp:jp + 1, :]
        beats = (other > gate) | ((other == gate) & (blk > jp))
        rank = rank + jnp.where(beats, 1.0, 0.0)
    return jnp.where(past & (rank < MOBA_TOPK), 1.0, 0.0)


def _moba_block(c, qT_ref, k_ref, vT_ref, km_ref, gate_ref, acc_ref, o_ref):
    del acc_ref
    nk = (c + 1) * BLK
    causal = _causal_block_mask_t()
    head_masks = _pair_head_masks()
    gated = c > MOBA_TOPK

    def scores(h):
        p, e = divmod(h, HEADS_PER_PAIR)
        feats = slice(p * LANES, (p + 1) * LANES)
        q2 = qT_ref[feats, :]
        qm = jnp.where(head_masks[e], q2, jnp.zeros_like(q2))
        sel = _moba_select(qm, km_ref[:, feats], c) if gated else None
        return _dot(k_ref[:, feats], qm), sel

    def attend(h, st, sel):
        parts = []
        for j in range(c):
            blk = st[j * BLK:(j + 1) * BLK]
            if gated:
                blk = jnp.where(sel[j:j + 1, :] > 0.5, blk, NEG_INF)
            parts.append(blk)
        parts.append(jnp.where(causal, st[c * BLK:nk], NEG_INF))
        st = jnp.concatenate(parts, axis=0) if c else parts[0]
        return _softmax_pv_t(st, vT_ref[h * HEAD_DIM:(h + 1) * HEAD_DIM, :])

    _pair_pipeline(
        N_PAIRS,
        lambda p: [scores(p * HEADS_PER_PAIR + e) for e in range(HEADS_PER_PAIR)],
        lambda p, cur: _store_pair(o_ref, gate_ref, p,
                                   [attend(p * HEADS_PER_PAIR + e, *cur[e])
                                    for e in range(HEADS_PER_PAIR)]))


def _moba(mqT, mk, mvT, kmean, gate, batch):
    mk3 = mk.reshape(batch, SEQ, ATT_W)
    acc = jnp.zeros((batch, SEQ, ATT_W), BF16)
    for c in range(N_BLK):
        nk = (c + 1) * BLK
        in_specs = [pl.BlockSpec((None, ATT_W, BLK), lambda b, c=c: (b, 0, c)),
                    pl.BlockSpec((None, nk, ATT_W), lambda b: (b, 0, 0)),
                    pl.BlockSpec((None, ATT_W, nk), lambda b: (b, 0, 0)),
                    pl.BlockSpec((None, N_BLK, ATT_W), lambda b: (b, 0, 0))]
        acc = _qblock_call(functools.partial(_moba_block, c), c, f"moba_q{c}", in_specs,
                           (mqT, mk3, mvT, kmean), gate.reshape(batch, SEQ, ATT_W), acc, batch)
    return acc.reshape(batch * SEQ, ATT_W)


def _key_to_f32(key):
    bits = key ^ ((key >> 31) & 0x7FFFFFFF)
    return lax.bitcast_convert_type(bits, F32)


def _count_ge(sc, thr):
    acc = jnp.zeros((CHAIN_ROWS, BLK), F32)
    for r in range(sc.shape[0] // CHAIN_ROWS):
        acc = acc + jnp.where(sc[r * CHAIN_ROWS:(r + 1) * CHAIN_ROWS] >= thr, 1.0, 0.0)
    return jnp.sum(acc, axis=0, keepdims=True)


POS_INF = float("inf")


def _dsa_topk_cap(sc, nk):
    kf = float(DSA_TOPK)
    cnt = _count_ge(sc, 0.0)
    t0 = jnp.where(cnt >= kf, 0, INT_MIN).astype(jnp.int32)

    def bit_step(it, t):
        cand = t | jnp.left_shift(jnp.int32(1), 30 - it)
        cnt = _count_ge(sc, _key_to_f32(cand))
        return jnp.where(cnt >= kf, cand, t)

    t = lax.fori_loop(0, 31, bit_step, t0)
    t_val = _key_to_f32(t)
    t_next = _key_to_f32(t + 1)
    need = kf - _count_ge(sc, t_next)
    r = lax.broadcasted_iota(jnp.int32, (BLK, BLK), 0)
    c = lax.broadcasted_iota(jnp.int32, (BLK, BLK), 1)
    strict_lower = jnp.where(c < r, 1.0, 0.0).astype(BF16)
    carry = jnp.zeros((1, BLK), F32)
    parts = []
    for j in range(nk // BLK):
        sc_j = sc[j * BLK:(j + 1) * BLK]
        gt = sc_j >= t_next
        eq = (sc_j >= t_val) & jnp.logical_not(gt)
        eq_f = jnp.where(eq, 1.0, 0.0)
        before = _dot(strict_lower, eq_f.astype(BF16)) + carry
        carry = carry + _col_reduce(eq_f, jnp.add, jnp.sum)
        parts.append(jnp.where(gt | (eq & (before < need)), POS_INF, NEG_INF))
    return jnp.concatenate(parts, axis=0)


def _dsa_block(c, iqT_ref, ik_ref, iwT_ref, qT_ref, k_ref, vT_ref, gate_ref, acc_ref, o_ref):
    del acc_ref
    nk = (c + 1) * BLK
    causal = _causal_block_mask_t()
    head_masks = _pair_head_masks()

    def masked_q_dots(keys, qT2):
        return [_dot(keys, jnp.where(m, qT2, jnp.zeros_like(qT2))) for m in head_masks]

    if c == 0:
        cap = jnp.where(causal, POS_INF, NEG_INF)
    else:
        wscale = IDX_HEADS ** -0.5
        acc = [jnp.zeros((nk, BLK), F32)]

        def accumulate(p, logits):
            for e in range(HEADS_PER_PAIR):
                w = iwT_ref[p * HEADS_PER_PAIR + e:p * HEADS_PER_PAIR + e + 1, :] * wscale
                acc[0] = acc[0] + jnp.maximum(logits[e], 0.0) * w

        _pair_pipeline(
            N_PAIRS,
            lambda p: masked_q_dots(ik_ref[...], iqT_ref[p * LANES:(p + 1) * LANES, :]),
            accumulate)
        own = jnp.where(causal, acc[0][c * BLK:nk], NEG_INF)
        sc = jnp.concatenate([acc[0][0:c * BLK], own], axis=0)
        cap = _dsa_topk_cap(sc, nk)

    def attend(p, scores):
        _store_pair(o_ref, gate_ref, p, [
            _softmax_pv_t(jnp.minimum(scores[e], cap),
                          vT_ref[p * LANES + e * HEAD_DIM:p * LANES + (e + 1) * HEAD_DIM, :])
            for e in range(HEADS_PER_PAIR)])

    _pair_pipeline(
        N_PAIRS,
        lambda p: masked_q_dots(k_ref[:, p * LANES:(p + 1) * LANES],
                                qT_ref[p * LANES:(p + 1) * LANES, :]),
        attend)


def _dsa(iqT, ik, iwT, dqT, dk, dvT, gate, batch):
    ik3 = ik.reshape(batch, SEQ, LANES)
    dk3 = dk.reshape(batch, SEQ, ATT_W)
    acc = jnp.zeros((batch, SEQ, ATT_W), BF16)
    for c in range(N_BLK):
        nk = (c + 1) * BLK
        qtile = lambda r, c=c: pl.BlockSpec((None, r, BLK), lambda b: (b, 0, c))
        keys = lambda w, nk=nk: pl.BlockSpec((None, nk, w), lambda b: (b, 0, 0))
        in_specs = [qtile(ATT_W), keys(LANES), qtile(BF16_ROWS), qtile(ATT_W), keys(ATT_W),
                    pl.BlockSpec((None, ATT_W, nk), lambda b: (b, 0, 0))]
        acc = _qblock_call(functools.partial(_dsa_block, c), c, f"dsa_q{c}", in_specs,
                           (iqT, ik3, iwT, dqT, dk3, dvT), gate.reshape(batch, SEQ, ATT_W), acc,
                           batch)
    return acc.reshape(batch * SEQ, ATT_W)


def _merge_body(x_ref, ya_ref, yb_ref, yc_ref, gates_ref, wa_ref, wb_ref, wc_ref, wo_ref, gf_ref,
                o_ref):
    u = jnp.zeros((BLK, D_MODEL), F32)
    for n, (y_ref, w_ref) in enumerate(((ya_ref, wa_ref), (yb_ref, wb_ref), (yc_ref, wc_ref))):
        sl = slice(n * D_MODEL, (n + 1) * D_MODEL)
        u = u + gates_ref[:, sl].astype(F32) * _dot(y_ref[...], w_ref[...])
    y = x_ref[...] + _dot(u.astype(BF16), wo_ref[...])
    o_ref[...] = _rmsnorm(y, gf_ref[...])


def _merge(x2, ya, yb, yc, gates, wa, wb, wc, wo, g_final):
    tile = lambda w: pl.BlockSpec((BLK, w), lambda i: (i, 0))
    const = lambda r, w: pl.BlockSpec((r, w), lambda i: (0, 0))
    m = x2.shape[0]
    return pl.pallas_call(
        _merge_body,
        grid=(m // BLK,),
        in_specs=[tile(D_MODEL), tile(ATT_W), tile(ATT_W), tile(X_W), tile(N_BRANCH * D_MODEL),
                  const(ATT_W, D_MODEL), const(ATT_W, D_MODEL), const(X_W, D_MODEL),
                  const(D_MODEL, D_MODEL), const(1, D_MODEL)],
        out_specs=tile(D_MODEL),
        out_shape=jax.ShapeDtypeStruct((m, D_MODEL), F32),
        compiler_params=pltpu.CompilerParams(
            dimension_semantics=("arbitrary",), vmem_limit_bytes=VMEM_LIMIT),
        name="merge",
    )(x2, ya, yb, yc, gates, wa, wb, wc, wo, g_final)


def _pack_w_in(w_in):
    offs = [0]
    for s in IN_SIZES:
        offs.append(offs[-1] + s)
    col = lambda n: w_in[:, offs[n]:offs[n + 1]]
    w_cols = jnp.concatenate([col(1), col(5), col(9), col(9), col(3), col(7), col(11), col(12),
                              col(13)], axis=1)
    iw = jnp.pad(col(10), ((0, 0), (0, BF16_ROWS - IDX_HEADS)))
    w_rows = jnp.concatenate([col(0), col(4), col(8), col(2), col(6), iw], axis=1).T
    return w_cols.astype(BF16), w_rows.astype(BF16)


def _rope_tables():
    inv = jnp.power(ROPE_THETA, -jnp.arange(HALF, dtype=F32) * 2.0 / HEAD_DIM)
    ang = jnp.arange(SEQ).astype(F32)[:, None] * inv[None, :]
    cos = jnp.cos(ang)
    sin = jnp.sin(ang)
    cos_lanes = jnp.tile(cos, (1, LANES // HALF))
    sin_lanes = jnp.tile(jnp.concatenate([-sin, sin], axis=1), (1, HEADS_PER_PAIR))
    return cos_lanes, sin_lanes, cos.T, sin.T


def _layer(x2, mem2, g_in, w_in, b_merge, g_mem, w_mem_kv, w_up_moba, w_up_dsa, w_up_cross, w_out,
           g_final, batch):
    w_cols, w_rows = _pack_w_in(w_in)
    xk, xv = _mem_proj(mem2, g_mem[None, :], w_mem_kv.astype(BF16))
    (mqT, dqT, iqT, mvT, dvT, iwT, mk, dk, ik, mgate, dgate, yc, gates, kmean) = _in_proj(
        x2, g_in[None, :], w_cols, w_rows, _rope_tables(), b_merge[None, :],
        xk.reshape(batch, MEM_LEN, X_W), xv.reshape(batch, MEM_LEN, X_W), batch)
    ya = _moba(mqT, mk, mvT, kmean.reshape(batch, N_BLK, ATT_W), mgate, batch)
    yb = _dsa(iqT, ik, iwT, dqT, dk, dvT, dgate, batch)
    return _merge(x2, ya, yb, yc, gates, w_up_moba.astype(BF16), w_up_dsa.astype(BF16),
                  w_up_cross.astype(BF16), w_out.astype(BF16), g_final)


def kernel(x, mem, g_in, w_in, b_merge, g_mem, w_mem_kv, w_up_moba, w_up_dsa, w_up_cross, w_out,
           g_final):
    batch, seq, d = x.shape
    assert seq == SEQ and d == D_MODEL and mem.shape[1] == MEM_LEN
    assert g_in.shape[0] == 1
    out = _layer(x.reshape(batch * seq, d), mem.reshape(batch * MEM_LEN, d), g_in[0], w_in[0],
                 b_merge[0], g_mem[0], w_mem_kv[0], w_up_moba[0], w_up_dsa[0], w_up_cross[0],
                 w_out[0], g_final[None, :], batch)
    return out.reshape(batch, seq, d)
```

```python
import functools

import jax
import jax.numpy as jnp
from jax import lax
from jax.experimental import pallas as pl
from jax.experimental.pallas import tpu as pltpu

D_MODEL = 1024
SEQ = 2048
HEAD_DIM = 64
HALF = HEAD_DIM // 2
N_HEADS = 8
ATT_W = N_HEADS * HEAD_DIM
BLK = 256
N_BLK = SEQ // BLK
PROJ_TILE = 2 * BLK
MOBA_TOPK = 3
DSA_TOPK = 256
IDX_HEADS = 8
MEM_LEN = 256
X_HEADS = 4
X_HEAD_DIM = 128
X_W = X_HEADS * X_HEAD_DIM
N_BRANCH = 3
ROPE_THETA = 10000.0
RMS_EPS = 1e-6
LANES = 128
HEADS_PER_PAIR = LANES // HEAD_DIM
N_PAIRS = ATT_W // LANES
BF16_ROWS = 16

IN_SIZES = (ATT_W, ATT_W, ATT_W, ATT_W, ATT_W, ATT_W, ATT_W, ATT_W,
            IDX_HEADS * HEAD_DIM, HEAD_DIM, IDX_HEADS, X_W, X_W, N_BRANCH * D_MODEL)

COL_MK, COL_DK, COL_IK = 0, 512, 1024
COL_MG, COL_DG, COL_XQ, COL_XG, COL_GL = 1152, 1664, 2176, 2688, 3200
N_COLS = COL_GL + N_BRANCH * D_MODEL
ROW_MQ, ROW_DQ, ROW_IQ, ROW_MV, ROW_DV, ROW_IW = 0, 512, 1024, 1536, 2048, 2560
N_ROWS = ROW_IW + BF16_ROWS

VMEM_LIMIT = 56 * 1024 * 1024

F32 = jnp.float32
BF16 = jnp.bfloat16
NEG_INF = float("-inf")
INT_MIN = -2 ** 31
LOG2E = 1.4426950408889634


def _dot(a, b):
    return jnp.dot(a, b, preferred_element_type=F32)


def _nt_dot(a, b):
    return lax.dot_general(a, b, (((1,), (1,)), ((), ())), preferred_element_type=F32)


def _rmsnorm(x, g):
    ms = jnp.mean(x * x, axis=-1, keepdims=True)
    return (x * lax.rsqrt(ms + RMS_EPS)) * g


def _softmax_pv(s, v):
    m = jnp.max(s, axis=1, keepdims=True)
    p = jnp.exp2(s - m)
    l = jnp.sum(p, axis=1, keepdims=True)
    return _dot(p.astype(BF16), v) / l


def _in_proj_body(x_ref, g_ref, wc_ref, wr_ref, cos_ref, sin_ref, cos_t_ref, sin_t_ref,
                  bm_ref, xk_ref, xv_ref,
                  mqT_ref, dqT_ref, iqT_ref, mvT_ref, dvT_ref, iwT_ref,
                  mk_ref, dk_ref, ik_ref, mg_ref, dg_ref, yc_ref, gl_ref, km_ref):
    h = _rmsnorm(x_ref[...], g_ref[...]).astype(BF16)
    qk_scale = HEAD_DIM ** -0.5
    att_scale = qk_scale * LOG2E

    def seg(off, width):
        return lambda: _dot(h, wc_ref[:, off:off + width])

    kept = {}

    def keep_silu_xg(z):
        kept["xg"] = jax.nn.silu(z)

    def cross_attention(z):
        xq = z.astype(BF16)
        xscale = X_HEAD_DIM ** -0.5 * LOG2E
        yc_parts = []
        for hd in range(X_HEADS):
            sl = slice(hd * X_HEAD_DIM, (hd + 1) * X_HEAD_DIM)
            s = _nt_dot(xq[:, sl], xk_ref[:, sl]) * xscale
            yc_parts.append(_softmax_pv(s, xv_ref[:, sl]))
        yc_ref[...] = (jnp.concatenate(yc_parts, axis=1) * kept["xg"]).astype(BF16)

    def store_merge_gate(c):
        sl = slice(c * D_MODEL, (c + 1) * D_MODEL)

        def epilogue(z):
            gl_ref[:, sl] = jax.nn.sigmoid(z + bm_ref[:, sl]).astype(BF16)
        return epilogue

    def store_silu(ref):
        def epilogue(z):
            ref[...] = jax.nn.silu(z).astype(BF16)
        return epilogue

    def store_feature_major(zt):
        cos_t = cos_t_ref[...]
        sin_t = sin_t_ref[...]
        for ref, row0, scale in ((mqT_ref, ROW_MQ, att_scale), (dqT_ref, ROW_DQ, att_scale),
                                 (iqT_ref, ROW_IQ, qk_scale)):
            for hd in range(N_HEADS):
                r = row0 + hd * HEAD_DIM
                x1 = zt[r:r + HALF]
                x2 = zt[r + HALF:r + HEAD_DIM]
                o = hd * HEAD_DIM
                ref[o:o + HALF, :] = ((x1 * cos_t - x2 * sin_t) * scale).astype(BF16)
                ref[o + HALF:o + HEAD_DIM, :] = ((x2 * cos_t + x1 * sin_t) * scale).astype(BF16)
        mvT_ref[...] = zt[ROW_MV:ROW_MV + ATT_W].astype(BF16)
        dvT_ref[...] = zt[ROW_DV:ROW_DV + ATT_W].astype(BF16)
        iwT_ref[...] = zt[ROW_IW:ROW_IW + BF16_ROWS]

    def store_roped(ref, mean_ref=None):
        def epilogue(z):
            cos = cos_ref[...]
            sin = sin_ref[...]
            lane = lax.broadcasted_iota(jnp.int32, (PROJ_TILE, LANES), 1)
            first_half = (lane & HALF) == 0
            for c in range(z.shape[1] // LANES):
                lanes = slice(c * LANES, (c + 1) * LANES)
                zc = z[:, lanes]
                partner = jnp.where(first_half, pltpu.roll(zc, LANES - HALF, 1),
                                    pltpu.roll(zc, HALF, 1))
                r = zc * cos + partner * sin
                if mean_ref is not None:
                    for blk in range(PROJ_TILE // BLK):
                        mean_ref[blk, :, lanes] = jnp.mean(r[blk * BLK:(blk + 1) * BLK], axis=0,
                                                           keepdims=True)
                ref[:, lanes] = r.astype(ref.dtype)
        return epilogue

    stages = [(seg(COL_XG, X_W), keep_silu_xg), (seg(COL_XQ, X_W), cross_attention)]
    stages += [(seg(COL_GL + c * D_MODEL, D_MODEL), store_merge_gate(c)) for c in range(N_BRANCH)]
    stages += [(seg(COL_MG, ATT_W), store_silu(mg_ref)), (seg(COL_DG, ATT_W), store_silu(dg_ref)),
               (lambda: _nt_dot(wr_ref[...], h), store_feature_major),
               (seg(COL_MK, ATT_W), store_roped(mk_ref, mean_ref=km_ref)),
               (seg(COL_DK, ATT_W), store_roped(dk_ref)),
               (seg(COL_IK, LANES), store_roped(ik_ref))]
    nxt = stages[0][0]()
    for i, (_, epilogue) in enumerate(stages):
        cur = nxt
        if i + 1 < len(stages):
            nxt = stages[i + 1][0]()
        epilogue(cur)


def _in_proj(x2, g_in, w_cols, w_rows, tables, b_merge, xk, xv, batch):
    m = x2.shape[0]
    n_tiles = m // PROJ_TILE
    per_seq = SEQ // PROJ_TILE
    row = lambda w: pl.BlockSpec((PROJ_TILE, w), lambda i: (i, 0))
    feat = lambda r: pl.BlockSpec((None, r, PROJ_TILE), lambda i: (i // per_seq, 0, i % per_seq))
    const = lambda shape: pl.BlockSpec(shape, lambda i: (0, 0), pipeline_mode=pl.Buffered(1))
    tm = lambda w, dt: jax.ShapeDtypeStruct((m, w), dt)
    fm = lambda r, dt: jax.ShapeDtypeStruct((batch, r, SEQ), dt)
    out_shape = [
        fm(ATT_W, BF16), fm(ATT_W, BF16), fm(ATT_W, BF16),
        fm(ATT_W, BF16), fm(ATT_W, BF16), fm(BF16_ROWS, F32),
        tm(ATT_W, BF16), tm(ATT_W, BF16), tm(LANES, BF16),
        tm(ATT_W, BF16), tm(ATT_W, BF16),
        tm(X_W, BF16),
        tm(N_BRANCH * D_MODEL, BF16),
        jax.ShapeDtypeStruct((m // BLK, 1, ATT_W), F32),
    ]
    out_specs = [feat(ATT_W)] * 5 + [feat(BF16_ROWS)] + [
        row(ATT_W), row(ATT_W), row(LANES), row(ATT_W), row(ATT_W), row(X_W),
        row(N_BRANCH * D_MODEL),
        pl.BlockSpec((PROJ_TILE // BLK, 1, ATT_W), lambda i: (i, 0, 0))]
    mem = pl.BlockSpec((None, MEM_LEN, X_W), lambda i: (i // per_seq, 0, 0))
    cos, sin, cos_t, sin_t = tables
    return pl.pallas_call(
        _in_proj_body,
        grid=(n_tiles,),
        in_specs=[
            pl.BlockSpec((PROJ_TILE, D_MODEL), lambda i: (i, 0)),
            pl.BlockSpec((1, D_MODEL), lambda i: (0, 0)),
            const((D_MODEL, N_COLS)),
            const((N_ROWS, D_MODEL)),
            pl.BlockSpec((PROJ_TILE, LANES), lambda i: (i % per_seq, 0)),
            pl.BlockSpec((PROJ_TILE, LANES), lambda i: (i % per_seq, 0)),
            pl.BlockSpec((HALF, PROJ_TILE), lambda i: (0, i % per_seq)),
            pl.BlockSpec((HALF, PROJ_TILE), lambda i: (0, i % per_seq)),
            pl.BlockSpec((1, N_BRANCH * D_MODEL), lambda i: (0, 0)),
            mem, mem,
        ],
        out_specs=out_specs,
        out_shape=out_shape,
        compiler_params=pltpu.CompilerParams(
            dimension_semantics=("arbitrary",), vmem_limit_bytes=VMEM_LIMIT),
        name="in_proj",
    )(x2, g_in, w_cols, w_rows, cos, sin, cos_t, sin_t, b_merge, xk, xv)


def _mem_proj_body(m_ref, g_ref, w_ref, k_ref, v_ref):
    h = _rmsnorm(m_ref[...], g_ref[...]).astype(BF16)
    k_ref[...] = _dot(h, w_ref[:, 0:X_W]).astype(BF16)
    v_ref[...] = _dot(h, w_ref[:, X_W:2 * X_W]).astype(BF16)


def _mem_proj(mem2, g_mem, w_kv):
    m = mem2.shape[0]
    return pl.pallas_call(
        _mem_proj_body,
        grid=(m // BLK,),
        in_specs=[
            pl.BlockSpec((BLK, D_MODEL), lambda i: (i, 0)),
            pl.BlockSpec((1, D_MODEL), lambda i: (0, 0)),
            pl.BlockSpec((D_MODEL, 2 * X_W), lambda i: (0, 0)),
        ],
        out_specs=[pl.BlockSpec((BLK, X_W), lambda i: (i, 0))] * 2,
        out_shape=[jax.ShapeDtypeStruct((m, X_W), BF16)] * 2,
        compiler_params=pltpu.CompilerParams(
            dimension_semantics=("arbitrary",), vmem_limit_bytes=VMEM_LIMIT),
        name="mem_proj",
    )(mem2, g_mem, w_kv)


def _pair_head_masks():
    r = lax.broadcasted_iota(jnp.int32, (LANES, BLK), 0)
    return [(r >= e * HEAD_DIM) & (r < (e + 1) * HEAD_DIM) for e in range(HEADS_PER_PAIR)]


def _causal_block_mask_t():
    kr = lax.broadcasted_iota(jnp.int32, (BLK, BLK), 0)
    qc = lax.broadcasted_iota(jnp.int32, (BLK, BLK), 1)
    return kr <= qc


CHAIN_ROWS = 64


def _col_reduce(x, combine, finish):
    rows = x.shape[0]
    acc = x[0:CHAIN_ROWS]
    for r in range(1, rows // CHAIN_ROWS):
        acc = combine(acc, x[r * CHAIN_ROWS:(r + 1) * CHAIN_ROWS])
    return finish(acc, axis=0, keepdims=True)


def _softmax_pv_t(st, vt):
    m = _col_reduce(st, jnp.maximum, jnp.max)
    p = jnp.exp2(st - m)
    l = _col_reduce(p, jnp.add, jnp.sum)
    return _dot(vt, p.astype(BF16)) / l


def _store_pair(o_ref, gate_ref, p, halves):
    lanes = slice(p * LANES, (p + 1) * LANES)
    y = jnp.concatenate(halves, axis=0).T
    o_ref[:, lanes] = (y * gate_ref[:, lanes].astype(F32)).astype(o_ref.dtype)


def _qblock_call(body, c, name, in_specs, args, gate, acc, batch):
    out_tile = pl.BlockSpec((None, BLK, ATT_W), lambda b: (b, c, 0))
    return pl.pallas_call(
        body,
        grid=(batch,),
        in_specs=in_specs + [out_tile, pl.BlockSpec(memory_space=pl.ANY)],
        out_specs=out_tile,
        out_shape=jax.ShapeDtypeStruct(acc.shape, acc.dtype),
        input_output_aliases={len(args) + 1: 0},
        compiler_params=pltpu.CompilerParams(
            dimension_semantics=("arbitrary",), vmem_limit_bytes=VMEM_LIMIT),
        name=name,
    )(*args, gate, acc)


def _pair_pipeline(n_pairs, issue, consume):
    nxt = issue(0)
    for p in range(n_pairs):
        cur = nxt
        if p + 1 < n_pairs:
            nxt = issue(p + 1)
        consume(p, cur)


def _moba_select(qm, km, n_past):
    km16 = jnp.concatenate([km, jnp.zeros((BF16_ROWS - N_BLK, LANES), F32)], axis=0).astype(BF16)
    gate = _dot(km16, qm)
    blk = lax.broadcasted_iota(jnp.int32, (BF16_ROWS, BLK), 0)
    past = blk < n_past
    gate = jnp.where(past, gate, NEG_INF)
    rank = jnp.zeros((BF16_ROWS, BLK), F32)
    for jp in range(n_past):
        other = gate[jp:jp + 1, :]
        beats = (other > gate) | ((other == gate) & (blk > jp))
        rank = rank + jnp.where(beats, 1.0, 0.0)
    return jnp.where(past & (rank < MOBA_TOPK), 1.0, 0.0)


def _moba_block(c, qT_ref, k_ref, vT_ref, km_ref, gate_ref, acc_ref, o_ref):
    del acc_ref
    nk = (c + 1) * BLK
    causal = _causal_block_mask_t()
    head_masks = _pair_head_masks()
    gated = c > MOBA_TOPK

    def scores(h):
        p, e = divmod(h, HEADS_PER_PAIR)
        feats = slice(p * LANES, (p + 1) * LANES)
        q2 = qT_ref[feats, :]
        qm = jnp.where(head_masks[e], q2, jnp.zeros_like(q2))
        sel = _moba_select(qm, km_ref[:, feats], c) if gated else None
        return _dot(k_ref[:, feats], qm), sel

    def attend(h, st, sel):
        parts = []
        for j in range(c):
            blk = st[j * BLK:(j + 1) * BLK]
            if gated:
                blk = jnp.where(sel[j:j + 1, :] > 0.5, blk, NEG_INF)
            parts.append(blk)
        parts.append(jnp.where(causal, st[c * BLK:nk], NEG_INF))
        st = jnp.concatenate(parts, axis=0) if c else parts[0]
        return _softmax_pv_t(st, vT_ref[h * HEAD_DIM:(h + 1) * HEAD_DIM, :])

    _pair_pipeline(
        N_PAIRS,
        lambda p: [scores(p * HEADS_PER_PAIR + e) for e in range(HEADS_PER_PAIR)],
        lambda p, cur: _store_pair(o_ref, gate_ref, p,
                                   [attend(p * HEADS_PER_PAIR + e, *cur[e])
                                    for e in range(HEADS_PER_PAIR)]))


def _moba(mqT, mk, mvT, kmean, gate, batch):
    mk3 = mk.reshape(batch, SEQ, ATT_W)
    acc = jnp.zeros((batch, SEQ, ATT_W), BF16)
    for c in range(N_BLK):
        nk = (c + 1) * BLK
        in_specs = [pl.BlockSpec((None, ATT_W, BLK), lambda b, c=c: (b, 0, c)),
                    pl.BlockSpec((None, nk, ATT_W), lambda b: (b, 0, 0)),
                    pl.BlockSpec((None, ATT_W, nk), lambda b: (b, 0, 0)),
                    pl.BlockSpec((None, N_BLK, ATT_W), lambda b: (b, 0, 0))]
        acc = _qblock_call(functools.partial(_moba_block, c), c, f"moba_q{c}", in_specs,
                           (mqT, mk3, mvT, kmean), gate.reshape(batch, SEQ, ATT_W), acc, batch)
    return acc.reshape(batch * SEQ, ATT_W)


def _key_to_f32(key):
    bits = key ^ ((key >> 31) & 0x7FFFFFFF)
    return lax.bitcast_convert_type(bits, F32)


def _count_ge(sc, thr):
    acc = jnp.zeros((CHAIN_ROWS, BLK), F32)
    for r in range(sc.shape[0] // CHAIN_ROWS):
        acc = acc + jnp.where(sc[r * CHAIN_ROWS:(r + 1) * CHAIN_ROWS] >= thr, 1.0, 0.0)
    return jnp.sum(acc, axis=0, keepdims=True)


POS_INF = float("inf")


def _dsa_topk_cap(sc, nk):
    kf = float(DSA_TOPK)
    cnt = _count_ge(sc, 0.0)
    t0 = jnp.where(cnt >= kf, 0, INT_MIN).astype(jnp.int32)

    def bit_step(it, t):
        cand = t | jnp.left_shift(jnp.int32(1), 30 - it)
        cnt = _count_ge(sc, _key_to_f32(cand))
        return jnp.where(cnt >= kf, cand, t)

    t = lax.fori_loop(0, 31, bit_step, t0)
    t_val = _key_to_f32(t)
    t_next = _key_to_f32(t + 1)
    need = kf - _count_ge(sc, t_next)
    r = lax.broadcasted_iota(jnp.int32, (BLK, BLK), 0)
    c = lax.broadcasted_iota(jnp.int32, (BLK, BLK), 1)
    strict_lower = jnp.where(c < r, 1.0, 0.0).astype(BF16)
    carry = jnp.zeros((1, BLK), F32)
    parts = []
    for j in range(nk // BLK):
        sc_j = sc[j * BLK:(j + 1) * BLK]
        gt = sc_j >= t_next
        eq = (sc_j >= t_val) & jnp.logical_not(gt)
        eq_f = jnp.where(eq, 1.0, 0.0)
        before = _dot(strict_lower, eq_f.astype(BF16)) + carry
        carry = carry + _col_reduce(eq_f, jnp.add, jnp.sum)
        parts.append(jnp.where(gt | (eq & (before < need)), POS_INF, NEG_INF))
    return jnp.concatenate(parts, axis=0)


def _dsa_block(c, iqT_ref, ik_ref, iwT_ref, qT_ref, k_ref, vT_ref, gate_ref, acc_ref, o_ref):
    del acc_ref
    nk = (c + 1) * BLK
    causal = _causal_block_mask_t()
    head_masks = _pair_head_masks()

    def masked_q_dots(keys, qT2):
        return [_dot(keys, jnp.where(m, qT2, jnp.zeros_like(qT2))) for m in head_masks]

    if c == 0:
        cap = jnp.where(causal, POS_INF, NEG_INF)
    else:
        wscale = IDX_HEADS ** -0.5
        acc = [jnp.zeros((nk, BLK), F32)]

        def accumulate(p, logits):
            for e in range(HEADS_PER_PAIR):
                w = iwT_ref[p * HEADS_PER_PAIR + e:p * HEADS_PER_PAIR + e + 1, :] * wscale
                acc[0] = acc[0] + jnp.maximum(logits[e], 0.0) * w

        _pair_pipeline(
            N_PAIRS,
            lambda p: masked_q_dots(ik_ref[...], iqT_ref[p * LANES:(p + 1) * LANES, :]),
            accumulate)
        own = jnp.where(causal, acc[0][c * BLK:nk], NEG_INF)
        sc = jnp.concatenate([acc[0][0:c * BLK], own], axis=0)
        cap = _dsa_topk_cap(sc, nk)

    def attend(p, scores):
        _store_pair(o_ref, gate_ref, p, [
            _softmax_pv_t(jnp.minimum(scores[e], cap),
                          vT_ref[p * LANES + e * HEAD_DIM:p * LANES + (e + 1) * HEAD_DIM, :])
            for e in range(HEADS_PER_PAIR)])

    _pair_pipeline(
        N_PAIRS,
        lambda p: masked_q_dots(k_ref[:, p * LANES:(p + 1) * LANES],
                                qT_ref[p * LANES:(p + 1) * LANES, :]),
        attend)


def _dsa(iqT, ik, iwT, dqT, dk, dvT, gate, batch):
    ik3 = ik.reshape(batch, SEQ, LANES)
    dk3 = dk.reshape(batch, SEQ, ATT_W)
    acc = jnp.zeros((batch, SEQ, ATT_W), BF16)
    for c in range(N_BLK):
        nk = (c + 1) * BLK
        qtile = lambda r, c=c: pl.BlockSpec((None, r, BLK), lambda b: (b, 0, c))
        keys = lambda w, nk=nk: pl.BlockSpec((None, nk, w), lambda b: (b, 0, 0))
        in_specs = [qtile(ATT_W), keys(LANES), qtile(BF16_ROWS), qtile(ATT_W), keys(ATT_W),
                    pl.BlockSpec((None, ATT_W, nk), lambda b: (b, 0, 0))]
        acc = _qblock_call(functools.partial(_dsa_block, c), c, f"dsa_q{c}", in_specs,
                           (iqT, ik3, iwT, dqT, dk3, dvT), gate.reshape(batch, SEQ, ATT_W), acc,
                           batch)
    return acc.reshape(batch * SEQ, ATT_W)


def _merge_body(x_ref, ya_ref, yb_ref, yc_ref, gates_ref, wa_ref, wb_ref, wc_ref, wo_ref, gf_ref,
                o_ref):
    u = jnp.zeros((PROJ_TILE, D_MODEL), F32)
    for n, (y_ref, w_ref) in enumerate(((ya_ref, wa_ref), (yb_ref, wb_ref), (yc_ref, wc_ref))):
        sl = slice(n * D_MODEL, (n + 1) * D_MODEL)
        u = u + gates_ref[:, sl].astype(F32) * _dot(y_ref[...], w_ref[...])
    y = x_ref[...] + _dot(u.astype(BF16), wo_ref[...])
    o_ref[...] = _rmsnorm(y, gf_ref[...])


def _merge(x2, ya, yb, yc, gates, wa, wb, wc, wo, g_final):
    tile = lambda w: pl.BlockSpec((PROJ_TILE, w), lambda i: (i, 0))
    const = lambda r, w: pl.BlockSpec((r, w), lambda i: (0, 0))
    m = x2.shape[0]
    return pl.pallas_call(
        _merge_body,
        grid=(m // PROJ_TILE,),
        in_specs=[tile(D_MODEL), tile(ATT_W), tile(ATT_W), tile(X_W), tile(N_BRANCH * D_MODEL),
                  const(ATT_W, D_MODEL), const(ATT_W, D_MODEL), const(X_W, D_MODEL),
                  const(D_MODEL, D_MODEL), const(1, D_MODEL)],
        out_specs=tile(D_MODEL),
        out_shape=jax.ShapeDtypeStruct((m, D_MODEL), F32),
        compiler_params=pltpu.CompilerParams(
            dimension_semantics=("arbitrary",), vmem_limit_bytes=VMEM_LIMIT),
        name="merge",
    )(x2, ya, yb, yc, gates, wa, wb, wc, wo, g_final)


def _pack_w_in(w_in):
    offs = [0]
    for s in IN_SIZES:
        offs.append(offs[-1] + s)
    col = lambda n: w_in[:, offs[n]:offs[n + 1]]
    w_cols = jnp.concatenate([col(1), col(5), col(9), col(9), col(3), col(7), col(11), col(12),
                              col(13)], axis=1)
    iw = jnp.pad(col(10), ((0, 0), (0, BF16_ROWS - IDX_HEADS)))
    w_rows = jnp.concatenate([col(0), col(4), col(8), col(2), col(6), iw], axis=1).T
    return w_cols.astype(BF16), w_rows.astype(BF16)


def _rope_tables():
    inv = jnp.power(ROPE_THETA, -jnp.arange(HALF, dtype=F32) * 2.0 / HEAD_DIM)
    ang = jnp.arange(SEQ).astype(F32)[:, None] * inv[None, :]
    cos = jnp.cos(ang)
    sin = jnp.sin(ang)
    cos_lanes = jnp.tile(cos, (1, LANES // HALF))
    sin_lanes = jnp.tile(jnp.concatenate([-sin, sin], axis=1), (1, HEADS_PER_PAIR))
    return cos_lanes, sin_lanes, cos.T, sin.T


def _layer(x2, mem2, g_in, w_in, b_merge, g_mem, w_mem_kv, w_up_moba, w_up_dsa, w_up_cross, w_out,
           g_final, batch):
    w_cols, w_rows = _pack_w_in(w_in)
    xk, xv = _mem_proj(mem2, g_mem[None, :], w_mem_kv.astype(BF16))
    (mqT, dqT, iqT, mvT, dvT, iwT, mk, dk, ik, mgate, dgate, yc, gates, kmean) = _in_proj(
        x2, g_in[None, :], w_cols, w_rows, _rope_tables(), b_merge[None, :],
        xk.reshape(batch, MEM_LEN, X_W), xv.reshape(batch, MEM_LEN, X_W), batch)
    ya = _moba(mqT, mk, mvT, kmean.reshape(batch, N_BLK, ATT_W), mgate, batch)
    yb = _dsa(iqT, ik, iwT, dqT, dk, dvT, dgate, batch)
    return _merge(x2, ya, yb, yc, gates, w_up_moba.astype(BF16), w_up_dsa.astype(BF16),
                  w_up_cross.astype(BF16), w_out.astype(BF16), g_final)


def kernel(x, mem, g_in, w_in, b_merge, g_mem, w_mem_kv, w_up_moba, w_up_dsa, w_up_cross, w_out,
           g_final):
    batch, seq, d = x.shape
    assert seq == SEQ and d == D_MODEL and mem.shape[1] == MEM_LEN
    assert g_in.shape[0] == 1
    out = _layer(x.reshape(batch * seq, d), mem.reshape(batch * MEM_LEN, d), g_in[0], w_in[0],
                 b_merge[0], g_mem[0], w_mem_kv[0], w_up_moba[0], w_up_dsa[0], w_up_cross[0],
                 w_out[0], g_final[None, :], batch)
    return out.reshape(batch, seq, d)
```

```python
import functools

import jax
import jax.numpy as jnp
from jax import lax
from jax.experimental import pallas as pl
from jax.experimental.pallas import tpu as pltpu

D_MODEL = 1024
SEQ = 2048
HEAD_DIM = 64
HALF = HEAD_DIM // 2
N_HEADS = 8
ATT_W = N_HEADS * HEAD_DIM
BLK = 256
N_BLK = SEQ // BLK
PROJ_TILE = 2 * BLK
MOBA_TOPK = 3
DSA_TOPK = 256
IDX_HEADS = 8
MEM_LEN = 256
X_HEADS = 4
X_HEAD_DIM = 128
X_W = X_HEADS * X_HEAD_DIM
N_BRANCH = 3
ROPE_THETA = 10000.0
RMS_EPS = 1e-6
LANES = 128
HEADS_PER_PAIR = LANES // HEAD_DIM
N_PAIRS = ATT_W // LANES
BF16_ROWS = 16

IN_SIZES = (ATT_W, ATT_W, ATT_W, ATT_W, ATT_W, ATT_W, ATT_W, ATT_W,
            IDX_HEADS * HEAD_DIM, HEAD_DIM, IDX_HEADS, X_W, X_W, N_BRANCH * D_MODEL)

COL_MK, COL_DK, COL_IK = 0, 512, 1024
COL_MG, COL_DG, COL_XQ, COL_XG, COL_GL = 1152, 1664, 2176, 2688, 3200
N_COLS = COL_GL + N_BRANCH * D_MODEL
ROW_MQ, ROW_DQ, ROW_IQ, ROW_MV, ROW_DV, ROW_IW = 0, 512, 1024, 1536, 2048, 2560
N_ROWS = ROW_IW + BF16_ROWS

VMEM_LIMIT = 56 * 1024 * 1024

F32 = jnp.float32
BF16 = jnp.bfloat16
NEG_INF = float("-inf")
INT_MIN = -2 ** 31
LOG2E = 1.4426950408889634


def _dot(a, b):
    return jnp.dot(a, b, preferred_element_type=F32)


def _nt_dot(a, b):
    return lax.dot_general(a, b, (((1,), (1,)), ((), ())), preferred_element_type=F32)


def _rmsnorm(x, g):
    ms = jnp.mean(x * x, axis=-1, keepdims=True)
    return (x * lax.rsqrt(ms + RMS_EPS)) * g


def _softmax_pv(s, v):
    m = jnp.max(s, axis=1, keepdims=True)
    p = jnp.exp2(s - m)
    l = jnp.sum(p, axis=1, keepdims=True)
    return _dot(p.astype(BF16), v) / l


def _in_proj_body(x_ref, g_ref, wc_ref, wr_ref, cos_ref, sin_ref, cos_t_ref, sin_t_ref,
                  bm_ref, xk_ref, xv_ref,
                  mqT_ref, dqT_ref, iqT_ref, mvT_ref, dvT_ref, iwT_ref,
                  mk_ref, dk_ref, ik_ref, mg_ref, dg_ref, yc_ref, gl_ref, km_ref):
    h = _rmsnorm(x_ref[...], g_ref[...]).astype(BF16)
    qk_scale = HEAD_DIM ** -0.5
    att_scale = qk_scale * LOG2E

    def seg(off, width):
        return lambda: _dot(h, wc_ref[:, off:off + width])

    kept = {}

    def keep_silu_xg(z):
        kept["xg"] = jax.nn.silu(z)

    def cross_attention(z):
        xq = z.astype(BF16)
        xscale = X_HEAD_DIM ** -0.5 * LOG2E
        yc_parts = []
        for hd in range(X_HEADS):
            sl = slice(hd * X_HEAD_DIM, (hd + 1) * X_HEAD_DIM)
            s = _nt_dot(xq[:, sl], xk_ref[:, sl]) * xscale
            yc_parts.append(_softmax_pv(s, xv_ref[:, sl]))
        yc_ref[...] = (jnp.concatenate(yc_parts, axis=1) * kept["xg"]).astype(BF16)

    def store_merge_gate(c):
        sl = slice(c * D_MODEL, (c + 1) * D_MODEL)

        def epilogue(z):
            gl_ref[:, sl] = jax.nn.sigmoid(z + bm_ref[:, sl]).astype(BF16)
        return epilogue

    def store_silu(ref):
        def epilogue(z):
            ref[...] = jax.nn.silu(z).astype(BF16)
        return epilogue

    def store_feature_major(zt):
        cos_t = cos_t_ref[...]
        sin_t = sin_t_ref[...]
        for ref, row0, scale in ((mqT_ref, ROW_MQ, att_scale), (dqT_ref, ROW_DQ, att_scale),
                                 (iqT_ref, ROW_IQ, qk_scale)):
            for hd in range(N_HEADS):
                r = row0 + hd * HEAD_DIM
                x1 = zt[r:r + HALF]
                x2 = zt[r + HALF:r + HEAD_DIM]
                o = hd * HEAD_DIM
                ref[o:o + HALF, :] = ((x1 * cos_t - x2 * sin_t) * scale).astype(BF16)
                ref[o + HALF:o + HEAD_DIM, :] = ((x2 * cos_t + x1 * sin_t) * scale).astype(BF16)
        mvT_ref[...] = zt[ROW_MV:ROW_MV + ATT_W].astype(BF16)
        dvT_ref[...] = zt[ROW_DV:ROW_DV + ATT_W].astype(BF16)
        iwT_ref[...] = zt[ROW_IW:ROW_IW + BF16_ROWS]

    def store_roped(ref, mean_ref=None):
        def epilogue(z):
            cos = cos_ref[...]
            sin = sin_ref[...]
            lane = lax.broadcasted_iota(jnp.int32, (PROJ_TILE, LANES), 1)
            first_half = (lane & HALF) == 0
            for c in range(z.shape[1] // LANES):
                lanes = slice(c * LANES, (c + 1) * LANES)
                zc = z[:, lanes]
                partner = jnp.where(first_half, pltpu.roll(zc, LANES - HALF, 1),
                                    pltpu.roll(zc, HALF, 1))
                r = zc * cos + partner * sin
                if mean_ref is not None:
                    for blk in range(PROJ_TILE // BLK):
                        mean_ref[blk, :, lanes] = jnp.mean(r[blk * BLK:(blk + 1) * BLK], axis=0,
                                                           keepdims=True)
                ref[:, lanes] = r.astype(ref.dtype)
        return epilogue

    stages = [(seg(COL_XG, X_W), keep_silu_xg), (seg(COL_XQ, X_W), cross_attention)]
    stages += [(seg(COL_GL + c * D_MODEL, D_MODEL), store_merge_gate(c)) for c in range(N_BRANCH)]
    stages += [(seg(COL_MG, ATT_W), store_silu(mg_ref)), (seg(COL_DG, ATT_W), store_silu(dg_ref)),
               (lambda: _nt_dot(wr_ref[...], h), store_feature_major),
               (seg(COL_MK, ATT_W), store_roped(mk_ref, mean_ref=km_ref)),
               (seg(COL_DK, ATT_W), store_roped(dk_ref)),
               (seg(COL_IK, LANES), store_roped(ik_ref))]
    nxt = stages[0][0]()
    for i, (_, epilogue) in enumerate(stages):
        cur = nxt
        if i + 1 < len(stages):
            nxt = stages[i + 1][0]()
        epilogue(cur)


def _in_proj(x2, g_in, w_cols, w_rows, tables, b_merge, xk, xv, batch):
    m = x2.shape[0]
    n_tiles = m // PROJ_TILE
    per_seq = SEQ // PROJ_TILE
    row = lambda w: pl.BlockSpec((PROJ_TILE, w), lambda i: (i, 0))
    feat = lambda r: pl.BlockSpec((None, r, PROJ_TILE), lambda i: (i // per_seq, 0, i % per_seq))
    const = lambda shape: pl.BlockSpec(shape, lambda i: (0, 0), pipeline_mode=pl.Buffered(1))
    tm = lambda w, dt: jax.ShapeDtypeStruct((m, w), dt)
    fm = lambda r, dt: jax.ShapeDtypeStruct((batch, r, SEQ), dt)
    out_shape = [
        fm(ATT_W, BF16), fm(ATT_W, BF16), fm(ATT_W, BF16),
        fm(ATT_W, BF16), fm(ATT_W, BF16), fm(BF16_ROWS, F32),
        tm(ATT_W, BF16), tm(ATT_W, BF16), tm(LANES, BF16),
        tm(ATT_W, BF16), tm(ATT_W, BF16),
        tm(X_W, BF16),
        tm(N_BRANCH * D_MODEL, BF16),
        jax.ShapeDtypeStruct((m // BLK, 1, ATT_W), F32),
    ]
    out_specs = [feat(ATT_W)] * 5 + [feat(BF16_ROWS)] + [
        row(ATT_W), row(ATT_W), row(LANES), row(ATT_W), row(ATT_W), row(X_W),
        row(N_BRANCH * D_MODEL),
        pl.BlockSpec((PROJ_TILE // BLK, 1, ATT_W), lambda i: (i, 0, 0))]
    mem = pl.BlockSpec((None, MEM_LEN, X_W), lambda i: (i // per_seq, 0, 0))
    cos, sin, cos_t, sin_t = tables
    return pl.pallas_call(
        _in_proj_body,
        grid=(n_tiles,),
        in_specs=[
            pl.BlockSpec((PROJ_TILE, D_MODEL), lambda i: (i, 0)),
            pl.BlockSpec((1, D_MODEL), lambda i: (0, 0)),
            const((D_MODEL, N_COLS)),
            const((N_ROWS, D_MODEL)),
            pl.BlockSpec((PROJ_TILE, LANES), lambda i: (i % per_seq, 0)),
            pl.BlockSpec((PROJ_TILE, LANES), lambda i: (i % per_seq, 0)),
            pl.BlockSpec((HALF, PROJ_TILE), lambda i: (0, i % per_seq)),
            pl.BlockSpec((HALF, PROJ_TILE), lambda i: (0, i % per_seq)),
            pl.BlockSpec((1, N_BRANCH * D_MODEL), lambda i: (0, 0)),
            mem, mem,
        ],
        out_specs=out_specs,
        out_shape=out_shape,
        compiler_params=pltpu.CompilerParams(
            dimension_semantics=("arbitrary",), vmem_limit_bytes=VMEM_LIMIT),
        name="in_proj",
    )(x2, g_in, w_cols, w_rows, cos, sin, cos_t, sin_t, b_merge, xk, xv)


def _mem_proj_body(m_ref, g_ref, w_ref, k_ref, v_ref):
    h = _rmsnorm(m_ref[...], g_ref[...]).astype(BF16)
    k_ref[...] = _dot(h, w_ref[:, 0:X_W]).astype(BF16)
    v_ref[...] = _dot(h, w_ref[:, X_W:2 * X_W]).astype(BF16)


def _mem_proj(mem2, g_mem, w_kv):
    m = mem2.shape[0]
    return pl.pallas_call(
        _mem_proj_body,
        grid=(m // BLK,),
        in_specs=[
            pl.BlockSpec((BLK, D_MODEL), lambda i: (i, 0)),
            pl.BlockSpec((1, D_MODEL), lambda i: (0, 0)),
            pl.BlockSpec((D_MODEL, 2 * X_W), lambda i: (0, 0)),
        ],
        out_specs=[pl.BlockSpec((BLK, X_W), lambda i: (i, 0))] * 2,
        out_shape=[jax.ShapeDtypeStruct((m, X_W), BF16)] * 2,
        compiler_params=pltpu.CompilerParams(
            dimension_semantics=("arbitrary",), vmem_limit_bytes=VMEM_LIMIT),
        name="mem_proj",
    )(mem2, g_mem, w_kv)


def _pair_head_masks():
    r = lax.broadcasted_iota(jnp.int32, (LANES, BLK), 0)
    return [(r >= e * HEAD_DIM) & (r < (e + 1) * HEAD_DIM) for e in range(HEADS_PER_PAIR)]


def _causal_block_mask_t():
    kr = lax.broadcasted_iota(jnp.int32, (BLK, BLK), 0)
    qc = lax.broadcasted_iota(jnp.int32, (BLK, BLK), 1)
    return kr <= qc


CHAIN_ROWS = 64


def _col_reduce(x, combine, finish):
    rows = x.shape[0]
    acc = x[0:CHAIN_ROWS]
    for r in range(1, rows // CHAIN_ROWS):
        acc = combine(acc, x[r * CHAIN_ROWS:(r + 1) * CHAIN_ROWS])
    return finish(acc, axis=0, keepdims=True)


def _softmax_pv_t(st, vt):
    m = _col_reduce(st, jnp.maximum, jnp.max)
    p = jnp.exp2(st - m).astype(BF16)
    ones = jnp.ones((BF16_ROWS, vt.shape[1]), BF16)
    out = _dot(jnp.concatenate([vt, ones], axis=0), p)
    return out[0:HEAD_DIM] / out[HEAD_DIM:HEAD_DIM + 1]


def _store_pair(o_ref, gate_ref, p, halves):
    lanes = slice(p * LANES, (p + 1) * LANES)
    y = jnp.concatenate(halves, axis=0).T
    o_ref[:, lanes] = (y * gate_ref[:, lanes].astype(F32)).astype(o_ref.dtype)


def _qblock_call(body, c, name, in_specs, args, gate, acc, batch):
    out_tile = pl.BlockSpec((None, BLK, ATT_W), lambda b: (b, c, 0))
    return pl.pallas_call(
        body,
        grid=(batch,),
        in_specs=in_specs + [out_tile, pl.BlockSpec(memory_space=pl.ANY)],
        out_specs=out_tile,
        out_shape=jax.ShapeDtypeStruct(acc.shape, acc.dtype),
        input_output_aliases={len(args) + 1: 0},
        compiler_params=pltpu.CompilerParams(
            dimension_semantics=("arbitrary",), vmem_limit_bytes=VMEM_LIMIT),
        name=name,
    )(*args, gate, acc)


def _pair_pipeline(n_pairs, issue, consume):
    nxt = issue(0)
    for p in range(n_pairs):
        cur = nxt
        if p + 1 < n_pairs:
            nxt = issue(p + 1)
        consume(p, cur)


def _moba_select(qm, km, n_past):
    km16 = jnp.concatenate([km, jnp.zeros((BF16_ROWS - N_BLK, LANES), F32)], axis=0).astype(BF16)
    gate = _dot(km16, qm)
    blk = lax.broadcasted_iota(jnp.int32, (BF16_ROWS, BLK), 0)
    past = blk < n_past
    gate = jnp.where(past, gate, NEG_INF)
    rank = jnp.zeros((BF16_ROWS, BLK), F32)
    for jp in range(n_past):
        other = gate[jp:jp + 1, :]
        beats = (other > gate) | ((other == gate) & (blk > jp))
        rank = rank + jnp.where(beats, 1.0, 0.0)
    return jnp.where(past & (rank < MOBA_TOPK), 1.0, 0.0)


def _moba_block(c, qT_ref, k_ref, vT_ref, km_ref, gate_ref, acc_ref, o_ref):
    del acc_ref
    nk = (c + 1) * BLK
    causal = _causal_block_mask_t()
    head_masks = _pair_head_masks()
    gated = c > MOBA_TOPK

    def scores(h):
        p, e = divmod(h, HEADS_PER_PAIR)
        feats = slice(p * LANES, (p + 1) * LANES)
        q2 = qT_ref[feats, :]
        qm = jnp.where(head_masks[e], q2, jnp.zeros_like(q2))
        sel = _moba_select(qm, km_ref[:, feats], c) if gated else None
        return _dot(k_ref[:, feats], qm), sel

    def attend(h, st, sel):
        parts = []
        for j in range(c):
            blk = st[j * BLK:(j + 1) * BLK]
            if gated:
                blk = jnp.where(sel[j:j + 1, :] > 0.5, blk, NEG_INF)
            parts.append(blk)
        parts.append(jnp.where(causal, st[c * BLK:nk], NEG_INF))
        st = jnp.concatenate(parts, axis=0) if c else parts[0]
        return _softmax_pv_t(st, vT_ref[h * HEAD_DIM:(h + 1) * HEAD_DIM, :])

    _pair_pipeline(
        N_PAIRS,
        lambda p: [scores(p * HEADS_PER_PAIR + e) for e in range(HEADS_PER_PAIR)],
        lambda p, cur: _store_pair(o_ref, gate_ref, p,
                                   [attend(p * HEADS_PER_PAIR + e, *cur[e])
                                    for e in range(HEADS_PER_PAIR)]))


def _moba(mqT, mk, mvT, kmean, gate, batch):
    mk3 = mk.reshape(batch, SEQ, ATT_W)
    acc = jnp.zeros((batch, SEQ, ATT_W), BF16)
    for c in range(N_BLK):
        nk = (c + 1) * BLK
        in_specs = [pl.BlockSpec((None, ATT_W, BLK), lambda b, c=c: (b, 0, c)),
                    pl.BlockSpec((None, nk, ATT_W), lambda b: (b, 0, 0)),
                    pl.BlockSpec((None, ATT_W, nk), lambda b: (b, 0, 0)),
                    pl.BlockSpec((None, N_BLK, ATT_W), lambda b: (b, 0, 0))]
        acc = _qblock_call(functools.partial(_moba_block, c), c, f"moba_q{c}", in_specs,
                           (mqT, mk3, mvT, kmean), gate.reshape(batch, SEQ, ATT_W), acc, batch)
    return acc.reshape(batch * SEQ, ATT_W)


def _key_to_f32(key):
    bits = key ^ ((key >> 31) & 0x7FFFFFFF)
    return lax.bitcast_convert_type(bits, F32)


def _count_ge(sc, thr):
    acc = jnp.zeros((CHAIN_ROWS, BLK), F32)
    for r in range(sc.shape[0] // CHAIN_ROWS):
        acc = acc + jnp.where(sc[r * CHAIN_ROWS:(r + 1) * CHAIN_ROWS] >= thr, 1.0, 0.0)
    return jnp.sum(acc, axis=0, keepdims=True)


POS_INF = float("inf")


def _dsa_topk_cap(sc, nk):
    kf = float(DSA_TOPK)
    cnt = _count_ge(sc, 0.0)
    t0 = jnp.where(cnt >= kf, 0, INT_MIN).astype(jnp.int32)

    def bit_step(it, t):
        cand = t | jnp.left_shift(jnp.int32(1), 30 - it)
        cnt = _count_ge(sc, _key_to_f32(cand))
        return jnp.where(cnt >= kf, cand, t)

    t = lax.fori_loop(0, 31, bit_step, t0)
    t_val = _key_to_f32(t)
    t_next = _key_to_f32(t + 1)
    need = kf - _count_ge(sc, t_next)
    r = lax.broadcasted_iota(jnp.int32, (BLK, BLK), 0)
    c = lax.broadcasted_iota(jnp.int32, (BLK, BLK), 1)
    strict_lower = jnp.where(c < r, 1.0, 0.0).astype(BF16)
    carry = jnp.zeros((1, BLK), F32)
    parts = []
    for j in range(nk // BLK):
        sc_j = sc[j * BLK:(j + 1) * BLK]
        gt = sc_j >= t_next
        eq = (sc_j >= t_val) & jnp.logical_not(gt)
        eq_f = jnp.where(eq, 1.0, 0.0)
        before = _dot(strict_lower, eq_f.astype(BF16)) + carry
        carry = carry + _col_reduce(eq_f, jnp.add, jnp.sum)
        parts.append(jnp.where(gt | (eq & (before < need)), POS_INF, NEG_INF))
    return jnp.concatenate(parts, axis=0)


def _dsa_block(c, iqT_ref, ik_ref, iwT_ref, qT_ref, k_ref, vT_ref, gate_ref, acc_ref, o_ref):
    del acc_ref
    nk = (c + 1) * BLK
    causal = _causal_block_mask_t()
    head_masks = _pair_head_masks()

    def masked_q_dots(keys, qT2):
        return [_dot(keys, jnp.where(m, qT2, jnp.zeros_like(qT2))) for m in head_masks]

    if c == 0:
        cap = jnp.where(causal, POS_INF, NEG_INF)
    else:
        wscale = IDX_HEADS ** -0.5
        acc = [jnp.zeros((nk, BLK), F32)]

        def accumulate(p, logits):
            for e in range(HEADS_PER_PAIR):
                w = iwT_ref[p * HEADS_PER_PAIR + e:p * HEADS_PER_PAIR + e + 1, :] * wscale
                acc[0] = acc[0] + jnp.maximum(logits[e], 0.0) * w

        _pair_pipeline(
            N_PAIRS,
            lambda p: masked_q_dots(ik_ref[...], iqT_ref[p * LANES:(p + 1) * LANES, :]),
            accumulate)
        own = jnp.where(causal, acc[0][c * BLK:nk], NEG_INF)
        sc = jnp.concatenate([acc[0][0:c * BLK], own], axis=0)
        cap = _dsa_topk_cap(sc, nk)

    def attend(p, scores):
        _store_pair(o_ref, gate_ref, p, [
            _softmax_pv_t(jnp.minimum(scores[e], cap),
                          vT_ref[p * LANES + e * HEAD_DIM:p * LANES + (e + 1) * HEAD_DIM, :])
            for e in range(HEADS_PER_PAIR)])

    _pair_pipeline(
        N_PAIRS,
        lambda p: masked_q_dots(k_ref[:, p * LANES:(p + 1) * LANES],
                                qT_ref[p * LANES:(p + 1) * LANES, :]),
        attend)


def _dsa(iqT, ik, iwT, dqT, dk, dvT, gate, batch):
    ik3 = ik.reshape(batch, SEQ, LANES)
    dk3 = dk.reshape(batch, SEQ, ATT_W)
    acc = jnp.zeros((batch, SEQ, ATT_W), BF16)
    for c in range(N_BLK):
        nk = (c + 1) * BLK
        qtile = lambda r, c=c: pl.BlockSpec((None, r, BLK), lambda b: (b, 0, c))
        keys = lambda w, nk=nk: pl.BlockSpec((None, nk, w), lambda b: (b, 0, 0))
        in_specs = [qtile(ATT_W), keys(LANES), qtile(BF16_ROWS), qtile(ATT_W), keys(ATT_W),
                    pl.BlockSpec((None, ATT_W, nk), lambda b: (b, 0, 0))]
        acc = _qblock_call(functools.partial(_dsa_block, c), c, f"dsa_q{c}", in_specs,
                           (iqT, ik3, iwT, dqT, dk3, dvT), gate.reshape(batch, SEQ, ATT_W), acc,
                           batch)
    return acc.reshape(batch * SEQ, ATT_W)


def _merge_body(x_ref, ya_ref, yb_ref, yc_ref, gates_ref, wa_ref, wb_ref, wc_ref, wo_ref, gf_ref,
                o_ref):
    u = jnp.zeros((PROJ_TILE, D_MODEL), F32)
    for n, (y_ref, w_ref) in enumerate(((ya_ref, wa_ref), (yb_ref, wb_ref), (yc_ref, wc_ref))):
        sl = slice(n * D_MODEL, (n + 1) * D_MODEL)
        u = u + gates_ref[:, sl].astype(F32) * _dot(y_ref[...], w_ref[...])
    y = x_ref[...] + _dot(u.astype(BF16), wo_ref[...])
    o_ref[...] = _rmsnorm(y, gf_ref[...])


def _merge(x2, ya, yb, yc, gates, wa, wb, wc, wo, g_final):
    tile = lambda w: pl.BlockSpec((PROJ_TILE, w), lambda i: (i, 0))
    const = lambda r, w: pl.BlockSpec((r, w), lambda i: (0, 0))
    m = x2.shape[0]
    return pl.pallas_call(
        _merge_body,
        grid=(m // PROJ_TILE,),
        in_specs=[tile(D_MODEL), tile(ATT_W), tile(ATT_W), tile(X_W), tile(N_BRANCH * D_MODEL),
                  const(ATT_W, D_MODEL), const(ATT_W, D_MODEL), const(X_W, D_MODEL),
                  const(D_MODEL, D_MODEL), const(1, D_MODEL)],
        out_specs=tile(D_MODEL),
        out_shape=jax.ShapeDtypeStruct((m, D_MODEL), F32),
        compiler_params=pltpu.CompilerParams(
            dimension_semantics=("arbitrary",), vmem_limit_bytes=VMEM_LIMIT),
        name="merge",
    )(x2, ya, yb, yc, gates, wa, wb, wc, wo, g_final)


def _pack_w_in(w_in):
    offs = [0]
    for s in IN_SIZES:
        offs.append(offs[-1] + s)
    col = lambda n: w_in[:, offs[n]:offs[n + 1]]
    w_cols = jnp.concatenate([col(1), col(5), col(9), col(9), col(3), col(7), col(11), col(12),
                              col(13)], axis=1)
    iw = jnp.pad(col(10), ((0, 0), (0, BF16_ROWS - IDX_HEADS)))
    w_rows = jnp.concatenate([col(0), col(4), col(8), col(2), col(6), iw], axis=1).T
    return w_cols.astype(BF16), w_rows.astype(BF16)


def _rope_tables():
    inv = jnp.power(ROPE_THETA, -jnp.arange(HALF, dtype=F32) * 2.0 / HEAD_DIM)
    ang = jnp.arange(SEQ).astype(F32)[:, None] * inv[None, :]
    cos = jnp.cos(ang)
    sin = jnp.sin(ang)
    cos_lanes = jnp.tile(cos, (1, LANES // HALF))
    sin_lanes = jnp.tile(jnp.concatenate([-sin, sin], axis=1), (1, HEADS_PER_PAIR))
    return cos_lanes, sin_lanes, cos.T, sin.T


def _layer(x2, mem2, g_in, w_in, b_merge, g_mem, w_mem_kv, w_up_moba, w_up_dsa, w_up_cross, w_out,
           g_final, batch):
    w_cols, w_rows = _pack_w_in(w_in)
    xk, xv = _mem_proj(mem2, g_mem[None, :], w_mem_kv.astype(BF16))
    (mqT, dqT, iqT, mvT, dvT, iwT, mk, dk, ik, mgate, dgate, yc, gates, kmean) = _in_proj(
        x2, g_in[None, :], w_cols, w_rows, _rope_tables(), b_merge[None, :],
        xk.reshape(batch, MEM_LEN, X_W), xv.reshape(batch, MEM_LEN, X_W), batch)
    ya = _moba(mqT, mk, mvT, kmean.reshape(batch, N_BLK, ATT_W), mgate, batch)
    yb = _dsa(iqT, ik, iwT, dqT, dk, dvT, dgate, batch)
    return _merge(x2, ya, yb, yc, gates, w_up_moba.astype(BF16), w_up_dsa.astype(BF16),
                  w_up_cross.astype(BF16), w_out.astype(BF16), g_final)


def kernel(x, mem, g_in, w_in, b_merge, g_mem, w_mem_kv, w_up_moba, w_up_dsa, w_up_cross, w_out,
           g_final):
    batch, seq, d = x.shape
    assert seq == SEQ and d == D_MODEL and mem.shape[1] == MEM_LEN
    assert g_in.shape[0] == 1
    out = _layer(x.reshape(batch * seq, d), mem.reshape(batch * MEM_LEN, d), g_in[0], w_in[0],
                 b_merge[0], g_mem[0], w_mem_kv[0], w_up_moba[0], w_up_dsa[0], w_up_cross[0],
                 w_out[0], g_final[None, :], batch)
    return out.reshape(batch, seq, d)
```

```python
import functools

import jax
import jax.numpy as jnp
from jax import lax
from jax.experimental import pallas as pl
from jax.experimental.pallas import tpu as pltpu

D_MODEL = 1024
SEQ = 2048
HEAD_DIM = 64
HALF = HEAD_DIM // 2
N_HEADS = 8
ATT_W = N_HEADS * HEAD_DIM
BLK = 256
N_BLK = SEQ // BLK
PROJ_TILE = 2 * BLK
MOBA_TOPK = 3
DSA_TOPK = 256
IDX_HEADS = 8
MEM_LEN = 256
X_HEADS = 4
X_HEAD_DIM = 128
X_W = X_HEADS * X_HEAD_DIM
N_BRANCH = 3
ROPE_THETA = 10000.0
RMS_EPS = 1e-6
LANES = 128
HEADS_PER_PAIR = LANES // HEAD_DIM
N_PAIRS = ATT_W // LANES
BF16_ROWS = 16

IN_SIZES = (ATT_W, ATT_W, ATT_W, ATT_W, ATT_W, ATT_W, ATT_W, ATT_W,
            IDX_HEADS * HEAD_DIM, HEAD_DIM, IDX_HEADS, X_W, X_W, N_BRANCH * D_MODEL)

COL_MK, COL_DK, COL_IK = 0, 512, 1024
COL_MG, COL_DG, COL_XQ, COL_XG, COL_GL = 1152, 1664, 2176, 2688, 3200
N_COLS = COL_GL + N_BRANCH * D_MODEL
ROW_MQ, ROW_DQ, ROW_IQ, ROW_MV, ROW_DV, ROW_IW = 0, 512, 1024, 1536, 2048, 2560
N_ROWS = ROW_IW + BF16_ROWS

VMEM_LIMIT = 56 * 1024 * 1024

F32 = jnp.float32
BF16 = jnp.bfloat16
NEG_INF = float("-inf")
INT_MIN = -2 ** 31
LOG2E = 1.4426950408889634


def _dot(a, b):
    return jnp.dot(a, b, preferred_element_type=F32)


def _nt_dot(a, b):
    return lax.dot_general(a, b, (((1,), (1,)), ((), ())), preferred_element_type=F32)


def _rmsnorm(x, g):
    ms = jnp.mean(x * x, axis=-1, keepdims=True)
    return (x * lax.rsqrt(ms + RMS_EPS)) * g


def _softmax_pv(s, v):
    m = jnp.max(s, axis=1, keepdims=True)
    p = jnp.exp2(s - m)
    l = jnp.sum(p, axis=1, keepdims=True)
    return _dot(p.astype(BF16), v) / l


def _in_proj_body(x_ref, g_ref, wc_ref, wr_ref, cos_ref, sin_ref, cos_t_ref, sin_t_ref,
                  bm_ref, xk_ref, xv_ref,
                  mqT_ref, dqT_ref, iqT_ref, mvT_ref, dvT_ref, iwT_ref,
                  mk_ref, dk_ref, ik_ref, mg_ref, dg_ref, yc_ref, gl_ref, km_ref):
    h = _rmsnorm(x_ref[...], g_ref[...]).astype(BF16)
    qk_scale = HEAD_DIM ** -0.5
    att_scale = qk_scale * LOG2E

    def seg(off, width):
        return lambda: _dot(h, wc_ref[:, off:off + width])

    kept = {}

    def keep_silu_xg(z):
        kept["xg"] = jax.nn.silu(z)

    def cross_attention(z):
        xq = z.astype(BF16)
        xscale = X_HEAD_DIM ** -0.5 * LOG2E
        yc_parts = []
        for hd in range(X_HEADS):
            sl = slice(hd * X_HEAD_DIM, (hd + 1) * X_HEAD_DIM)
            s = _nt_dot(xq[:, sl], xk_ref[:, sl]) * xscale
            yc_parts.append(_softmax_pv(s, xv_ref[:, sl]))
        yc_ref[...] = (jnp.concatenate(yc_parts, axis=1) * kept["xg"]).astype(BF16)

    def store_merge_gate(c):
        sl = slice(c * D_MODEL, (c + 1) * D_MODEL)

        def epilogue(z):
            gl_ref[:, sl] = jax.nn.sigmoid(z + bm_ref[:, sl]).astype(BF16)
        return epilogue

    def store_silu(ref):
        def epilogue(z):
            ref[...] = jax.nn.silu(z).astype(BF16)
        return epilogue

    def store_feature_major(zt):
        cos_t = cos_t_ref[...]
        sin_t = sin_t_ref[...]
        for ref, row0, scale in ((mqT_ref, ROW_MQ, att_scale), (dqT_ref, ROW_DQ, att_scale),
                                 (iqT_ref, ROW_IQ, qk_scale)):
            for hd in range(N_HEADS):
                r = row0 + hd * HEAD_DIM
                x1 = zt[r:r + HALF]
                x2 = zt[r + HALF:r + HEAD_DIM]
                o = hd * HEAD_DIM
                ref[o:o + HALF, :] = ((x1 * cos_t - x2 * sin_t) * scale).astype(BF16)
                ref[o + HALF:o + HEAD_DIM, :] = ((x2 * cos_t + x1 * sin_t) * scale).astype(BF16)
        mvT_ref[...] = zt[ROW_MV:ROW_MV + ATT_W].astype(BF16)
        dvT_ref[...] = zt[ROW_DV:ROW_DV + ATT_W].astype(BF16)
        iwT_ref[...] = zt[ROW_IW:ROW_IW + BF16_ROWS]

    def store_roped(ref, mean_ref=None):
        def epilogue(z):
            cos = cos_ref[...]
            sin = sin_ref[...]
            lane = lax.broadcasted_iota(jnp.int32, (PROJ_TILE, LANES), 1)
            first_half = (lane & HALF) == 0
            for c in range(z.shape[1] // LANES):
                lanes = slice(c * LANES, (c + 1) * LANES)
                zc = z[:, lanes]
                partner = jnp.where(first_half, pltpu.roll(zc, LANES - HALF, 1),
                                    pltpu.roll(zc, HALF, 1))
                r = zc * cos + partner * sin
                if mean_ref is not None:
                    for blk in range(PROJ_TILE // BLK):
                        mean_ref[blk, :, lanes] = jnp.mean(r[blk * BLK:(blk + 1) * BLK], axis=0,
                                                           keepdims=True)
                ref[:, lanes] = r.astype(ref.dtype)
        return epilogue

    stages = [(seg(COL_XG, X_W), keep_silu_xg), (seg(COL_XQ, X_W), cross_attention)]
    stages += [(seg(COL_GL + c * D_MODEL, D_MODEL), store_merge_gate(c)) for c in range(N_BRANCH)]
    stages += [(seg(COL_MG, ATT_W), store_silu(mg_ref)), (seg(COL_DG, ATT_W), store_silu(dg_ref)),
               (lambda: _nt_dot(wr_ref[...], h), store_feature_major),
               (seg(COL_MK, ATT_W), store_roped(mk_ref, mean_ref=km_ref)),
               (seg(COL_DK, ATT_W), store_roped(dk_ref)),
               (seg(COL_IK, LANES), store_roped(ik_ref))]
    nxt = stages[0][0]()
    for i, (_, epilogue) in enumerate(stages):
        cur = nxt
        if i + 1 < len(stages):
            nxt = stages[i + 1][0]()
        epilogue(cur)


def _in_proj(x2, g_in, w_cols, w_rows, tables, b_merge, xk, xv, batch):
    m = x2.shape[0]
    n_tiles = m // PROJ_TILE
    per_seq = SEQ // PROJ_TILE
    row = lambda w: pl.BlockSpec((PROJ_TILE, w), lambda i: (i, 0))
    feat = lambda r: pl.BlockSpec((None, r, PROJ_TILE), lambda i: (i // per_seq, 0, i % per_seq))
    const = lambda shape: pl.BlockSpec(shape, lambda i: (0, 0), pipeline_mode=pl.Buffered(1))
    tm = lambda w, dt: jax.ShapeDtypeStruct((m, w), dt)
    fm = lambda r, dt: jax.ShapeDtypeStruct((batch, r, SEQ), dt)
    out_shape = [
        fm(ATT_W, BF16), fm(ATT_W, BF16), fm(ATT_W, BF16),
        fm(ATT_W, BF16), fm(ATT_W, BF16), fm(BF16_ROWS, F32),
        tm(ATT_W, BF16), tm(ATT_W, BF16), tm(LANES, BF16),
        tm(ATT_W, BF16), tm(ATT_W, BF16),
        tm(X_W, BF16),
        tm(N_BRANCH * D_MODEL, BF16),
        jax.ShapeDtypeStruct((m // BLK, 1, ATT_W), F32),
    ]
    out_specs = [feat(ATT_W)] * 5 + [feat(BF16_ROWS)] + [
        row(ATT_W), row(ATT_W), row(LANES), row(ATT_W), row(ATT_W), row(X_W),
        row(N_BRANCH * D_MODEL),
        pl.BlockSpec((PROJ_TILE // BLK, 1, ATT_W), lambda i: (i, 0, 0))]
    mem = pl.BlockSpec((None, MEM_LEN, X_W), lambda i: (i // per_seq, 0, 0))
    cos, sin, cos_t, sin_t = tables
    return pl.pallas_call(
        _in_proj_body,
        grid=(n_tiles,),
        in_specs=[
            pl.BlockSpec((PROJ_TILE, D_MODEL), lambda i: (i, 0)),
            pl.BlockSpec((1, D_MODEL), lambda i: (0, 0)),
            const((D_MODEL, N_COLS)),
            const((N_ROWS, D_MODEL)),
            pl.BlockSpec((PROJ_TILE, LANES), lambda i: (i % per_seq, 0)),
            pl.BlockSpec((PROJ_TILE, LANES), lambda i: (i % per_seq, 0)),
            pl.BlockSpec((HALF, PROJ_TILE), lambda i: (0, i % per_seq)),
            pl.BlockSpec((HALF, PROJ_TILE), lambda i: (0, i % per_seq)),
            pl.BlockSpec((1, N_BRANCH * D_MODEL), lambda i: (0, 0)),
            mem, mem,
        ],
        out_specs=out_specs,
        out_shape=out_shape,
        compiler_params=pltpu.CompilerParams(
            dimension_semantics=("arbitrary",), vmem_limit_bytes=VMEM_LIMIT),
        name="in_proj",
    )(x2, g_in, w_cols, w_rows, cos, sin, cos_t, sin_t, b_merge, xk, xv)


def _mem_proj_body(m_ref, g_ref, w_ref, k_ref, v_ref):
    h = _rmsnorm(m_ref[...], g_ref[...]).astype(BF16)
    k_ref[...] = _dot(h, w_ref[:, 0:X_W]).astype(BF16)
    v_ref[...] = _dot(h, w_ref[:, X_W:2 * X_W]).astype(BF16)


def _mem_proj(mem2, g_mem, w_kv):
    m = mem2.shape[0]
    return pl.pallas_call(
        _mem_proj_body,
        grid=(m // BLK,),
        in_specs=[
            pl.BlockSpec((BLK, D_MODEL), lambda i: (i, 0)),
            pl.BlockSpec((1, D_MODEL), lambda i: (0, 0)),
            pl.BlockSpec((D_MODEL, 2 * X_W), lambda i: (0, 0)),
        ],
        out_specs=[pl.BlockSpec((BLK, X_W), lambda i: (i, 0))] * 2,
        out_shape=[jax.ShapeDtypeStruct((m, X_W), BF16)] * 2,
        compiler_params=pltpu.CompilerParams(
            dimension_semantics=("arbitrary",), vmem_limit_bytes=VMEM_LIMIT),
        name="mem_proj",
    )(mem2, g_mem, w_kv)


def _pair_head_masks():
    r = lax.broadcasted_iota(jnp.int32, (LANES, BLK), 0)
    return [(r >= e * HEAD_DIM) & (r < (e + 1) * HEAD_DIM) for e in range(HEADS_PER_PAIR)]


def _causal_block_mask_t():
    kr = lax.broadcasted_iota(jnp.int32, (BLK, BLK), 0)
    qc = lax.broadcasted_iota(jnp.int32, (BLK, BLK), 1)
    return kr <= qc


CHAIN_ROWS = 64


def _col_reduce(x, combine, finish):
    rows = x.shape[0]
    acc = x[0:CHAIN_ROWS]
    for r in range(1, rows // CHAIN_ROWS):
        acc = combine(acc, x[r * CHAIN_ROWS:(r + 1) * CHAIN_ROWS])
    return finish(acc, axis=0, keepdims=True)


SAFE_EXP = 100


def _softmax_pv_t(st, vt, shifted):
    if shifted:
        st = st - _col_reduce(st, jnp.maximum, jnp.max)
    p = jnp.exp2(st).astype(BF16)
    ones = jnp.ones((BF16_ROWS, vt.shape[1]), BF16)
    out = _dot(jnp.concatenate([vt, ones], axis=0), p)
    l = out[HEAD_DIM:HEAD_DIM + 1]
    return out[0:HEAD_DIM] / l, l


def _denominators_safe(ls):
    ok = None
    for l in ls:
        good = (l > 2.0 ** -SAFE_EXP) & (l < 2.0 ** SAFE_EXP)
        ok = good if ok is None else ok & good
    return jnp.min(jnp.where(ok, 1.0, 0.0))


def _attend_guarded(attend_all):
    safe = _denominators_safe(attend_all(False))

    @pl.when(safe < 0.5)
    def _():
        attend_all(True)


def _store_pair(o_ref, gate_ref, p, halves):
    lanes = slice(p * LANES, (p + 1) * LANES)
    y = jnp.concatenate(halves, axis=0).T
    o_ref[:, lanes] = (y * gate_ref[:, lanes].astype(F32)).astype(o_ref.dtype)


def _qblock_call(body, c, name, in_specs, args, gate, acc, batch):
    out_tile = pl.BlockSpec((None, BLK, ATT_W), lambda b: (b, c, 0))
    return pl.pallas_call(
        body,
        grid=(batch,),
        in_specs=in_specs + [out_tile, pl.BlockSpec(memory_space=pl.ANY)],
        out_specs=out_tile,
        out_shape=jax.ShapeDtypeStruct(acc.shape, acc.dtype),
        input_output_aliases={len(args) + 1: 0},
        compiler_params=pltpu.CompilerParams(
            dimension_semantics=("arbitrary",), vmem_limit_bytes=VMEM_LIMIT),
        name=name,
    )(*args, gate, acc)


def _pair_pipeline(n_pairs, issue, consume):
    nxt = issue(0)
    for p in range(n_pairs):
        cur = nxt
        if p + 1 < n_pairs:
            nxt = issue(p + 1)
        consume(p, cur)


def _moba_select(qm, km, n_past):
    km16 = jnp.concatenate([km, jnp.zeros((BF16_ROWS - N_BLK, LANES), F32)], axis=0).astype(BF16)
    gate = _dot(km16, qm)
    blk = lax.broadcasted_iota(jnp.int32, (BF16_ROWS, BLK), 0)
    past = blk < n_past
    gate = jnp.where(past, gate, NEG_INF)
    rank = jnp.zeros((BF16_ROWS, BLK), F32)
    for jp in range(n_past):
        other = gate[jp:jp + 1, :]
        beats = (other > gate) | ((other == gate) & (blk > jp))
        rank = rank + jnp.where(beats, 1.0, 0.0)
    return jnp.where(past & (rank < MOBA_TOPK), 1.0, 0.0)


def _moba_block(c, qT_ref, k_ref, vT_ref, km_ref, gate_ref, acc_ref, o_ref):
    del acc_ref
    nk = (c + 1) * BLK
    causal = _causal_block_mask_t()
    head_masks = _pair_head_masks()
    gated = c > MOBA_TOPK

    def scores(h):
        p, e = divmod(h, HEADS_PER_PAIR)
        feats = slice(p * LANES, (p + 1) * LANES)
        q2 = qT_ref[feats, :]
        qm = jnp.where(head_masks[e], q2, jnp.zeros_like(q2))
        sel = _moba_select(qm, km_ref[:, feats], c) if gated else None
        return _dot(k_ref[:, feats], qm), sel

    def attend(h, st, sel, shifted):
        parts = []
        for j in range(c):
            blk = st[j * BLK:(j + 1) * BLK]
            if gated:
                blk = jnp.where(sel[j:j + 1, :] > 0.5, blk, NEG_INF)
            parts.append(blk)
        parts.append(jnp.where(causal, st[c * BLK:nk], NEG_INF))
        st = jnp.concatenate(parts, axis=0) if c else parts[0]
        return _softmax_pv_t(st, vT_ref[h * HEAD_DIM:(h + 1) * HEAD_DIM, :], shifted)

    def attend_all(shifted):
        denominators = []

        def consume(p, cur):
            outs = [attend(p * HEADS_PER_PAIR + e, *cur[e], shifted) for e in range(HEADS_PER_PAIR)]
            denominators.extend(l for _, l in outs)
            _store_pair(o_ref, gate_ref, p, [o for o, _ in outs])

        _pair_pipeline(
            N_PAIRS,
            lambda p: [scores(p * HEADS_PER_PAIR + e) for e in range(HEADS_PER_PAIR)],
            consume)
        return denominators

    _attend_guarded(attend_all)


def _moba(mqT, mk, mvT, kmean, gate, batch):
    mk3 = mk.reshape(batch, SEQ, ATT_W)
    acc = jnp.zeros((batch, SEQ, ATT_W), BF16)
    for c in range(N_BLK):
        nk = (c + 1) * BLK
        in_specs = [pl.BlockSpec((None, ATT_W, BLK), lambda b, c=c: (b, 0, c)),
                    pl.BlockSpec((None, nk, ATT_W), lambda b: (b, 0, 0)),
                    pl.BlockSpec((None, ATT_W, nk), lambda b: (b, 0, 0)),
                    pl.BlockSpec((None, N_BLK, ATT_W), lambda b: (b, 0, 0))]
        acc = _qblock_call(functools.partial(_moba_block, c), c, f"moba_q{c}", in_specs,
                           (mqT, mk3, mvT, kmean), gate.reshape(batch, SEQ, ATT_W), acc, batch)
    return acc.reshape(batch * SEQ, ATT_W)


def _key_to_f32(key):
    bits = key ^ ((key >> 31) & 0x7FFFFFFF)
    return lax.bitcast_convert_type(bits, F32)


def _count_ge(sc, thr):
    acc = jnp.zeros((CHAIN_ROWS, BLK), F32)
    for r in range(sc.shape[0] // CHAIN_ROWS):
        acc = acc + jnp.where(sc[r * CHAIN_ROWS:(r + 1) * CHAIN_ROWS] >= thr, 1.0, 0.0)
    return jnp.sum(acc, axis=0, keepdims=True)


POS_INF = float("inf")


def _dsa_topk_cap(sc, nk):
    kf = float(DSA_TOPK)
    cnt = _count_ge(sc, 0.0)
    t0 = jnp.where(cnt >= kf, 0, INT_MIN).astype(jnp.int32)

    def bit_step(it, t):
        cand = t | jnp.left_shift(jnp.int32(1), 30 - it)
        cnt = _count_ge(sc, _key_to_f32(cand))
        return jnp.where(cnt >= kf, cand, t)

    t = lax.fori_loop(0, 31, bit_step, t0)
    t_val = _key_to_f32(t)
    t_next = _key_to_f32(t + 1)
    need = kf - _count_ge(sc, t_next)
    r = lax.broadcasted_iota(jnp.int32, (BLK, BLK), 0)
    c = lax.broadcasted_iota(jnp.int32, (BLK, BLK), 1)
    strict_lower = jnp.where(c < r, 1.0, 0.0).astype(BF16)
    carry = jnp.zeros((1, BLK), F32)
    parts = []
    for j in range(nk // BLK):
        sc_j = sc[j * BLK:(j + 1) * BLK]
        gt = sc_j >= t_next
        eq = (sc_j >= t_val) & jnp.logical_not(gt)
        eq_f = jnp.where(eq, 1.0, 0.0)
        before = _dot(strict_lower, eq_f.astype(BF16)) + carry
        carry = carry + _col_reduce(eq_f, jnp.add, jnp.sum)
        parts.append(jnp.where(gt | (eq & (before < need)), POS_INF, NEG_INF))
    return jnp.concatenate(parts, axis=0)


def _dsa_block(c, iqT_ref, ik_ref, iwT_ref, qT_ref, k_ref, vT_ref, gate_ref, acc_ref, o_ref):
    del acc_ref
    nk = (c + 1) * BLK
    causal = _causal_block_mask_t()
    head_masks = _pair_head_masks()

    def masked_q_dots(keys, qT2):
        return [_dot(keys, jnp.where(m, qT2, jnp.zeros_like(qT2))) for m in head_masks]

    if c == 0:
        cap = jnp.where(causal, POS_INF, NEG_INF)
    else:
        wscale = IDX_HEADS ** -0.5
        acc = [jnp.zeros((nk, BLK), F32)]

        def accumulate(p, logits):
            for e in range(HEADS_PER_PAIR):
                w = iwT_ref[p * HEADS_PER_PAIR + e:p * HEADS_PER_PAIR + e + 1, :] * wscale
                acc[0] = acc[0] + jnp.maximum(logits[e], 0.0) * w

        _pair_pipeline(
            N_PAIRS,
            lambda p: masked_q_dots(ik_ref[...], iqT_ref[p * LANES:(p + 1) * LANES, :]),
            accumulate)
        own = jnp.where(causal, acc[0][c * BLK:nk], NEG_INF)
        sc = jnp.concatenate([acc[0][0:c * BLK], own], axis=0)
        cap = _dsa_topk_cap(sc, nk)

    def attend_all(shifted):
        denominators = []

        def consume(p, scores):
            outs = [_softmax_pv_t(
                jnp.minimum(scores[e], cap),
                vT_ref[p * LANES + e * HEAD_DIM:p * LANES + (e + 1) * HEAD_DIM, :], shifted)
                for e in range(HEADS_PER_PAIR)]
            denominators.extend(l for _, l in outs)
            _store_pair(o_ref, gate_ref, p, [o for o, _ in outs])

        _pair_pipeline(
            N_PAIRS,
            lambda p: masked_q_dots(k_ref[:, p * LANES:(p + 1) * LANES],
                                    qT_ref[p * LANES:(p + 1) * LANES, :]),
            consume)
        return denominators

    _attend_guarded(attend_all)


def _dsa(iqT, ik, iwT, dqT, dk, dvT, gate, batch):
    ik3 = ik.reshape(batch, SEQ, LANES)
    dk3 = dk.reshape(batch, SEQ, ATT_W)
    acc = jnp.zeros((batch, SEQ, ATT_W), BF16)
    for c in range(N_BLK):
        nk = (c + 1) * BLK
        qtile = lambda r, c=c: pl.BlockSpec((None, r, BLK), lambda b: (b, 0, c))
        keys = lambda w, nk=nk: pl.BlockSpec((None, nk, w), lambda b: (b, 0, 0))
        in_specs = [qtile(ATT_W), keys(LANES), qtile(BF16_ROWS), qtile(ATT_W), keys(ATT_W),
                    pl.BlockSpec((None, ATT_W, nk), lambda b: (b, 0, 0))]
        acc = _qblock_call(functools.partial(_dsa_block, c), c, f"dsa_q{c}", in_specs,
                           (iqT, ik3, iwT, dqT, dk3, dvT), gate.reshape(batch, SEQ, ATT_W), acc,
                           batch)
    return acc.reshape(batch * SEQ, ATT_W)


def _merge_body(x_ref, ya_ref, yb_ref, yc_ref, gates_ref, wa_ref, wb_ref, wc_ref, wo_ref, gf_ref,
                o_ref):
    u = jnp.zeros((PROJ_TILE, D_MODEL), F32)
    for n, (y_ref, w_ref) in enumerate(((ya_ref, wa_ref), (yb_ref, wb_ref), (yc_ref, wc_ref))):
        sl = slice(n * D_MODEL, (n + 1) * D_MODEL)
        u = u + gates_ref[:, sl].astype(F32) * _dot(y_ref[...], w_ref[...])
    y = x_ref[...] + _dot(u.astype(BF16), wo_ref[...])
    o_ref[...] = _rmsnorm(y, gf_ref[...])


def _merge(x2, ya, yb, yc, gates, wa, wb, wc, wo, g_final):
    tile = lambda w: pl.BlockSpec((PROJ_TILE, w), lambda i: (i, 0))
    const = lambda r, w: pl.BlockSpec((r, w), lambda i: (0, 0))
    m = x2.shape[0]
    return pl.pallas_call(
        _merge_body,
        grid=(m // PROJ_TILE,),
        in_specs=[tile(D_MODEL), tile(ATT_W), tile(ATT_W), tile(X_W), tile(N_BRANCH * D_MODEL),
                  const(ATT_W, D_MODEL), const(ATT_W, D_MODEL), const(X_W, D_MODEL),
                  const(D_MODEL, D_MODEL), const(1, D_MODEL)],
        out_specs=tile(D_MODEL),
        out_shape=jax.ShapeDtypeStruct((m, D_MODEL), F32),
        compiler_params=pltpu.CompilerParams(
            dimension_semantics=("arbitrary",), vmem_limit_bytes=VMEM_LIMIT),
        name="merge",
    )(x2, ya, yb, yc, gates, wa, wb, wc, wo, g_final)


def _pack_w_in(w_in):
    offs = [0]
    for s in IN_SIZES:
        offs.append(offs[-1] + s)
    col = lambda n: w_in[:, offs[n]:offs[n + 1]]
    w_cols = jnp.concatenate([col(1), col(5), col(9), col(9), col(3), col(7), col(11), col(12),
                              col(13)], axis=1)
    iw = jnp.pad(col(10), ((0, 0), (0, BF16_ROWS - IDX_HEADS)))
    w_rows = jnp.concatenate([col(0), col(4), col(8), col(2), col(6), iw], axis=1).T
    return w_cols.astype(BF16), w_rows.astype(BF16)


def _rope_tables():
    inv = jnp.power(ROPE_THETA, -jnp.arange(HALF, dtype=F32) * 2.0 / HEAD_DIM)
    ang = jnp.arange(SEQ).astype(F32)[:, None] * inv[None, :]
    cos = jnp.cos(ang)
    sin = jnp.sin(ang)
    cos_lanes = jnp.tile(cos, (1, LANES // HALF))
    sin_lanes = jnp.tile(jnp.concatenate([-sin, sin], axis=1), (1, HEADS_PER_PAIR))
    return cos_lanes, sin_lanes, cos.T, sin.T


def _layer(x2, mem2, g_in, w_in, b_merge, g_mem, w_mem_kv, w_up_moba, w_up_dsa, w_up_cross, w_out,
           g_final, batch):
    w_cols, w_rows = _pack_w_in(w_in)
    xk, xv = _mem_proj(mem2, g_mem[None, :], w_mem_kv.astype(BF16))
    (mqT, dqT, iqT, mvT, dvT, iwT, mk, dk, ik, mgate, dgate, yc, gates, kmean) = _in_proj(
        x2, g_in[None, :], w_cols, w_rows, _rope_tables(), b_merge[None, :],
        xk.reshape(batch, MEM_LEN, X_W), xv.reshape(batch, MEM_LEN, X_W), batch)
    ya = _moba(mqT, mk, mvT, kmean.reshape(batch, N_BLK, ATT_W), mgate, batch)
    yb = _dsa(iqT, ik, iwT, dqT, dk, dvT, dgate, batch)
    return _merge(x2, ya, yb, yc, gates, w_up_moba.astype(BF16), w_up_dsa.astype(BF16),
                  w_up_cross.astype(BF16), w_out.astype(BF16), g_final)


def kernel(x, mem, g_in, w_in, b_merge, g_mem, w_mem_kv, w_up_moba, w_up_dsa, w_up_cross, w_out,
           g_final):
    batch, seq, d = x.shape
    assert seq == SEQ and d == D_MODEL and mem.shape[1] == MEM_LEN
    assert g_in.shape[0] == 1
    out = _layer(x.reshape(batch * seq, d), mem.reshape(batch * MEM_LEN, d), g_in[0], w_in[0],
                 b_merge[0], g_mem[0], w_mem_kv[0], w_up_moba[0], w_up_dsa[0], w_up_cross[0],
                 w_out[0], g_final[None, :], batch)
    return out.reshape(batch, seq, d)
```

```python
import functools

import jax
import jax.numpy as jnp
from jax import lax
from jax.experimental import pallas as pl
from jax.experimental.pallas import tpu as pltpu

D_MODEL = 1024
SEQ = 2048
HEAD_DIM = 64
HALF = HEAD_DIM // 2
N_HEADS = 8
ATT_W = N_HEADS * HEAD_DIM
BLK = 256
N_BLK = SEQ // BLK
PROJ_TILE = 2 * BLK
MOBA_TOPK = 3
DSA_TOPK = 256
IDX_HEADS = 8
MEM_LEN = 256
X_HEADS = 4
X_HEAD_DIM = 128
X_W = X_HEADS * X_HEAD_DIM
N_BRANCH = 3
ROPE_THETA = 10000.0
RMS_EPS = 1e-6
LANES = 128
HEADS_PER_PAIR = LANES // HEAD_DIM
N_PAIRS = ATT_W // LANES
BF16_ROWS = 16

IN_SIZES = (ATT_W, ATT_W, ATT_W, ATT_W, ATT_W, ATT_W, ATT_W, ATT_W,
            IDX_HEADS * HEAD_DIM, HEAD_DIM, IDX_HEADS, X_W, X_W, N_BRANCH * D_MODEL)

COL_MK, COL_DK, COL_IK = 0, 512, 1024
COL_MG, COL_DG, COL_XQ, COL_XG, COL_GL = 1152, 1664, 2176, 2688, 3200
N_COLS = COL_GL + N_BRANCH * D_MODEL
ROW_MQ, ROW_DQ, ROW_IQ, ROW_MV, ROW_DV, ROW_IW = 0, 512, 1024, 1536, 2048, 2560
N_ROWS = ROW_IW + BF16_ROWS

VMEM_LIMIT = 56 * 1024 * 1024

F32 = jnp.float32
BF16 = jnp.bfloat16
NEG_INF = float("-inf")
INT_MIN = -2 ** 31
LOG2E = 1.4426950408889634


def _dot(a, b):
    return jnp.dot(a, b, preferred_element_type=F32)


def _nt_dot(a, b):
    return lax.dot_general(a, b, (((1,), (1,)), ((), ())), preferred_element_type=F32)


def _rmsnorm(x, g):
    ms = jnp.mean(x * x, axis=-1, keepdims=True)
    return (x * lax.rsqrt(ms + RMS_EPS)) * g


def _softmax_pv(s, v):
    m = jnp.max(s, axis=1, keepdims=True)
    p = jnp.exp2(s - m)
    l = jnp.sum(p, axis=1, keepdims=True)
    return _dot(p.astype(BF16), v) / l


def _in_proj_body(x_ref, g_ref, wc_ref, wr_ref, cos_ref, sin_ref, cos_t_ref, sin_t_ref,
                  bm_ref, xk_ref, xv_ref,
                  mqT_ref, dqT_ref, iqT_ref, mvT_ref, dvT_ref, iwT_ref,
                  mk_ref, dk_ref, ik_ref, mg_ref, dg_ref, yc_ref, gl_ref, km_ref):
    h = _rmsnorm(x_ref[...], g_ref[...]).astype(BF16)
    qk_scale = HEAD_DIM ** -0.5
    att_scale = qk_scale * LOG2E

    def seg(off, width):
        return lambda: _dot(h, wc_ref[:, off:off + width])

    kept = {}

    def keep_silu_xg(z):
        kept["xg"] = jax.nn.silu(z)

    def cross_attention(z):
        xq = z.astype(BF16)
        xscale = X_HEAD_DIM ** -0.5 * LOG2E
        yc_parts = []
        for hd in range(X_HEADS):
            sl = slice(hd * X_HEAD_DIM, (hd + 1) * X_HEAD_DIM)
            s = _nt_dot(xq[:, sl], xk_ref[:, sl]) * xscale
            yc_parts.append(_softmax_pv(s, xv_ref[:, sl]))
        yc_ref[...] = (jnp.concatenate(yc_parts, axis=1) * kept["xg"]).astype(BF16)

    def store_merge_gate(c):
        sl = slice(c * D_MODEL, (c + 1) * D_MODEL)

        def epilogue(z):
            gl_ref[:, sl] = jax.nn.sigmoid(z + bm_ref[:, sl]).astype(BF16)
        return epilogue

    def store_silu(ref):
        def epilogue(z):
            ref[...] = jax.nn.silu(z).astype(BF16)
        return epilogue

    def store_feature_major(zt):
        cos_t = cos_t_ref[...]
        sin_t = sin_t_ref[...]
        for ref, row0, scale in ((mqT_ref, ROW_MQ, att_scale), (dqT_ref, ROW_DQ, att_scale),
                                 (iqT_ref, ROW_IQ, qk_scale)):
            for hd in range(N_HEADS):
                r = row0 + hd * HEAD_DIM
                x1 = zt[r:r + HALF]
                x2 = zt[r + HALF:r + HEAD_DIM]
                o = hd * HEAD_DIM
                ref[o:o + HALF, :] = ((x1 * cos_t - x2 * sin_t) * scale).astype(BF16)
                ref[o + HALF:o + HEAD_DIM, :] = ((x2 * cos_t + x1 * sin_t) * scale).astype(BF16)
        mvT_ref[...] = zt[ROW_MV:ROW_MV + ATT_W].astype(BF16)
        dvT_ref[...] = zt[ROW_DV:ROW_DV + ATT_W].astype(BF16)
        iwT_ref[...] = zt[ROW_IW:ROW_IW + BF16_ROWS]

    def store_roped(ref, mean_ref=None):
        def epilogue(z):
            cos = cos_ref[...]
            sin = sin_ref[...]
            lane = lax.broadcasted_iota(jnp.int32, (PROJ_TILE, LANES), 1)
            first_half = (lane & HALF) == 0
            for c in range(z.shape[1] // LANES):
                lanes = slice(c * LANES, (c + 1) * LANES)
                zc = z[:, lanes]
                partner = jnp.where(first_half, pltpu.roll(zc, LANES - HALF, 1),
                                    pltpu.roll(zc, HALF, 1))
                r = zc * cos + partner * sin
                if mean_ref is not None:
                    for blk in range(PROJ_TILE // BLK):
                        mean_ref[blk, :, lanes] = jnp.mean(r[blk * BLK:(blk + 1) * BLK], axis=0,
                                                           keepdims=True)
                ref[:, lanes] = r.astype(ref.dtype)
        return epilogue

    stages = [(seg(COL_XG, X_W), keep_silu_xg), (seg(COL_XQ, X_W), cross_attention)]
    stages += [(seg(COL_GL + c * D_MODEL, D_MODEL), store_merge_gate(c)) for c in range(N_BRANCH)]
    stages += [(seg(COL_MG, ATT_W), store_silu(mg_ref)), (seg(COL_DG, ATT_W), store_silu(dg_ref)),
               (lambda: _nt_dot(wr_ref[...], h), store_feature_major),
               (seg(COL_MK, ATT_W), store_roped(mk_ref, mean_ref=km_ref)),
               (seg(COL_DK, ATT_W), store_roped(dk_ref)),
               (seg(COL_IK, LANES), store_roped(ik_ref))]
    nxt = stages[0][0]()
    for i, (_, epilogue) in enumerate(stages):
        cur = nxt
        if i + 1 < len(stages):
            nxt = stages[i + 1][0]()
        epilogue(cur)


def _in_proj(x2, g_in, w_cols, w_rows, tables, b_merge, xk, xv, batch):
    m = x2.shape[0]
    n_tiles = m // PROJ_TILE
    per_seq = SEQ // PROJ_TILE
    row = lambda w: pl.BlockSpec((PROJ_TILE, w), lambda i: (i, 0))
    feat = lambda r: pl.BlockSpec((None, r, PROJ_TILE), lambda i: (i // per_seq, 0, i % per_seq))
    const = lambda shape: pl.BlockSpec(shape, lambda i: (0, 0), pipeline_mode=pl.Buffered(1))
    tm = lambda w, dt: jax.ShapeDtypeStruct((m, w), dt)
    fm = lambda r, dt: jax.ShapeDtypeStruct((batch, r, SEQ), dt)
    out_shape = [
        fm(ATT_W, BF16), fm(ATT_W, BF16), fm(ATT_W, BF16),
        fm(ATT_W, BF16), fm(ATT_W, BF16), fm(BF16_ROWS, F32),
        tm(ATT_W, BF16), tm(ATT_W, BF16), tm(LANES, BF16),
        tm(ATT_W, BF16), tm(ATT_W, BF16),
        tm(X_W, BF16),
        tm(N_BRANCH * D_MODEL, BF16),
        jax.ShapeDtypeStruct((m // BLK, 1, ATT_W), F32),
    ]
    out_specs = [feat(ATT_W)] * 5 + [feat(BF16_ROWS)] + [
        row(ATT_W), row(ATT_W), row(LANES), row(ATT_W), row(ATT_W), row(X_W),
        row(N_BRANCH * D_MODEL),
        pl.BlockSpec((PROJ_TILE // BLK, 1, ATT_W), lambda i: (i, 0, 0))]
    mem = pl.BlockSpec((None, MEM_LEN, X_W), lambda i: (i // per_seq, 0, 0))
    cos, sin, cos_t, sin_t = tables
    return pl.pallas_call(
        _in_proj_body,
        grid=(n_tiles,),
        in_specs=[
            pl.BlockSpec((PROJ_TILE, D_MODEL), lambda i: (i, 0)),
            pl.BlockSpec((1, D_MODEL), lambda i: (0, 0)),
            const((D_MODEL, N_COLS)),
            const((N_ROWS, D_MODEL)),
            pl.BlockSpec((PROJ_TILE, LANES), lambda i: (i % per_seq, 0)),
            pl.BlockSpec((PROJ_TILE, LANES), lambda i: (i % per_seq, 0)),
            pl.BlockSpec((HALF, PROJ_TILE), lambda i: (0, i % per_seq)),
            pl.BlockSpec((HALF, PROJ_TILE), lambda i: (0, i % per_seq)),
            pl.BlockSpec((1, N_BRANCH * D_MODEL), lambda i: (0, 0)),
            mem, mem,
        ],
        out_specs=out_specs,
        out_shape=out_shape,
        compiler_params=pltpu.CompilerParams(
            dimension_semantics=("arbitrary",), vmem_limit_bytes=VMEM_LIMIT),
        name="in_proj",
    )(x2, g_in, w_cols, w_rows, cos, sin, cos_t, sin_t, b_merge, xk, xv)


def _mem_proj_body(m_ref, g_ref, w_ref, k_ref, v_ref):
    h = _rmsnorm(m_ref[...], g_ref[...]).astype(BF16)
    k_ref[...] = _dot(h, w_ref[:, 0:X_W]).astype(BF16)
    v_ref[...] = _dot(h, w_ref[:, X_W:2 * X_W]).astype(BF16)


def _mem_proj(mem2, g_mem, w_kv):
    m = mem2.shape[0]
    return pl.pallas_call(
        _mem_proj_body,
        grid=(m // BLK,),
        in_specs=[
            pl.BlockSpec((BLK, D_MODEL), lambda i: (i, 0)),
            pl.BlockSpec((1, D_MODEL), lambda i: (0, 0)),
            pl.BlockSpec((D_MODEL, 2 * X_W), lambda i: (0, 0)),
        ],
        out_specs=[pl.BlockSpec((BLK, X_W), lambda i: (i, 0))] * 2,
        out_shape=[jax.ShapeDtypeStruct((m, X_W), BF16)] * 2,
        compiler_params=pltpu.CompilerParams(
            dimension_semantics=("arbitrary",), vmem_limit_bytes=VMEM_LIMIT),
        name="mem_proj",
    )(mem2, g_mem, w_kv)


def _pair_head_masks():
    r = lax.broadcasted_iota(jnp.int32, (LANES, BLK), 0)
    return [(r >= e * HEAD_DIM) & (r < (e + 1) * HEAD_DIM) for e in range(HEADS_PER_PAIR)]


def _causal_block_mask_t():
    kr = lax.broadcasted_iota(jnp.int32, (BLK, BLK), 0)
    qc = lax.broadcasted_iota(jnp.int32, (BLK, BLK), 1)
    return kr <= qc


CHAIN_ROWS = 64


def _col_reduce(x, combine, finish):
    rows = x.shape[0]
    acc = x[0:CHAIN_ROWS]
    for r in range(1, rows // CHAIN_ROWS):
        acc = combine(acc, x[r * CHAIN_ROWS:(r + 1) * CHAIN_ROWS])
    return finish(acc, axis=0, keepdims=True)


SAFE_EXP = 100


def _softmax_pv_t(st, vt, shifted):
    if shifted:
        st = st - _col_reduce(st, jnp.maximum, jnp.max)
    p = jnp.exp2(st).astype(BF16)
    ones = jnp.ones((BF16_ROWS, vt.shape[1]), BF16)
    out = _dot(jnp.concatenate([vt, ones], axis=0), p)
    l = out[HEAD_DIM:HEAD_DIM + 1]
    return out[0:HEAD_DIM] / l, l


def _denominators_safe(ls):
    ok = None
    for l in ls:
        good = (l > 2.0 ** -SAFE_EXP) & (l < 2.0 ** SAFE_EXP)
        ok = good if ok is None else ok & good
    return jnp.min(jnp.where(ok, 1.0, 0.0))


def _attend_guarded(attend_all):
    safe = _denominators_safe(attend_all(False))

    @pl.when(safe < 0.5)
    def _():
        attend_all(True)


def _store_pair(o_ref, gate_ref, p, halves):
    lanes = slice(p * LANES, (p + 1) * LANES)
    y = jnp.concatenate(halves, axis=0).T
    o_ref[:, lanes] = (y * gate_ref[:, lanes].astype(F32)).astype(o_ref.dtype)


def _qblock_call(body, c, name, in_specs, args, gate, acc, batch, scratch_shapes=()):
    out_tile = pl.BlockSpec((None, BLK, ATT_W), lambda b: (b, c, 0))
    return pl.pallas_call(
        body,
        grid=(batch,),
        in_specs=in_specs + [out_tile, pl.BlockSpec(memory_space=pl.ANY)],
        out_specs=out_tile,
        out_shape=jax.ShapeDtypeStruct(acc.shape, acc.dtype),
        input_output_aliases={len(args) + 1: 0},
        scratch_shapes=list(scratch_shapes),
        compiler_params=pltpu.CompilerParams(
            dimension_semantics=("arbitrary",), vmem_limit_bytes=VMEM_LIMIT),
        name=name,
    )(*args, gate, acc)


def _pair_pipeline(n_pairs, issue, consume):
    nxt = issue(0)
    for p in range(n_pairs):
        cur = nxt
        if p + 1 < n_pairs:
            nxt = issue(p + 1)
        consume(p, cur)


def _moba_select(qm, km, n_past):
    km16 = jnp.concatenate([km, jnp.zeros((BF16_ROWS - N_BLK, LANES), F32)], axis=0).astype(BF16)
    gate = _dot(km16, qm)
    blk = lax.broadcasted_iota(jnp.int32, (BF16_ROWS, BLK), 0)
    past = blk < n_past
    gate = jnp.where(past, gate, NEG_INF)
    rank = jnp.zeros((BF16_ROWS, BLK), F32)
    for jp in range(n_past):
        other = gate[jp:jp + 1, :]
        beats = (other > gate) | ((other == gate) & (blk > jp))
        rank = rank + jnp.where(beats, 1.0, 0.0)
    return jnp.where(past & (rank < MOBA_TOPK), 1.0, 0.0)


def _moba_block(c, qT_ref, k_ref, vT_ref, km_ref, gate_ref, acc_ref, o_ref):
    del acc_ref
    nk = (c + 1) * BLK
    causal = _causal_block_mask_t()
    head_masks = _pair_head_masks()
    gated = c > MOBA_TOPK

    def scores(h):
        p, e = divmod(h, HEADS_PER_PAIR)
        feats = slice(p * LANES, (p + 1) * LANES)
        q2 = qT_ref[feats, :]
        qm = jnp.where(head_masks[e], q2, jnp.zeros_like(q2))
        sel = _moba_select(qm, km_ref[:, feats], c) if gated else None
        return _dot(k_ref[:, feats], qm), sel

    def attend(h, st, sel, shifted):
        parts = []
        for j in range(c):
            blk = st[j * BLK:(j + 1) * BLK]
            if gated:
                blk = jnp.where(sel[j:j + 1, :] > 0.5, blk, NEG_INF)
            parts.append(blk)
        parts.append(jnp.where(causal, st[c * BLK:nk], NEG_INF))
        st = jnp.concatenate(parts, axis=0) if c else parts[0]
        return _softmax_pv_t(st, vT_ref[h * HEAD_DIM:(h + 1) * HEAD_DIM, :], shifted)

    def attend_all(shifted):
        denominators = []

        def consume(p, cur):
            outs = [attend(p * HEADS_PER_PAIR + e, *cur[e], shifted) for e in range(HEADS_PER_PAIR)]
            denominators.extend(l for _, l in outs)
            _store_pair(o_ref, gate_ref, p, [o for o, _ in outs])

        _pair_pipeline(
            N_PAIRS,
            lambda p: [scores(p * HEADS_PER_PAIR + e) for e in range(HEADS_PER_PAIR)],
            consume)
        return denominators

    _attend_guarded(attend_all)


def _moba(mqT, mk, mvT, kmean, gate, batch):
    mk3 = mk.reshape(batch, SEQ, ATT_W)
    acc = jnp.zeros((batch, SEQ, ATT_W), BF16)
    for c in range(N_BLK):
        nk = (c + 1) * BLK
        in_specs = [pl.BlockSpec((None, ATT_W, BLK), lambda b, c=c: (b, 0, c)),
                    pl.BlockSpec((None, nk, ATT_W), lambda b: (b, 0, 0)),
                    pl.BlockSpec((None, ATT_W, nk), lambda b: (b, 0, 0)),
                    pl.BlockSpec((None, N_BLK, ATT_W), lambda b: (b, 0, 0))]
        acc = _qblock_call(functools.partial(_moba_block, c), c, f"moba_q{c}", in_specs,
                           (mqT, mk3, mvT, kmean), gate.reshape(batch, SEQ, ATT_W), acc, batch)
    return acc.reshape(batch * SEQ, ATT_W)


def _key_to_f32(key):
    bits = key ^ ((key >> 31) & 0x7FFFFFFF)
    return lax.bitcast_convert_type(bits, F32)


def _count_ge(sc, thr):
    acc = jnp.zeros((CHAIN_ROWS, BLK), F32)
    for r in range(sc.shape[0] // CHAIN_ROWS):
        acc = acc + jnp.where(sc[r * CHAIN_ROWS:(r + 1) * CHAIN_ROWS] >= thr, 1.0, 0.0)
    return jnp.sum(acc, axis=0, keepdims=True)


POS_INF = float("inf")


def _dsa_topk_cap(sc, nk, cap_ref):
    kf = float(DSA_TOPK)
    cnt0 = _count_ge(sc, 0.0)
    nonneg = cnt0 >= kf
    t0 = jnp.where(nonneg, 0, INT_MIN).astype(jnp.int32)
    n0 = jnp.where(nonneg, cnt0, float(nk))

    def bit_step(it, state):
        t, n_ge = state
        cand = t | jnp.left_shift(jnp.int32(1), 30 - it)
        cnt = _count_ge(sc, _key_to_f32(cand))
        take = cnt >= kf
        return jnp.where(take, cand, t), jnp.where(take, cnt, n_ge)

    t, n_ge = lax.fori_loop(0, 31, bit_step, (t0, n0))
    t_val = _key_to_f32(t)
    ties_beyond_k = jnp.max(n_ge) > kf

    @pl.when(jnp.logical_not(ties_beyond_k))
    def _():
        cap_ref[...] = jnp.where(sc >= t_val, POS_INF, NEG_INF)

    @pl.when(ties_beyond_k)
    def _():
        t_next = _key_to_f32(t + 1)
        need = kf - _count_ge(sc, t_next)
        r = lax.broadcasted_iota(jnp.int32, (BLK, BLK), 0)
        c = lax.broadcasted_iota(jnp.int32, (BLK, BLK), 1)
        strict_lower = jnp.where(c < r, 1.0, 0.0).astype(BF16)
        carry = jnp.zeros((1, BLK), F32)
        for j in range(nk // BLK):
            rows = slice(j * BLK, (j + 1) * BLK)
            sc_j = sc[rows]
            gt = sc_j >= t_next
            eq = (sc_j >= t_val) & jnp.logical_not(gt)
            eq_f = jnp.where(eq, 1.0, 0.0)
            before = _dot(strict_lower, eq_f.astype(BF16)) + carry
            carry = carry + _col_reduce(eq_f, jnp.add, jnp.sum)
            cap_ref[rows, :] = jnp.where(gt | (eq & (before < need)), POS_INF, NEG_INF)

    return cap_ref[...]


def _dsa_block(c, iqT_ref, ik_ref, iwT_ref, qT_ref, k_ref, vT_ref, gate_ref, acc_ref, o_ref,
               cap_ref):
    del acc_ref
    nk = (c + 1) * BLK
    causal = _causal_block_mask_t()
    head_masks = _pair_head_masks()

    def masked_q_dots(keys, qT2):
        return [_dot(keys, jnp.where(m, qT2, jnp.zeros_like(qT2))) for m in head_masks]

    if c == 0:
        cap = jnp.where(causal, POS_INF, NEG_INF)
    else:
        wscale = IDX_HEADS ** -0.5
        acc = [jnp.zeros((nk, BLK), F32)]

        def accumulate(p, logits):
            for e in range(HEADS_PER_PAIR):
                w = iwT_ref[p * HEADS_PER_PAIR + e:p * HEADS_PER_PAIR + e + 1, :] * wscale
                acc[0] = acc[0] + jnp.maximum(logits[e], 0.0) * w

        _pair_pipeline(
            N_PAIRS,
            lambda p: masked_q_dots(ik_ref[...], iqT_ref[p * LANES:(p + 1) * LANES, :]),
            accumulate)
        own = jnp.where(causal, acc[0][c * BLK:nk], NEG_INF)
        sc = jnp.concatenate([acc[0][0:c * BLK], own], axis=0)
        cap = _dsa_topk_cap(sc, nk, cap_ref)

    def attend_all(shifted):
        denominators = []

        def consume(p, scores):
            outs = [_softmax_pv_t(
                jnp.minimum(scores[e], cap),
                vT_ref[p * LANES + e * HEAD_DIM:p * LANES + (e + 1) * HEAD_DIM, :], shifted)
                for e in range(HEADS_PER_PAIR)]
            denominators.extend(l for _, l in outs)
            _store_pair(o_ref, gate_ref, p, [o for o, _ in outs])

        _pair_pipeline(
            N_PAIRS,
            lambda p: masked_q_dots(k_ref[:, p * LANES:(p + 1) * LANES],
                                    qT_ref[p * LANES:(p + 1) * LANES, :]),
            consume)
        return denominators

    _attend_guarded(attend_all)


def _dsa(iqT, ik, iwT, dqT, dk, dvT, gate, batch):
    ik3 = ik.reshape(batch, SEQ, LANES)
    dk3 = dk.reshape(batch, SEQ, ATT_W)
    acc = jnp.zeros((batch, SEQ, ATT_W), BF16)
    for c in range(N_BLK):
        nk = (c + 1) * BLK
        qtile = lambda r, c=c: pl.BlockSpec((None, r, BLK), lambda b: (b, 0, c))
        keys = lambda w, nk=nk: pl.BlockSpec((None, nk, w), lambda b: (b, 0, 0))
        in_specs = [qtile(ATT_W), keys(LANES), qtile(BF16_ROWS), qtile(ATT_W), keys(ATT_W),
                    pl.BlockSpec((None, ATT_W, nk), lambda b: (b, 0, 0))]
        acc = _qblock_call(functools.partial(_dsa_block, c), c, f"dsa_q{c}", in_specs,
                           (iqT, ik3, iwT, dqT, dk3, dvT), gate.reshape(batch, SEQ, ATT_W), acc,
                           batch, scratch_shapes=[pltpu.VMEM((nk, BLK), F32)])
    return acc.reshape(batch * SEQ, ATT_W)


def _merge_body(x_ref, ya_ref, yb_ref, yc_ref, gates_ref, wa_ref, wb_ref, wc_ref, wo_ref, gf_ref,
                o_ref):
    u = jnp.zeros((PROJ_TILE, D_MODEL), F32)
    for n, (y_ref, w_ref) in enumerate(((ya_ref, wa_ref), (yb_ref, wb_ref), (yc_ref, wc_ref))):
        sl = slice(n * D_MODEL, (n + 1) * D_MODEL)
        u = u + gates_ref[:, sl].astype(F32) * _dot(y_ref[...], w_ref[...])
    y = x_ref[...] + _dot(u.astype(BF16), wo_ref[...])
    o_ref[...] = _rmsnorm(y, gf_ref[...])


def _merge(x2, ya, yb, yc, gates, wa, wb, wc, wo, g_final):
    tile = lambda w: pl.BlockSpec((PROJ_TILE, w), lambda i: (i, 0))
    const = lambda r, w: pl.BlockSpec((r, w), lambda i: (0, 0))
    m = x2.shape[0]
    return pl.pallas_call(
        _merge_body,
        grid=(m // PROJ_TILE,),
        in_specs=[tile(D_MODEL), tile(ATT_W), tile(ATT_W), tile(X_W), tile(N_BRANCH * D_MODEL),
                  const(ATT_W, D_MODEL), const(ATT_W, D_MODEL), const(X_W, D_MODEL),
                  const(D_MODEL, D_MODEL), const(1, D_MODEL)],
        out_specs=tile(D_MODEL),
        out_shape=jax.ShapeDtypeStruct((m, D_MODEL), F32),
        compiler_params=pltpu.CompilerParams(
            dimension_semantics=("arbitrary",), vmem_limit_bytes=VMEM_LIMIT),
        name="merge",
    )(x2, ya, yb, yc, gates, wa, wb, wc, wo, g_final)


def _pack_w_in(w_in):
    offs = [0]
    for s in IN_SIZES:
        offs.append(offs[-1] + s)
    col = lambda n: w_in[:, offs[n]:offs[n + 1]]
    w_cols = jnp.concatenate([col(1), col(5), col(9), col(9), col(3), col(7), col(11), col(12),
                              col(13)], axis=1)
    iw = jnp.pad(col(10), ((0, 0), (0, BF16_ROWS - IDX_HEADS)))
    w_rows = jnp.concatenate([col(0), col(4), col(8), col(2), col(6), iw], axis=1).T
    return w_cols.astype(BF16), w_rows.astype(BF16)


def _rope_tables():
    inv = jnp.power(ROPE_THETA, -jnp.arange(HALF, dtype=F32) * 2.0 / HEAD_DIM)
    ang = jnp.arange(SEQ).astype(F32)[:, None] * inv[None, :]
    cos = jnp.cos(ang)
    sin = jnp.sin(ang)
    cos_lanes = jnp.tile(cos, (1, LANES // HALF))
    sin_lanes = jnp.tile(jnp.concatenate([-sin, sin], axis=1), (1, HEADS_PER_PAIR))
    return cos_lanes, sin_lanes, cos.T, sin.T


def _layer(x2, mem2, g_in, w_in, b_merge, g_mem, w_mem_kv, w_up_moba, w_up_dsa, w_up_cross, w_out,
           g_final, batch):
    w_cols, w_rows = _pack_w_in(w_in)
    xk, xv = _mem_proj(mem2, g_mem[None, :], w_mem_kv.astype(BF16))
    (mqT, dqT, iqT, mvT, dvT, iwT, mk, dk, ik, mgate, dgate, yc, gates, kmean) = _in_proj(
        x2, g_in[None, :], w_cols, w_rows, _rope_tables(), b_merge[None, :],
        xk.reshape(batch, MEM_LEN, X_W), xv.reshape(batch, MEM_LEN, X_W), batch)
    ya = _moba(mqT, mk, mvT, kmean.reshape(batch, N_BLK, ATT_W), mgate, batch)
    yb = _dsa(iqT, ik, iwT, dqT, dk, dvT, dgate, batch)
    return _merge(x2, ya, yb, yc, gates, w_up_moba.astype(BF16), w_up_dsa.astype(BF16),
                  w_up_cross.astype(BF16), w_out.astype(BF16), g_final)


def kernel(x, mem, g_in, w_in, b_merge, g_mem, w_mem_kv, w_up_moba, w_up_dsa, w_up_cross, w_out,
           g_final):
    batch, seq, d = x.shape
    assert seq == SEQ and d == D_MODEL and mem.shape[1] == MEM_LEN
    assert g_in.shape[0] == 1
    out = _layer(x.reshape(batch * seq, d), mem.reshape(batch * MEM_LEN, d), g_in[0], w_in[0],
                 b_merge[0], g_mem[0], w_mem_kv[0], w_up_moba[0], w_up_dsa[0], w_up_cross[0],
                 w_out[0], g_final[None, :], batch)
    return out.reshape(batch, seq, d)
```

```python
import functools

import jax
import jax.numpy as jnp
from jax import lax
from jax.experimental import pallas as pl
from jax.experimental.pallas import tpu as pltpu

D_MODEL = 1024
SEQ = 2048
HEAD_DIM = 64
HALF = HEAD_DIM // 2
N_HEADS = 8
ATT_W = N_HEADS * HEAD_DIM
BLK = 256
N_BLK = SEQ // BLK
SEQS = 2
PROJ_TILE = 2 * BLK
MOBA_TOPK = 3
DSA_TOPK = 256
IDX_HEADS = 8
MEM_LEN = 256
X_HEADS = 4
X_HEAD_DIM = 128
X_W = X_HEADS * X_HEAD_DIM
N_BRANCH = 3
ROPE_THETA = 10000.0
RMS_EPS = 1e-6
LANES = 128
HEADS_PER_PAIR = LANES // HEAD_DIM
N_PAIRS = ATT_W // LANES
BF16_ROWS = 16

IN_SIZES = (ATT_W, ATT_W, ATT_W, ATT_W, ATT_W, ATT_W, ATT_W, ATT_W,
            IDX_HEADS * HEAD_DIM, HEAD_DIM, IDX_HEADS, X_W, X_W, N_BRANCH * D_MODEL)

COL_MK, COL_DK, COL_IK = 0, 512, 1024
COL_MG, COL_DG, COL_XQ, COL_XG, COL_GL = 1152, 1664, 2176, 2688, 3200
N_COLS = COL_GL + N_BRANCH * D_MODEL
ROW_MQ, ROW_DQ, ROW_IQ, ROW_MV, ROW_DV, ROW_IW = 0, 512, 1024, 1536, 2048, 2560
N_ROWS = ROW_IW + BF16_ROWS

VMEM_LIMIT = 56 * 1024 * 1024

F32 = jnp.float32
BF16 = jnp.bfloat16
NEG_INF = float("-inf")
INT_MIN = -2 ** 31
LOG2E = 1.4426950408889634


def _dot(a, b):
    return jnp.dot(a, b, preferred_element_type=F32)


def _nt_dot(a, b):
    return lax.dot_general(a, b, (((1,), (1,)), ((), ())), preferred_element_type=F32)


def _rmsnorm(x, g):
    ms = jnp.mean(x * x, axis=-1, keepdims=True)
    return (x * lax.rsqrt(ms + RMS_EPS)) * g


def _softmax_pv(s, v):
    m = jnp.max(s, axis=1, keepdims=True)
    p = jnp.exp2(s - m)
    l = jnp.sum(p, axis=1, keepdims=True)
    return _dot(p.astype(BF16), v) / l


def _in_proj_body(x_ref, g_ref, wc_ref, wr_ref, cos_ref, sin_ref, cos_t_ref, sin_t_ref,
                  bm_ref, xk_ref, xv_ref,
                  mqT_ref, dqT_ref, iqT_ref, mvT_ref, dvT_ref, iwT_ref,
                  mk_ref, dk_ref, ik_ref, mg_ref, dg_ref, yc_ref, gl_ref, km_ref):
    h = _rmsnorm(x_ref[...], g_ref[...]).astype(BF16)
    qk_scale = HEAD_DIM ** -0.5
    att_scale = qk_scale * LOG2E

    def seg(off, width):
        return lambda: _dot(h, wc_ref[:, off:off + width])

    kept = {}

    def keep_silu_xg(z):
        kept["xg"] = jax.nn.silu(z)

    def cross_attention(z):
        xq = z.astype(BF16)
        xscale = X_HEAD_DIM ** -0.5 * LOG2E
        yc_parts = []
        for hd in range(X_HEADS):
            sl = slice(hd * X_HEAD_DIM, (hd + 1) * X_HEAD_DIM)
            s = _nt_dot(xq[:, sl], xk_ref[:, sl]) * xscale
            yc_parts.append(_softmax_pv(s, xv_ref[:, sl]))
        yc_ref[...] = (jnp.concatenate(yc_parts, axis=1) * kept["xg"]).astype(BF16)

    def store_merge_gate(c):
        sl = slice(c * D_MODEL, (c + 1) * D_MODEL)

        def epilogue(z):
            gl_ref[:, sl] = jax.nn.sigmoid(z + bm_ref[:, sl]).astype(BF16)
        return epilogue

    def store_silu(ref):
        def epilogue(z):
            ref[...] = jax.nn.silu(z).astype(BF16)
        return epilogue

    def store_feature_major(zt):
        cos_t = cos_t_ref[...]
        sin_t = sin_t_ref[...]
        for ref, row0, scale in ((mqT_ref, ROW_MQ, att_scale), (dqT_ref, ROW_DQ, att_scale),
                                 (iqT_ref, ROW_IQ, qk_scale)):
            for hd in range(N_HEADS):
                r = row0 + hd * HEAD_DIM
                x1 = zt[r:r + HALF]
                x2 = zt[r + HALF:r + HEAD_DIM]
                o = hd * HEAD_DIM
                ref[o:o + HALF, :] = ((x1 * cos_t - x2 * sin_t) * scale).astype(BF16)
                ref[o + HALF:o + HEAD_DIM, :] = ((x2 * cos_t + x1 * sin_t) * scale).astype(BF16)
        mvT_ref[...] = zt[ROW_MV:ROW_MV + ATT_W].astype(BF16)
        dvT_ref[...] = zt[ROW_DV:ROW_DV + ATT_W].astype(BF16)
        iwT_ref[...] = zt[ROW_IW:ROW_IW + BF16_ROWS]

    def store_roped(ref, mean_ref=None):
        def epilogue(z):
            cos = cos_ref[...]
            sin = sin_ref[...]
            lane = lax.broadcasted_iota(jnp.int32, (PROJ_TILE, LANES), 1)
            first_half = (lane & HALF) == 0
            for c in range(z.shape[1] // LANES):
                lanes = slice(c * LANES, (c + 1) * LANES)
                zc = z[:, lanes]
                partner = jnp.where(first_half, pltpu.roll(zc, LANES - HALF, 1),
                                    pltpu.roll(zc, HALF, 1))
                r = zc * cos + partner * sin
                if mean_ref is not None:
                    for blk in range(PROJ_TILE // BLK):
                        mean_ref[blk, :, lanes] = jnp.mean(r[blk * BLK:(blk + 1) * BLK], axis=0,
                                                           keepdims=True)
                ref[:, lanes] = r.astype(ref.dtype)
        return epilogue

    stages = [(seg(COL_XG, X_W), keep_silu_xg), (seg(COL_XQ, X_W), cross_attention)]
    stages += [(seg(COL_GL + c * D_MODEL, D_MODEL), store_merge_gate(c)) for c in range(N_BRANCH)]
    stages += [(seg(COL_MG, ATT_W), store_silu(mg_ref)), (seg(COL_DG, ATT_W), store_silu(dg_ref)),
               (lambda: _nt_dot(wr_ref[...], h), store_feature_major),
               (seg(COL_MK, ATT_W), store_roped(mk_ref, mean_ref=km_ref)),
               (seg(COL_DK, ATT_W), store_roped(dk_ref)),
               (seg(COL_IK, LANES), store_roped(ik_ref))]
    nxt = stages[0][0]()
    for i, (_, epilogue) in enumerate(stages):
        cur = nxt
        if i + 1 < len(stages):
            nxt = stages[i + 1][0]()
        epilogue(cur)


def _in_proj(x2, g_in, w_cols, w_rows, tables, b_merge, xk, xv, batch):
    m = x2.shape[0]
    n_tiles = m // PROJ_TILE
    per_seq = SEQ // PROJ_TILE
    row = lambda w: pl.BlockSpec((PROJ_TILE, w), lambda i: (i, 0))
    feat = lambda r: pl.BlockSpec((None, r, PROJ_TILE), lambda i: (i // per_seq, 0, i % per_seq))
    const = lambda shape: pl.BlockSpec(shape, lambda i: (0, 0), pipeline_mode=pl.Buffered(1))
    tm = lambda w, dt: jax.ShapeDtypeStruct((m, w), dt)
    fm = lambda r, dt: jax.ShapeDtypeStruct((batch, r, SEQ), dt)
    out_shape = [
        fm(ATT_W, BF16), fm(ATT_W, BF16), fm(ATT_W, BF16),
        fm(ATT_W, BF16), fm(ATT_W, BF16), fm(BF16_ROWS, F32),
        tm(ATT_W, BF16), tm(ATT_W, BF16), tm(LANES, BF16),
        tm(ATT_W, BF16), tm(ATT_W, BF16),
        tm(X_W, BF16),
        tm(N_BRANCH * D_MODEL, BF16),
        jax.ShapeDtypeStruct((m // BLK, 1, ATT_W), F32),
    ]
    out_specs = [feat(ATT_W)] * 5 + [feat(BF16_ROWS)] + [
        row(ATT_W), row(ATT_W), row(LANES), row(ATT_W), row(ATT_W), row(X_W),
        row(N_BRANCH * D_MODEL),
        pl.BlockSpec((PROJ_TILE // BLK, 1, ATT_W), lambda i: (i, 0, 0))]
    mem = pl.BlockSpec((None, MEM_LEN, X_W), lambda i: (i // per_seq, 0, 0))
    cos, sin, cos_t, sin_t = tables
    return pl.pallas_call(
        _in_proj_body,
        grid=(n_tiles,),
        in_specs=[
            pl.BlockSpec((PROJ_TILE, D_MODEL), lambda i: (i, 0)),
            pl.BlockSpec((1, D_MODEL), lambda i: (0, 0)),
            const((D_MODEL, N_COLS)),
            const((N_ROWS, D_MODEL)),
            pl.BlockSpec((PROJ_TILE, LANES), lambda i: (i % per_seq, 0)),
            pl.BlockSpec((PROJ_TILE, LANES), lambda i: (i % per_seq, 0)),
            pl.BlockSpec((HALF, PROJ_TILE), lambda i: (0, i % per_seq)),
            pl.BlockSpec((HALF, PROJ_TILE), lambda i: (0, i % per_seq)),
            pl.BlockSpec((1, N_BRANCH * D_MODEL), lambda i: (0, 0)),
            mem, mem,
        ],
        out_specs=out_specs,
        out_shape=out_shape,
        compiler_params=pltpu.CompilerParams(
            dimension_semantics=("arbitrary",), vmem_limit_bytes=VMEM_LIMIT),
        name="in_proj",
    )(x2, g_in, w_cols, w_rows, cos, sin, cos_t, sin_t, b_merge, xk, xv)


def _mem_proj_body(m_ref, g_ref, w_ref, k_ref, v_ref):
    h = _rmsnorm(m_ref[...], g_ref[...]).astype(BF16)
    k_ref[...] = _dot(h, w_ref[:, 0:X_W]).astype(BF16)
    v_ref[...] = _dot(h, w_ref[:, X_W:2 * X_W]).astype(BF16)


def _mem_proj(mem2, g_mem, w_kv):
    m = mem2.shape[0]
    return pl.pallas_call(
        _mem_proj_body,
        grid=(m // BLK,),
        in_specs=[
            pl.BlockSpec((BLK, D_MODEL), lambda i: (i, 0)),
            pl.BlockSpec((1, D_MODEL), lambda i: (0, 0)),
            pl.BlockSpec((D_MODEL, 2 * X_W), lambda i: (0, 0)),
        ],
        out_specs=[pl.BlockSpec((BLK, X_W), lambda i: (i, 0))] * 2,
        out_shape=[jax.ShapeDtypeStruct((m, X_W), BF16)] * 2,
        compiler_params=pltpu.CompilerParams(
            dimension_semantics=("arbitrary",), vmem_limit_bytes=VMEM_LIMIT),
        name="mem_proj",
    )(mem2, g_mem, w_kv)


def _pair_head_masks():
    r = lax.broadcasted_iota(jnp.int32, (LANES, BLK), 0)
    return [(r >= e * HEAD_DIM) & (r < (e + 1) * HEAD_DIM) for e in range(HEADS_PER_PAIR)]


def _causal_block_mask_t():
    kr = lax.broadcasted_iota(jnp.int32, (BLK, BLK), 0)
    qc = lax.broadcasted_iota(jnp.int32, (BLK, BLK), 1)
    return kr <= qc


CHAIN_ROWS = 64


def _col_reduce(x, combine, finish):
    rows = x.shape[0]
    acc = x[0:CHAIN_ROWS]
    for r in range(1, rows // CHAIN_ROWS):
        acc = combine(acc, x[r * CHAIN_ROWS:(r + 1) * CHAIN_ROWS])
    return finish(acc, axis=0, keepdims=True)


SAFE_EXP = 100


def _softmax_pv_t(st, vt, shifted):
    if shifted:
        st = st - _col_reduce(st, jnp.maximum, jnp.max)
    p = jnp.exp2(st).astype(BF16)
    ones = jnp.ones((BF16_ROWS, vt.shape[1]), BF16)
    out = _dot(jnp.concatenate([vt, ones], axis=0), p)
    l = out[HEAD_DIM:HEAD_DIM + 1]
    return out[0:HEAD_DIM] / l, l


def _denominators_safe(ls):
    ok = None
    for l in ls:
        good = (l > 2.0 ** -SAFE_EXP) & (l < 2.0 ** SAFE_EXP)
        ok = good if ok is None else ok & good
    return jnp.min(jnp.where(ok, 1.0, 0.0))


def _attend_guarded(attend_all):
    safe = _denominators_safe(attend_all(False))

    @pl.when(safe < 0.5)
    def _():
        attend_all(True)


def _store_pair(o_ref, gate_ref, p, halves):
    lanes = slice(p * LANES, (p + 1) * LANES)
    y = jnp.concatenate(halves, axis=0).T
    o_ref[:, lanes] = (y * gate_ref[:, lanes].astype(F32)).astype(o_ref.dtype)


def _qblock_call(body, c, name, in_specs, args, gate, acc, batch, scratch_shapes=()):
    out_tile = pl.BlockSpec((SEQS, BLK, ATT_W), lambda b: (b, c, 0))
    n_blocked = len(args) + 1

    def per_sequence(*refs):
        blocked, (acc_ref, o_ref), scratch = (refs[:n_blocked], refs[n_blocked:n_blocked + 2],
                                              refs[n_blocked + 2:])

        def one_sequence(i, carry):
            body(*[r.at[i] for r in blocked], acc_ref, o_ref.at[i], *scratch)
            return carry

        lax.fori_loop(0, SEQS, one_sequence, 0)

    return pl.pallas_call(
        per_sequence,
        grid=(batch // SEQS,),
        in_specs=in_specs + [out_tile, pl.BlockSpec(memory_space=pl.ANY)],
        out_specs=out_tile,
        out_shape=jax.ShapeDtypeStruct(acc.shape, acc.dtype),
        input_output_aliases={len(args) + 1: 0},
        scratch_shapes=list(scratch_shapes),
        compiler_params=pltpu.CompilerParams(
            dimension_semantics=("arbitrary",), vmem_limit_bytes=VMEM_LIMIT),
        name=name,
    )(*args, gate, acc)


def _pair_pipeline(n_pairs, issue, consume):
    nxt = issue(0)
    for p in range(n_pairs):
        cur = nxt
        if p + 1 < n_pairs:
            nxt = issue(p + 1)
        consume(p, cur)


def _moba_select(qm, km, n_past):
    km16 = jnp.concatenate([km, jnp.zeros((BF16_ROWS - N_BLK, LANES), F32)], axis=0).astype(BF16)
    gate = _dot(km16, qm)
    blk = lax.broadcasted_iota(jnp.int32, (BF16_ROWS, BLK), 0)
    past = blk < n_past
    gate = jnp.where(past, gate, NEG_INF)
    rank = jnp.zeros((BF16_ROWS, BLK), F32)
    for jp in range(n_past):
        other = gate[jp:jp + 1, :]
        beats = (other > gate) | ((other == gate) & (blk > jp))
        rank = rank + jnp.where(beats, 1.0, 0.0)
    return jnp.where(past & (rank < MOBA_TOPK), 1.0, 0.0)


def _moba_block(c, qT_ref, k_ref, vT_ref, km_ref, gate_ref, acc_ref, o_ref):
    del acc_ref
    nk = (c + 1) * BLK
    causal = _causal_block_mask_t()
    head_masks = _pair_head_masks()
    gated = c > MOBA_TOPK

    def scores(h):
        p, e = divmod(h, HEADS_PER_PAIR)
        feats = slice(p * LANES, (p + 1) * LANES)
        q2 = qT_ref[feats, :]
        qm = jnp.where(head_masks[e], q2, jnp.zeros_like(q2))
        sel = _moba_select(qm, km_ref[:, feats], c) if gated else None
        return _dot(k_ref[:, feats], qm), sel

    def attend(h, st, sel, shifted):
        parts = []
        for j in range(c):
            blk = st[j * BLK:(j + 1) * BLK]
            if gated:
                blk = jnp.where(sel[j:j + 1, :] > 0.5, blk, NEG_INF)
            parts.append(blk)
        parts.append(jnp.where(causal, st[c * BLK:nk], NEG_INF))
        st = jnp.concatenate(parts, axis=0) if c else parts[0]
        return _softmax_pv_t(st, vT_ref[h * HEAD_DIM:(h + 1) * HEAD_DIM, :], shifted)

    def attend_all(shifted):
        denominators = []

        def consume(p, cur):
            outs = [attend(p * HEADS_PER_PAIR + e, *cur[e], shifted) for e in range(HEADS_PER_PAIR)]
            denominators.extend(l for _, l in outs)
            _store_pair(o_ref, gate_ref, p, [o for o, _ in outs])

        _pair_pipeline(
            N_PAIRS,
            lambda p: [scores(p * HEADS_PER_PAIR + e) for e in range(HEADS_PER_PAIR)],
            consume)
        return denominators

    _attend_guarded(attend_all)


def _moba(mqT, mk, mvT, kmean, gate, batch):
    mk3 = mk.reshape(batch, SEQ, ATT_W)
    acc = jnp.zeros((batch, SEQ, ATT_W), BF16)
    for c in range(N_BLK):
        nk = (c + 1) * BLK
        in_specs = [pl.BlockSpec((SEQS, ATT_W, BLK), lambda b, c=c: (b, 0, c)),
                    pl.BlockSpec((SEQS, nk, ATT_W), lambda b: (b, 0, 0)),
                    pl.BlockSpec((SEQS, ATT_W, nk), lambda b: (b, 0, 0)),
                    pl.BlockSpec((SEQS, N_BLK, ATT_W), lambda b: (b, 0, 0))]
        acc = _qblock_call(functools.partial(_moba_block, c), c, f"moba_q{c}", in_specs,
                           (mqT, mk3, mvT, kmean), gate.reshape(batch, SEQ, ATT_W), acc, batch)
    return acc.reshape(batch * SEQ, ATT_W)


def _key_to_f32(key):
    bits = key ^ ((key >> 31) & 0x7FFFFFFF)
    return lax.bitcast_convert_type(bits, F32)


def _count_ge(sc, thr):
    acc = jnp.zeros((CHAIN_ROWS, BLK), F32)
    for r in range(sc.shape[0] // CHAIN_ROWS):
        acc = acc + jnp.where(sc[r * CHAIN_ROWS:(r + 1) * CHAIN_ROWS] >= thr, 1.0, 0.0)
    return jnp.sum(acc, axis=0, keepdims=True)


POS_INF = float("inf")


def _dsa_topk_cap(sc, nk, cap_ref):
    kf = float(DSA_TOPK)
    cnt0 = _count_ge(sc, 0.0)
    nonneg = cnt0 >= kf
    t0 = jnp.where(nonneg, 0, INT_MIN).astype(jnp.int32)
    n0 = jnp.where(nonneg, cnt0, float(nk))

    def bit_step(it, state):
        t, n_ge = state
        cand = t | jnp.left_shift(jnp.int32(1), 30 - it)
        cnt = _count_ge(sc, _key_to_f32(cand))
        take = cnt >= kf
        return jnp.where(take, cand, t), jnp.where(take, cnt, n_ge)

    t, n_ge = lax.fori_loop(0, 31, bit_step, (t0, n0))
    t_val = _key_to_f32(t)
    ties_beyond_k = jnp.max(n_ge) > kf

    @pl.when(jnp.logical_not(ties_beyond_k))
    def _():
        cap_ref[...] = jnp.where(sc >= t_val, POS_INF, NEG_INF)

    @pl.when(ties_beyond_k)
    def _():
        t_next = _key_to_f32(t + 1)
        need = kf - _count_ge(sc, t_next)
        r = lax.broadcasted_iota(jnp.int32, (BLK, BLK), 0)
        c = lax.broadcasted_iota(jnp.int32, (BLK, BLK), 1)
        strict_lower = jnp.where(c < r, 1.0, 0.0).astype(BF16)
        carry = jnp.zeros((1, BLK), F32)
        for j in range(nk // BLK):
            rows = slice(j * BLK, (j + 1) * BLK)
            sc_j = sc[rows]
            gt = sc_j >= t_next
            eq = (sc_j >= t_val) & jnp.logical_not(gt)
            eq_f = jnp.where(eq, 1.0, 0.0)
            before = _dot(strict_lower, eq_f.astype(BF16)) + carry
            carry = carry + _col_reduce(eq_f, jnp.add, jnp.sum)
            cap_ref[rows, :] = jnp.where(gt | (eq & (before < need)), POS_INF, NEG_INF)

    return cap_ref[...]


def _dsa_block(c, iqT_ref, ik_ref, iwT_ref, qT_ref, k_ref, vT_ref, gate_ref, acc_ref, o_ref,
               cap_ref):
    del acc_ref
    nk = (c + 1) * BLK
    causal = _causal_block_mask_t()
    head_masks = _pair_head_masks()

    def masked_q_dots(keys, qT2):
        return [_dot(keys, jnp.where(m, qT2, jnp.zeros_like(qT2))) for m in head_masks]

    if c == 0:
        cap = jnp.where(causal, POS_INF, NEG_INF)
    else:
        wscale = IDX_HEADS ** -0.5
        acc = [jnp.zeros((nk, BLK), F32)]

        def accumulate(p, logits):
            for e in range(HEADS_PER_PAIR):
                w = iwT_ref[p * HEADS_PER_PAIR + e:p * HEADS_PER_PAIR + e + 1, :] * wscale
                acc[0] = acc[0] + jnp.maximum(logits[e], 0.0) * w

        _pair_pipeline(
            N_PAIRS,
            lambda p: masked_q_dots(ik_ref[...], iqT_ref[p * LANES:(p + 1) * LANES, :]),
            accumulate)
        own = jnp.where(causal, acc[0][c * BLK:nk], NEG_INF)
        sc = jnp.concatenate([acc[0][0:c * BLK], own], axis=0)
        cap = _dsa_topk_cap(sc, nk, cap_ref)

    def attend_all(shifted):
        denominators = []

        def consume(p, scores):
            outs = [_softmax_pv_t(
                jnp.minimum(scores[e], cap),
                vT_ref[p * LANES + e * HEAD_DIM:p * LANES + (e + 1) * HEAD_DIM, :], shifted)
                for e in range(HEADS_PER_PAIR)]
            denominators.extend(l for _, l in outs)
            _store_pair(o_ref, gate_ref, p, [o for o, _ in outs])

        _pair_pipeline(
            N_PAIRS,
            lambda p: masked_q_dots(k_ref[:, p * LANES:(p + 1) * LANES],
                                    qT_ref[p * LANES:(p + 1) * LANES, :]),
            consume)
        return denominators

    _attend_guarded(attend_all)


def _dsa(iqT, ik, iwT, dqT, dk, dvT, gate, batch):
    ik3 = ik.reshape(batch, SEQ, LANES)
    dk3 = dk.reshape(batch, SEQ, ATT_W)
    acc = jnp.zeros((batch, SEQ, ATT_W), BF16)
    for c in range(N_BLK):
        nk = (c + 1) * BLK
        qtile = lambda r, c=c: pl.BlockSpec((SEQS, r, BLK), lambda b: (b, 0, c))
        keys = lambda w, nk=nk: pl.BlockSpec((SEQS, nk, w), lambda b: (b, 0, 0))
        in_specs = [qtile(ATT_W), keys(LANES), qtile(BF16_ROWS), qtile(ATT_W), keys(ATT_W),
                    pl.BlockSpec((SEQS, ATT_W, nk), lambda b: (b, 0, 0))]
        acc = _qblock_call(functools.partial(_dsa_block, c), c, f"dsa_q{c}", in_specs,
                           (iqT, ik3, iwT, dqT, dk3, dvT), gate.reshape(batch, SEQ, ATT_W), acc,
                           batch, scratch_shapes=[pltpu.VMEM((nk, BLK), F32)])
    return acc.reshape(batch * SEQ, ATT_W)


def _merge_body(x_ref, ya_ref, yb_ref, yc_ref, gates_ref, wa_ref, wb_ref, wc_ref, wo_ref, gf_ref,
                o_ref):
    u = jnp.zeros((PROJ_TILE, D_MODEL), F32)
    for n, (y_ref, w_ref) in enumerate(((ya_ref, wa_ref), (yb_ref, wb_ref), (yc_ref, wc_ref))):
        sl = slice(n * D_MODEL, (n + 1) * D_MODEL)
        u = u + gates_ref[:, sl].astype(F32) * _dot(y_ref[...], w_ref[...])
    y = x_ref[...] + _dot(u.astype(BF16), wo_ref[...])
    o_ref[...] = _rmsnorm(y, gf_ref[...])


def _merge(x2, ya, yb, yc, gates, wa, wb, wc, wo, g_final):
    tile = lambda w: pl.BlockSpec((PROJ_TILE, w), lambda i: (i, 0))
    const = lambda r, w: pl.BlockSpec((r, w), lambda i: (0, 0))
    m = x2.shape[0]
    return pl.pallas_call(
        _merge_body,
        grid=(m // PROJ_TILE,),
        in_specs=[tile(D_MODEL), tile(ATT_W), tile(ATT_W), tile(X_W), tile(N_BRANCH * D_MODEL),
                  const(ATT_W, D_MODEL), const(ATT_W, D_MODEL), const(X_W, D_MODEL),
                  const(D_MODEL, D_MODEL), const(1, D_MODEL)],
        out_specs=tile(D_MODEL),
        out_shape=jax.ShapeDtypeStruct((m, D_MODEL), F32),
        compiler_params=pltpu.CompilerParams(
            dimension_semantics=("arbitrary",), vmem_limit_bytes=VMEM_LIMIT),
        name="merge",
    )(x2, ya, yb, yc, gates, wa, wb, wc, wo, g_final)


def _pack_w_in(w_in):
    offs = [0]
    for s in IN_SIZES:
        offs.append(offs[-1] + s)
    col = lambda n: w_in[:, offs[n]:offs[n + 1]]
    w_cols = jnp.concatenate([col(1), col(5), col(9), col(9), col(3), col(7), col(11), col(12),
                              col(13)], axis=1)
    iw = jnp.pad(col(10), ((0, 0), (0, BF16_ROWS - IDX_HEADS)))
    w_rows = jnp.concatenate([col(0), col(4), col(8), col(2), col(6), iw], axis=1).T
    return w_cols.astype(BF16), w_rows.astype(BF16)


def _rope_tables():
    inv = jnp.power(ROPE_THETA, -jnp.arange(HALF, dtype=F32) * 2.0 / HEAD_DIM)
    ang = jnp.arange(SEQ).astype(F32)[:, None] * inv[None, :]
    cos = jnp.cos(ang)
    sin = jnp.sin(ang)
    cos_lanes = jnp.tile(cos, (1, LANES // HALF))
    sin_lanes = jnp.tile(jnp.concatenate([-sin, sin], axis=1), (1, HEADS_PER_PAIR))
    return cos_lanes, sin_lanes, cos.T, sin.T


def _layer(x2, mem2, g_in, w_in, b_merge, g_mem, w_mem_kv, w_up_moba, w_up_dsa, w_up_cross, w_out,
           g_final, batch):
    w_cols, w_rows = _pack_w_in(w_in)
    xk, xv = _mem_proj(mem2, g_mem[None, :], w_mem_kv.astype(BF16))
    (mqT, dqT, iqT, mvT, dvT, iwT, mk, dk, ik, mgate, dgate, yc, gates, kmean) = _in_proj(
        x2, g_in[None, :], w_cols, w_rows, _rope_tables(), b_merge[None, :],
        xk.reshape(batch, MEM_LEN, X_W), xv.reshape(batch, MEM_LEN, X_W), batch)
    ya = _moba(mqT, mk, mvT, kmean.reshape(batch, N_BLK, ATT_W), mgate, batch)
    yb = _dsa(iqT, ik, iwT, dqT, dk, dvT, dgate, batch)
    return _merge(x2, ya, yb, yc, gates, w_up_moba.astype(BF16), w_up_dsa.astype(BF16),
                  w_up_cross.astype(BF16), w_out.astype(BF16), g_final)


def kernel(x, mem, g_in, w_in, b_merge, g_mem, w_mem_kv, w_up_moba, w_up_dsa, w_up_cross, w_out,
           g_final):
    batch, seq, d = x.shape
    assert seq == SEQ and d == D_MODEL and mem.shape[1] == MEM_LEN and batch % SEQS == 0
    assert g_in.shape[0] == 1
    out = _layer(x.reshape(batch * seq, d), mem.reshape(batch * MEM_LEN, d), g_in[0], w_in[0],
                 b_merge[0], g_mem[0], w_mem_kv[0], w_up_moba[0], w_up_dsa[0], w_up_cross[0],
                 w_out[0], g_final[None, :], batch)
    return out.reshape(batch, seq, d)
```

```python
import functools

import jax
import jax.numpy as jnp
from jax import lax
from jax.experimental import pallas as pl
from jax.experimental.pallas import tpu as pltpu

D_MODEL = 1024
SEQ = 2048
HEAD_DIM = 64
HALF = HEAD_DIM // 2
N_HEADS = 8
ATT_W = N_HEADS * HEAD_DIM
BLK = 256
N_BLK = SEQ // BLK
PROJ_TILE = 2 * BLK
MOBA_TOPK = 3
DSA_TOPK = 256
IDX_HEADS = 8
MEM_LEN = 256
X_HEADS = 4
X_HEAD_DIM = 128
X_W = X_HEADS * X_HEAD_DIM
N_BRANCH = 3
ROPE_THETA = 10000.0
RMS_EPS = 1e-6
LANES = 128
HEADS_PER_PAIR = LANES // HEAD_DIM
N_PAIRS = ATT_W // LANES
BF16_ROWS = 16

IN_SIZES = (ATT_W, ATT_W, ATT_W, ATT_W, ATT_W, ATT_W, ATT_W, ATT_W,
            IDX_HEADS * HEAD_DIM, HEAD_DIM, IDX_HEADS, X_W, X_W, N_BRANCH * D_MODEL)

COL_MK, COL_DK, COL_IK = 0, 512, 1024
COL_MG, COL_DG, COL_XQ, COL_XG, COL_GL = 1152, 1664, 2176, 2688, 3200
N_COLS = COL_GL + N_BRANCH * D_MODEL
ROW_MQ, ROW_DQ, ROW_IQ, ROW_MV, ROW_DV, ROW_IW = 0, 512, 1024, 1536, 2048, 2560
N_ROWS = ROW_IW + BF16_ROWS

VMEM_LIMIT = 56 * 1024 * 1024

F32 = jnp.float32
BF16 = jnp.bfloat16
NEG_INF = float("-inf")
INT_MIN = -2 ** 31
LOG2E = 1.4426950408889634


def _dot(a, b):
    return jnp.dot(a, b, preferred_element_type=F32)


def _nt_dot(a, b):
    return lax.dot_general(a, b, (((1,), (1,)), ((), ())), preferred_element_type=F32)


def _rmsnorm(x, g):
    ms = jnp.mean(x * x, axis=-1, keepdims=True)
    return (x * lax.rsqrt(ms + RMS_EPS)) * g


def _softmax_pv(s, v):
    m = jnp.max(s, axis=1, keepdims=True)
    p = jnp.exp2(s - m)
    l = jnp.sum(p, axis=1, keepdims=True)
    return _dot(p.astype(BF16), v) / l


def _in_proj_body(x_ref, g_ref, wc_ref, wr_ref, cos_ref, sin_ref, cos_t_ref, sin_t_ref,
                  bm_ref, xk_ref, xv_ref,
                  mqT_ref, dqT_ref, iqT_ref, mvT_ref, dvT_ref, iwT_ref,
                  mk_ref, dk_ref, ik_ref, mg_ref, dg_ref, yc_ref, gl_ref, km_ref, ya0_ref, yb0_ref):
    ya0_ref[...] = jnp.zeros(ya0_ref.shape, ya0_ref.dtype)
    yb0_ref[...] = jnp.zeros(yb0_ref.shape, yb0_ref.dtype)
    h = _rmsnorm(x_ref[...], g_ref[...]).astype(BF16)
    qk_scale = HEAD_DIM ** -0.5
    att_scale = qk_scale * LOG2E

    def seg(off, width):
        return lambda: _dot(h, wc_ref[:, off:off + width])

    kept = {}

    def keep_silu_xg(z):
        kept["xg"] = jax.nn.silu(z)

    def cross_attention(z):
        xq = z.astype(BF16)
        xscale = X_HEAD_DIM ** -0.5 * LOG2E
        yc_parts = []
        for hd in range(X_HEADS):
            sl = slice(hd * X_HEAD_DIM, (hd + 1) * X_HEAD_DIM)
            s = _nt_dot(xq[:, sl], xk_ref[:, sl]) * xscale
            yc_parts.append(_softmax_pv(s, xv_ref[:, sl]))
        yc_ref[...] = (jnp.concatenate(yc_parts, axis=1) * kept["xg"]).astype(BF16)

    def store_merge_gate(c):
        sl = slice(c * D_MODEL, (c + 1) * D_MODEL)

        def epilogue(z):
            gl_ref[:, sl] = jax.nn.sigmoid(z + bm_ref[:, sl]).astype(BF16)
        return epilogue

    def store_silu(ref):
        def epilogue(z):
            ref[...] = jax.nn.silu(z).astype(BF16)
        return epilogue

    def store_feature_major(zt):
        cos_t = cos_t_ref[...]
        sin_t = sin_t_ref[...]
        for ref, row0, scale in ((mqT_ref, ROW_MQ, att_scale), (dqT_ref, ROW_DQ, att_scale),
                                 (iqT_ref, ROW_IQ, qk_scale)):
            for hd in range(N_HEADS):
                r = row0 + hd * HEAD_DIM
                x1 = zt[r:r + HALF]
                x2 = zt[r + HALF:r + HEAD_DIM]
                o = hd * HEAD_DIM
                ref[o:o + HALF, :] = ((x1 * cos_t - x2 * sin_t) * scale).astype(BF16)
                ref[o + HALF:o + HEAD_DIM, :] = ((x2 * cos_t + x1 * sin_t) * scale).astype(BF16)
        mvT_ref[...] = zt[ROW_MV:ROW_MV + ATT_W].astype(BF16)
        dvT_ref[...] = zt[ROW_DV:ROW_DV + ATT_W].astype(BF16)
        iwT_ref[...] = zt[ROW_IW:ROW_IW + BF16_ROWS]

    def store_roped(ref, mean_ref=None):
        def epilogue(z):
            cos = cos_ref[...]
            sin = sin_ref[...]
            lane = lax.broadcasted_iota(jnp.int32, (PROJ_TILE, LANES), 1)
            first_half = (lane & HALF) == 0
            for c in range(z.shape[1] // LANES):
                lanes = slice(c * LANES, (c + 1) * LANES)
                zc = z[:, lanes]
                partner = jnp.where(first_half, pltpu.roll(zc, LANES - HALF, 1),
                                    pltpu.roll(zc, HALF, 1))
                r = zc * cos + partner * sin
                if mean_ref is not None:
                    for blk in range(PROJ_TILE // BLK):
                        mean_ref[blk, :, lanes] = jnp.mean(r[blk * BLK:(blk + 1) * BLK], axis=0,
                                                           keepdims=True)
                ref[:, lanes] = r.astype(ref.dtype)
        return epilogue

    stages = [(seg(COL_XG, X_W), keep_silu_xg), (seg(COL_XQ, X_W), cross_attention)]
    stages += [(seg(COL_GL + c * D_MODEL, D_MODEL), store_merge_gate(c)) for c in range(N_BRANCH)]
    stages += [(seg(COL_MG, ATT_W), store_silu(mg_ref)), (seg(COL_DG, ATT_W), store_silu(dg_ref)),
               (lambda: _nt_dot(wr_ref[...], h), store_feature_major),
               (seg(COL_MK, ATT_W), store_roped(mk_ref, mean_ref=km_ref)),
               (seg(COL_DK, ATT_W), store_roped(dk_ref)),
               (seg(COL_IK, LANES), store_roped(ik_ref))]
    nxt = stages[0][0]()
    for i, (_, epilogue) in enumerate(stages):
        cur = nxt
        if i + 1 < len(stages):
            nxt = stages[i + 1][0]()
        epilogue(cur)


def _in_proj(x2, g_in, w_cols, w_rows, tables, b_merge, xk, xv, batch):
    m = x2.shape[0]
    n_tiles = m // PROJ_TILE
    per_seq = SEQ // PROJ_TILE
    row = lambda w: pl.BlockSpec((PROJ_TILE, w), lambda i: (i, 0))
    feat = lambda r: pl.BlockSpec((None, r, PROJ_TILE), lambda i: (i // per_seq, 0, i % per_seq))
    const = lambda shape: pl.BlockSpec(shape, lambda i: (0, 0), pipeline_mode=pl.Buffered(1))
    tm = lambda w, dt: jax.ShapeDtypeStruct((m, w), dt)
    fm = lambda r, dt: jax.ShapeDtypeStruct((batch, r, SEQ), dt)
    out_shape = [
        fm(ATT_W, BF16), fm(ATT_W, BF16), fm(ATT_W, BF16),
        fm(ATT_W, BF16), fm(ATT_W, BF16), fm(BF16_ROWS, F32),
        tm(ATT_W, BF16), tm(ATT_W, BF16), tm(LANES, BF16),
        tm(ATT_W, BF16), tm(ATT_W, BF16),
        tm(X_W, BF16),
        tm(N_BRANCH * D_MODEL, BF16),
        jax.ShapeDtypeStruct((m // BLK, 1, ATT_W), F32),
        tm(ATT_W, BF16), tm(ATT_W, BF16),
    ]
    out_specs = [feat(ATT_W)] * 5 + [feat(BF16_ROWS)] + [
        row(ATT_W), row(ATT_W), row(LANES), row(ATT_W), row(ATT_W), row(X_W),
        row(N_BRANCH * D_MODEL),
        pl.BlockSpec((PROJ_TILE // BLK, 1, ATT_W), lambda i: (i, 0, 0)),
        row(ATT_W), row(ATT_W)]
    mem = pl.BlockSpec((None, MEM_LEN, X_W), lambda i: (i // per_seq, 0, 0))
    cos, sin, cos_t, sin_t = tables
    return pl.pallas_call(
        _in_proj_body,
        grid=(n_tiles,),
        in_specs=[
            pl.BlockSpec((PROJ_TILE, D_MODEL), lambda i: (i, 0)),
            pl.BlockSpec((1, D_MODEL), lambda i: (0, 0)),
            const((D_MODEL, N_COLS)),
            const((N_ROWS, D_MODEL)),
            pl.BlockSpec((PROJ_TILE, LANES), lambda i: (i % per_seq, 0)),
            pl.BlockSpec((PROJ_TILE, LANES), lambda i: (i % per_seq, 0)),
            pl.BlockSpec((HALF, PROJ_TILE), lambda i: (0, i % per_seq)),
            pl.BlockSpec((HALF, PROJ_TILE), lambda i: (0, i % per_seq)),
            pl.BlockSpec((1, N_BRANCH * D_MODEL), lambda i: (0, 0)),
            mem, mem,
        ],
        out_specs=out_specs,
        out_shape=out_shape,
        compiler_params=pltpu.CompilerParams(
            dimension_semantics=("arbitrary",), vmem_limit_bytes=VMEM_LIMIT),
        name="in_proj",
    )(x2, g_in, w_cols, w_rows, cos, sin, cos_t, sin_t, b_merge, xk, xv)


def _mem_proj_body(m_ref, g_ref, w_ref, k_ref, v_ref):
    h = _rmsnorm(m_ref[...], g_ref[...]).astype(BF16)
    k_ref[...] = _dot(h, w_ref[:, 0:X_W]).astype(BF16)
    v_ref[...] = _dot(h, w_ref[:, X_W:2 * X_W]).astype(BF16)


def _mem_proj(mem2, g_mem, w_kv):
    m = mem2.shape[0]
    return pl.pallas_call(
        _mem_proj_body,
        grid=(m // BLK,),
        in_specs=[
            pl.BlockSpec((BLK, D_MODEL), lambda i: (i, 0)),
            pl.BlockSpec((1, D_MODEL), lambda i: (0, 0)),
            pl.BlockSpec((D_MODEL, 2 * X_W), lambda i: (0, 0)),
        ],
        out_specs=[pl.BlockSpec((BLK, X_W), lambda i: (i, 0))] * 2,
        out_shape=[jax.ShapeDtypeStruct((m, X_W), BF16)] * 2,
        compiler_params=pltpu.CompilerParams(
            dimension_semantics=("arbitrary",), vmem_limit_bytes=VMEM_LIMIT),
        name="mem_proj",
    )(mem2, g_mem, w_kv)


def _pair_head_masks():
    r = lax.broadcasted_iota(jnp.int32, (LANES, BLK), 0)
    return [(r >= e * HEAD_DIM) & (r < (e + 1) * HEAD_DIM) for e in range(HEADS_PER_PAIR)]


def _causal_block_mask_t():
    kr = lax.broadcasted_iota(jnp.int32, (BLK, BLK), 0)
    qc = lax.broadcasted_iota(jnp.int32, (BLK, BLK), 1)
    return kr <= qc


CHAIN_ROWS = 64


def _col_reduce(x, combine, finish):
    rows = x.shape[0]
    acc = x[0:CHAIN_ROWS]
    for r in range(1, rows // CHAIN_ROWS):
        acc = combine(acc, x[r * CHAIN_ROWS:(r + 1) * CHAIN_ROWS])
    return finish(acc, axis=0, keepdims=True)


SAFE_EXP = 100


def _softmax_pv_t(st, vt, shifted):
    if shifted:
        st = st - _col_reduce(st, jnp.maximum, jnp.max)
    p = jnp.exp2(st).astype(BF16)
    ones = jnp.ones((BF16_ROWS, vt.shape[1]), BF16)
    out = _dot(jnp.concatenate([vt, ones], axis=0), p)
    l = out[HEAD_DIM:HEAD_DIM + 1]
    return out[0:HEAD_DIM] / l, l


def _denominators_safe(ls):
    ok = None
    for l in ls:
        good = (l > 2.0 ** -SAFE_EXP) & (l < 2.0 ** SAFE_EXP)
        ok = good if ok is None else ok & good
    return jnp.min(jnp.where(ok, 1.0, 0.0))


def _attend_guarded(attend_all):
    safe = _denominators_safe(attend_all(False))

    @pl.when(safe < 0.5)
    def _():
        attend_all(True)


def _store_pair(o_ref, gate_ref, p, halves):
    lanes = slice(p * LANES, (p + 1) * LANES)
    y = jnp.concatenate(halves, axis=0).T
    o_ref[:, lanes] = (y * gate_ref[:, lanes].astype(F32)).astype(o_ref.dtype)


def _qblock_call(body, c, name, in_specs, args, gate, acc, batch, scratch_shapes=()):
    out_tile = pl.BlockSpec((None, BLK, ATT_W), lambda b: (b, c, 0))
    return pl.pallas_call(
        body,
        grid=(batch,),
        in_specs=in_specs + [out_tile, pl.BlockSpec(memory_space=pl.ANY)],
        out_specs=out_tile,
        out_shape=jax.ShapeDtypeStruct(acc.shape, acc.dtype),
        input_output_aliases={len(args) + 1: 0},
        scratch_shapes=list(scratch_shapes),
        compiler_params=pltpu.CompilerParams(
            dimension_semantics=("arbitrary",), vmem_limit_bytes=VMEM_LIMIT),
        name=name,
    )(*args, gate, acc)


def _pair_pipeline(n_pairs, issue, consume):
    nxt = issue(0)
    for p in range(n_pairs):
        cur = nxt
        if p + 1 < n_pairs:
            nxt = issue(p + 1)
        consume(p, cur)


def _moba_select(qm, km, n_past):
    km16 = jnp.concatenate([km, jnp.zeros((BF16_ROWS - N_BLK, LANES), F32)], axis=0).astype(BF16)
    gate = _dot(km16, qm)
    blk = lax.broadcasted_iota(jnp.int32, (BF16_ROWS, BLK), 0)
    past = blk < n_past
    gate = jnp.where(past, gate, NEG_INF)
    rank = jnp.zeros((BF16_ROWS, BLK), F32)
    for jp in range(n_past):
        other = gate[jp:jp + 1, :]
        beats = (other > gate) | ((other == gate) & (blk > jp))
        rank = rank + jnp.where(beats, 1.0, 0.0)
    return jnp.where(past & (rank < MOBA_TOPK), 1.0, 0.0)


def _moba_block(c, qT_ref, k_ref, vT_ref, km_ref, gate_ref, acc_ref, o_ref):
    del acc_ref
    nk = (c + 1) * BLK
    causal = _causal_block_mask_t()
    head_masks = _pair_head_masks()
    gated = c > MOBA_TOPK

    def scores(h):
        p, e = divmod(h, HEADS_PER_PAIR)
        feats = slice(p * LANES, (p + 1) * LANES)
        q2 = qT_ref[feats, :]
        qm = jnp.where(head_masks[e], q2, jnp.zeros_like(q2))
        sel = _moba_select(qm, km_ref[:, feats], c) if gated else None
        return _dot(k_ref[:, feats], qm), sel

    def attend(h, st, sel, shifted):
        parts = []
        for j in range(c):
            blk = st[j * BLK:(j + 1) * BLK]
            if gated:
                blk = jnp.where(sel[j:j + 1, :] > 0.5, blk, NEG_INF)
            parts.append(blk)
        parts.append(jnp.where(causal, st[c * BLK:nk], NEG_INF))
        st = jnp.concatenate(parts, axis=0) if c else parts[0]
        return _softmax_pv_t(st, vT_ref[h * HEAD_DIM:(h + 1) * HEAD_DIM, :], shifted)

    def attend_all(shifted):
        denominators = []

        def consume(p, cur):
            outs = [attend(p * HEADS_PER_PAIR + e, *cur[e], shifted) for e in range(HEADS_PER_PAIR)]
            denominators.extend(l for _, l in outs)
            _store_pair(o_ref, gate_ref, p, [o for o, _ in outs])

        _pair_pipeline(
            N_PAIRS,
            lambda p: [scores(p * HEADS_PER_PAIR + e) for e in range(HEADS_PER_PAIR)],
            consume)
        return denominators

    _attend_guarded(attend_all)


def _moba(mqT, mk, mvT, kmean, gate, zeroed, batch):
    mk3 = mk.reshape(batch, SEQ, ATT_W)
    acc = zeroed.reshape(batch, SEQ, ATT_W)
    for c in range(N_BLK):
        nk = (c + 1) * BLK
        in_specs = [pl.BlockSpec((None, ATT_W, BLK), lambda b, c=c: (b, 0, c)),
                    pl.BlockSpec((None, nk, ATT_W), lambda b: (b, 0, 0)),
                    pl.BlockSpec((None, ATT_W, nk), lambda b: (b, 0, 0)),
                    pl.BlockSpec((None, N_BLK, ATT_W), lambda b: (b, 0, 0))]
        acc = _qblock_call(functools.partial(_moba_block, c), c, f"moba_q{c}", in_specs,
                           (mqT, mk3, mvT, kmean), gate.reshape(batch, SEQ, ATT_W), acc, batch)
    return acc.reshape(batch * SEQ, ATT_W)


def _key_to_f32(key):
    bits = key ^ ((key >> 31) & 0x7FFFFFFF)
    return lax.bitcast_convert_type(bits, F32)


def _count_ge(sc, thr):
    acc = jnp.zeros((CHAIN_ROWS, BLK), F32)
    for r in range(sc.shape[0] // CHAIN_ROWS):
        acc = acc + jnp.where(sc[r * CHAIN_ROWS:(r + 1) * CHAIN_ROWS] >= thr, 1.0, 0.0)
    return jnp.sum(acc, axis=0, keepdims=True)


POS_INF = float("inf")


def _dsa_topk_cap(sc, nk, cap_ref):
    kf = float(DSA_TOPK)
    cnt0 = _count_ge(sc, 0.0)
    nonneg = cnt0 >= kf
    t0 = jnp.where(nonneg, 0, INT_MIN).astype(jnp.int32)
    n0 = jnp.where(nonneg, cnt0, float(nk))

    def bit_step(it, state):
        t, n_ge = state
        cand = t | jnp.left_shift(jnp.int32(1), 30 - it)
        cnt = _count_ge(sc, _key_to_f32(cand))
        take = cnt >= kf
        return jnp.where(take, cand, t), jnp.where(take, cnt, n_ge)

    t, n_ge = lax.fori_loop(0, 31, bit_step, (t0, n0))
    t_val = _key_to_f32(t)
    ties_beyond_k = jnp.max(n_ge) > kf

    @pl.when(jnp.logical_not(ties_beyond_k))
    def _():
        cap_ref[...] = jnp.where(sc >= t_val, POS_INF, NEG_INF)

    @pl.when(ties_beyond_k)
    def _():
        t_next = _key_to_f32(t + 1)
        need = kf - _count_ge(sc, t_next)
        r = lax.broadcasted_iota(jnp.int32, (BLK, BLK), 0)
        c = lax.broadcasted_iota(jnp.int32, (BLK, BLK), 1)
        strict_lower = jnp.where(c < r, 1.0, 0.0).astype(BF16)
        carry = jnp.zeros((1, BLK), F32)
        for j in range(nk // BLK):
            rows = slice(j * BLK, (j + 1) * BLK)
            sc_j = sc[rows]
            gt = sc_j >= t_next
            eq = (sc_j >= t_val) & jnp.logical_not(gt)
            eq_f = jnp.where(eq, 1.0, 0.0)
            before = _dot(strict_lower, eq_f.astype(BF16)) + carry
            carry = carry + _col_reduce(eq_f, jnp.add, jnp.sum)
            cap_ref[rows, :] = jnp.where(gt | (eq & (before < need)), POS_INF, NEG_INF)

    return cap_ref[...]


def _dsa_block(c, iqT_ref, ik_ref, iwT_ref, qT_ref, k_ref, vT_ref, gate_ref, acc_ref, o_ref,
               cap_ref):
    del acc_ref
    nk = (c + 1) * BLK
    causal = _causal_block_mask_t()
    head_masks = _pair_head_masks()

    def masked_q_dots(keys, qT2):
        return [_dot(keys, jnp.where(m, qT2, jnp.zeros_like(qT2))) for m in head_masks]

    if c == 0:
        cap = jnp.where(causal, POS_INF, NEG_INF)
    else:
        wscale = IDX_HEADS ** -0.5
        acc = [jnp.zeros((nk, BLK), F32)]

        def accumulate(p, logits):
            for e in range(HEADS_PER_PAIR):
                w = iwT_ref[p * HEADS_PER_PAIR + e:p * HEADS_PER_PAIR + e + 1, :] * wscale
                acc[0] = acc[0] + jnp.maximum(logits[e], 0.0) * w

        _pair_pipeline(
            N_PAIRS,
            lambda p: masked_q_dots(ik_ref[...], iqT_ref[p * LANES:(p + 1) * LANES, :]),
            accumulate)
        own = jnp.where(causal, acc[0][c * BLK:nk], NEG_INF)
        sc = jnp.concatenate([acc[0][0:c * BLK], own], axis=0)
        cap = _dsa_topk_cap(sc, nk, cap_ref)

    def attend_all(shifted):
        denominators = []

        def consume(p, scores):
            outs = [_softmax_pv_t(
                jnp.minimum(scores[e], cap),
                vT_ref[p * LANES + e * HEAD_DIM:p * LANES + (e + 1) * HEAD_DIM, :], shifted)
                for e in range(HEADS_PER_PAIR)]
            denominators.extend(l for _, l in outs)
            _store_pair(o_ref, gate_ref, p, [o for o, _ in outs])

        _pair_pipeline(
            N_PAIRS,
            lambda p: masked_q_dots(k_ref[:, p * LANES:(p + 1) * LANES],
                                    qT_ref[p * LANES:(p + 1) * LANES, :]),
            consume)
        return denominators

    _attend_guarded(attend_all)


def _dsa(iqT, ik, iwT, dqT, dk, dvT, gate, zeroed, batch):
    ik3 = ik.reshape(batch, SEQ, LANES)
    dk3 = dk.reshape(batch, SEQ, ATT_W)
    acc = zeroed.reshape(batch, SEQ, ATT_W)
    for c in range(N_BLK):
        nk = (c + 1) * BLK
        qtile = lambda r, c=c: pl.BlockSpec((None, r, BLK), lambda b: (b, 0, c))
        keys = lambda w, nk=nk: pl.BlockSpec((None, nk, w), lambda b: (b, 0, 0))
        in_specs = [qtile(ATT_W), keys(LANES), qtile(BF16_ROWS), qtile(ATT_W), keys(ATT_W),
                    pl.BlockSpec((None, ATT_W, nk), lambda b: (b, 0, 0))]
        acc = _qblock_call(functools.partial(_dsa_block, c), c, f"dsa_q{c}", in_specs,
                           (iqT, ik3, iwT, dqT, dk3, dvT), gate.reshape(batch, SEQ, ATT_W), acc,
                           batch, scratch_shapes=[pltpu.VMEM((nk, BLK), F32)])
    return acc.reshape(batch * SEQ, ATT_W)


def _merge_body(x_ref, ya_ref, yb_ref, yc_ref, gates_ref, wa_ref, wb_ref, wc_ref, wo_ref, gf_ref,
                o_ref):
    u = jnp.zeros((PROJ_TILE, D_MODEL), F32)
    for n, (y_ref, w_ref) in enumerate(((ya_ref, wa_ref), (yb_ref, wb_ref), (yc_ref, wc_ref))):
        sl = slice(n * D_MODEL, (n + 1) * D_MODEL)
        u = u + gates_ref[:, sl].astype(F32) * _dot(y_ref[...], w_ref[...])
    y = x_ref[...] + _dot(u.astype(BF16), wo_ref[...])
    o_ref[...] = _rmsnorm(y, gf_ref[...])


def _merge(x2, ya, yb, yc, gates, wa, wb, wc, wo, g_final):
    tile = lambda w: pl.BlockSpec((PROJ_TILE, w), lambda i: (i, 0))
    const = lambda r, w: pl.BlockSpec((r, w), lambda i: (0, 0))
    m = x2.shape[0]
    return pl.pallas_call(
        _merge_body,
        grid=(m // PROJ_TILE,),
        in_specs=[tile(D_MODEL), tile(ATT_W), tile(ATT_W), tile(X_W), tile(N_BRANCH * D_MODEL),
                  const(ATT_W, D_MODEL), const(ATT_W, D_MODEL), const(X_W, D_MODEL),
                  const(D_MODEL, D_MODEL), const(1, D_MODEL)],
        out_specs=tile(D_MODEL),
        out_shape=jax.ShapeDtypeStruct((m, D_MODEL), F32),
        compiler_params=pltpu.CompilerParams(
            dimension_semantics=("arbitrary",), vmem_limit_bytes=VMEM_LIMIT),
        name="merge",
    )(x2, ya, yb, yc, gates, wa, wb, wc, wo, g_final)


def _pack_w_in(w_in):
    offs = [0]
    for s in IN_SIZES:
        offs.append(offs[-1] + s)
    col = lambda n: w_in[:, offs[n]:offs[n + 1]]
    w_cols = jnp.concatenate([col(1), col(5), col(9), col(9), col(3), col(7), col(11), col(12),
                              col(13)], axis=1)
    iw = jnp.pad(col(10), ((0, 0), (0, BF16_ROWS - IDX_HEADS)))
    w_rows = jnp.concatenate([col(0), col(4), col(8), col(2), col(6), iw], axis=1).astype(BF16).T
    return w_cols.astype(BF16), w_rows


def _rope_tables():
    inv = jnp.power(ROPE_THETA, -jnp.arange(HALF, dtype=F32) * 2.0 / HEAD_DIM)
    ang = jnp.arange(SEQ).astype(F32)[:, None] * inv[None, :]
    cos = jnp.cos(ang)
    sin = jnp.sin(ang)
    cos_lanes = jnp.tile(cos, (1, LANES // HALF))
    sin_lanes = jnp.tile(jnp.concatenate([-sin, sin], axis=1), (1, HEADS_PER_PAIR))
    return cos_lanes, sin_lanes, cos.T, sin.T


def _layer(x2, mem2, g_in, w_in, b_merge, g_mem, w_mem_kv, w_up_moba, w_up_dsa, w_up_cross, w_out,
           g_final, batch):
    w_cols, w_rows = _pack_w_in(w_in)
    xk, xv = _mem_proj(mem2, g_mem[None, :], w_mem_kv.astype(BF16))
    (mqT, dqT, iqT, mvT, dvT, iwT, mk, dk, ik, mgate, dgate, yc, gates, kmean, ya0, yb0) = _in_proj(
        x2, g_in[None, :], w_cols, w_rows, _rope_tables(), b_merge[None, :],
        xk.reshape(batch, MEM_LEN, X_W), xv.reshape(batch, MEM_LEN, X_W), batch)
    ya = _moba(mqT, mk, mvT, kmean.reshape(batch, N_BLK, ATT_W), mgate, ya0, batch)
    yb = _dsa(iqT, ik, iwT, dqT, dk, dvT, dgate, yb0, batch)
    return _merge(x2, ya, yb, yc, gates, w_up_moba.astype(BF16), w_up_dsa.astype(BF16),
                  w_up_cross.astype(BF16), w_out.astype(BF16), g_final)


def kernel(x, mem, g_in, w_in, b_merge, g_mem, w_mem_kv, w_up_moba, w_up_dsa, w_up_cross, w_out,
           g_final):
    batch, seq, d = x.shape
    assert seq == SEQ and d == D_MODEL and mem.shape[1] == MEM_LEN
    assert g_in.shape[0] == 1
    out = _layer(x.reshape(batch * seq, d), mem.reshape(batch * MEM_LEN, d), g_in[0], w_in[0],
                 b_merge[0], g_mem[0], w_mem_kv[0], w_up_moba[0], w_up_dsa[0], w_up_cross[0],
                 w_out[0], g_final[None, :], batch)
    return out.reshape(batch, seq, d)
```

```python
import functools

import jax
import jax.numpy as jnp
from jax import lax
from jax.experimental import pallas as pl
from jax.experimental.pallas import tpu as pltpu

D_MODEL = 1024
SEQ = 2048
HEAD_DIM = 64
HALF = HEAD_DIM // 2
N_HEADS = 8
ATT_W = N_HEADS * HEAD_DIM
BLK = 256
N_BLK = SEQ // BLK
PROJ_TILE = 2 * BLK
MOBA_TOPK = 3
DSA_TOPK = 256
IDX_HEADS = 8
MEM_LEN = 256
X_HEADS = 4
X_HEAD_DIM = 128
X_W = X_HEADS * X_HEAD_DIM
N_BRANCH = 3
ROPE_THETA = 10000.0
RMS_EPS = 1e-6
LANES = 128
HEADS_PER_PAIR = LANES // HEAD_DIM
N_PAIRS = ATT_W // LANES
BF16_ROWS = 16

IN_SIZES = (ATT_W, ATT_W, ATT_W, ATT_W, ATT_W, ATT_W, ATT_W, ATT_W,
            IDX_HEADS * HEAD_DIM, HEAD_DIM, IDX_HEADS, X_W, X_W, N_BRANCH * D_MODEL)

COL_MK, COL_DK, COL_IK = 0, 512, 1024
COL_MG, COL_DG, COL_XQ, COL_XG, COL_GL = 1152, 1664, 2176, 2688, 3200
N_COLS = COL_GL + N_BRANCH * D_MODEL
ROW_MQ, ROW_DQ, ROW_IQ, ROW_MV, ROW_DV, ROW_IW = 0, 512, 1024, 1536, 2048, 2560
N_ROWS = ROW_IW + BF16_ROWS

VMEM_LIMIT = 56 * 1024 * 1024

F32 = jnp.float32
BF16 = jnp.bfloat16
NEG_INF = float("-inf")
INT_MIN = -2 ** 31
LOG2E = 1.4426950408889634


def _dot(a, b):
    return jnp.dot(a, b, preferred_element_type=F32)


def _nt_dot(a, b):
    return lax.dot_general(a, b, (((1,), (1,)), ((), ())), preferred_element_type=F32)


def _rmsnorm(x, g):
    ms = jnp.mean(x * x, axis=-1, keepdims=True)
    return (x * lax.rsqrt(ms + RMS_EPS)) * g


def _softmax_pv(s, v):
    m = jnp.max(s, axis=1, keepdims=True)
    p = jnp.exp2(s - m)
    l = jnp.sum(p, axis=1, keepdims=True)
    return _dot(p.astype(BF16), v) / l


def _in_proj_body(x_ref, g_ref, wc_ref, wr_ref, cos_ref, sin_ref, cos_t_ref, sin_t_ref,
                  bm_ref, xk_ref, xv_ref,
                  mqT_ref, dqT_ref, iqT_ref, mvT_ref, dvT_ref, iwT_ref,
                  mk_ref, dk_ref, ik_ref, mg_ref, dg_ref, yc_ref, gl_ref, km_ref, ya0_ref, yb0_ref):
    ya0_ref[...] = jnp.zeros(ya0_ref.shape, ya0_ref.dtype)
    yb0_ref[...] = jnp.zeros(yb0_ref.shape, yb0_ref.dtype)
    h = _rmsnorm(x_ref[...], g_ref[...]).astype(BF16)
    qk_scale = HEAD_DIM ** -0.5
    att_scale = qk_scale * LOG2E

    def seg(off, width):
        return lambda: _dot(h, wc_ref[:, off:off + width])

    kept = {}

    def keep_silu_xg(z):
        kept["xg"] = jax.nn.silu(z)

    def cross_attention(z):
        xq = z.astype(BF16)
        xscale = X_HEAD_DIM ** -0.5 * LOG2E
        yc_parts = []
        for hd in range(X_HEADS):
            sl = slice(hd * X_HEAD_DIM, (hd + 1) * X_HEAD_DIM)
            s = _nt_dot(xq[:, sl], xk_ref[:, sl]) * xscale
            yc_parts.append(_softmax_pv(s, xv_ref[:, sl]))
        yc_ref[...] = (jnp.concatenate(yc_parts, axis=1) * kept["xg"]).astype(BF16)

    def store_merge_gate(c):
        sl = slice(c * D_MODEL, (c + 1) * D_MODEL)

        def epilogue(z):
            gl_ref[:, sl] = jax.nn.sigmoid(z + bm_ref[:, sl]).astype(BF16)
        return epilogue

    def store_silu(ref):
        def epilogue(z):
            ref[...] = jax.nn.silu(z).astype(BF16)
        return epilogue

    def store_feature_major(zt):
        cos_t = cos_t_ref[...]
        sin_t = sin_t_ref[...]
        for ref, row0, scale in ((mqT_ref, ROW_MQ, att_scale), (dqT_ref, ROW_DQ, att_scale),
                                 (iqT_ref, ROW_IQ, qk_scale)):
            for hd in range(N_HEADS):
                r = row0 + hd * HEAD_DIM
                x1 = zt[r:r + HALF]
                x2 = zt[r + HALF:r + HEAD_DIM]
                o = hd * HEAD_DIM
                ref[o:o + HALF, :] = ((x1 * cos_t - x2 * sin_t) * scale).astype(BF16)
                ref[o + HALF:o + HEAD_DIM, :] = ((x2 * cos_t + x1 * sin_t) * scale).astype(BF16)
        mvT_ref[...] = zt[ROW_MV:ROW_MV + ATT_W].astype(BF16)
        dvT_ref[...] = zt[ROW_DV:ROW_DV + ATT_W].astype(BF16)
        iwT_ref[...] = zt[ROW_IW:ROW_IW + BF16_ROWS]

    def store_roped(ref, mean_ref=None):
        def epilogue(z):
            cos = cos_ref[...]
            sin = sin_ref[...]
            lane = lax.broadcasted_iota(jnp.int32, (PROJ_TILE, LANES), 1)
            first_half = (lane & HALF) == 0
            for c in range(z.shape[1] // LANES):
                lanes = slice(c * LANES, (c + 1) * LANES)
                zc = z[:, lanes]
                partner = jnp.where(first_half, pltpu.roll(zc, LANES - HALF, 1),
                                    pltpu.roll(zc, HALF, 1))
                r = zc * cos + partner * sin
                if mean_ref is not None:
                    for blk in range(PROJ_TILE // BLK):
                        mean_ref[blk, :, lanes] = jnp.mean(r[blk * BLK:(blk + 1) * BLK], axis=0,
                                                           keepdims=True)
                ref[:, lanes] = r.astype(ref.dtype)
        return epilogue

    stages = [(seg(COL_XG, X_W), keep_silu_xg), (seg(COL_XQ, X_W), cross_attention)]
    stages += [(seg(COL_GL + c * D_MODEL, D_MODEL), store_merge_gate(c)) for c in range(N_BRANCH)]
    stages += [(seg(COL_MG, ATT_W), store_silu(mg_ref)), (seg(COL_DG, ATT_W), store_silu(dg_ref)),
               (lambda: _nt_dot(wr_ref[...], h), store_feature_major),
               (seg(COL_MK, ATT_W), store_roped(mk_ref, mean_ref=km_ref)),
               (seg(COL_DK, ATT_W), store_roped(dk_ref)),
               (seg(COL_IK, LANES), store_roped(ik_ref))]
    nxt = stages[0][0]()
    for i, (_, epilogue) in enumerate(stages):
        cur = nxt
        if i + 1 < len(stages):
            nxt = stages[i + 1][0]()
        epilogue(cur)


def _in_proj(x2, g_in, w_cols, w_rows, tables, b_merge, xk, xv, batch):
    m = x2.shape[0]
    n_tiles = m // PROJ_TILE
    per_seq = SEQ // PROJ_TILE
    row = lambda w: pl.BlockSpec((PROJ_TILE, w), lambda i: (i, 0))
    feat = lambda r: pl.BlockSpec((None, r, PROJ_TILE), lambda i: (i // per_seq, 0, i % per_seq))
    const = lambda shape: pl.BlockSpec(shape, lambda i: (0, 0), pipeline_mode=pl.Buffered(1))
    tm = lambda w, dt: jax.ShapeDtypeStruct((m, w), dt)
    fm = lambda r, dt: jax.ShapeDtypeStruct((batch, r, SEQ), dt)
    out_shape = [
        fm(ATT_W, BF16), fm(ATT_W, BF16), fm(ATT_W, BF16),
        fm(ATT_W, BF16), fm(ATT_W, BF16), fm(BF16_ROWS, F32),
        tm(ATT_W, BF16), tm(ATT_W, BF16), tm(LANES, BF16),
        tm(ATT_W, BF16), tm(ATT_W, BF16),
        tm(X_W, BF16),
        tm(N_BRANCH * D_MODEL, BF16),
        jax.ShapeDtypeStruct((m // BLK, 1, ATT_W), F32),
        tm(ATT_W, BF16), tm(ATT_W, BF16),
    ]
    out_specs = [feat(ATT_W)] * 5 + [feat(BF16_ROWS)] + [
        row(ATT_W), row(ATT_W), row(LANES), row(ATT_W), row(ATT_W), row(X_W),
        row(N_BRANCH * D_MODEL),
        pl.BlockSpec((PROJ_TILE // BLK, 1, ATT_W), lambda i: (i, 0, 0)),
        row(ATT_W), row(ATT_W)]
    mem = pl.BlockSpec((None, MEM_LEN, X_W), lambda i: (i // per_seq, 0, 0))
    cos, sin, cos_t, sin_t = tables
    return pl.pallas_call(
        _in_proj_body,
        grid=(n_tiles,),
        in_specs=[
            pl.BlockSpec((PROJ_TILE, D_MODEL), lambda i: (i, 0)),
            pl.BlockSpec((1, D_MODEL), lambda i: (0, 0)),
            const((D_MODEL, N_COLS)),
            const((N_ROWS, D_MODEL)),
            pl.BlockSpec((PROJ_TILE, LANES), lambda i: (i % per_seq, 0)),
            pl.BlockSpec((PROJ_TILE, LANES), lambda i: (i % per_seq, 0)),
            pl.BlockSpec((HALF, PROJ_TILE), lambda i: (0, i % per_seq)),
            pl.BlockSpec((HALF, PROJ_TILE), lambda i: (0, i % per_seq)),
            pl.BlockSpec((1, N_BRANCH * D_MODEL), lambda i: (0, 0)),
            mem, mem,
        ],
        out_specs=out_specs,
        out_shape=out_shape,
        compiler_params=pltpu.CompilerParams(
            dimension_semantics=("arbitrary",), vmem_limit_bytes=VMEM_LIMIT),
        name="in_proj",
    )(x2, g_in, w_cols, w_rows, cos, sin, cos_t, sin_t, b_merge, xk, xv)


def _mem_proj_body(m_ref, g_ref, w_ref, k_ref, v_ref):
    h = _rmsnorm(m_ref[...], g_ref[...]).astype(BF16)
    k_ref[...] = _dot(h, w_ref[:, 0:X_W]).astype(BF16)
    v_ref[...] = _dot(h, w_ref[:, X_W:2 * X_W]).astype(BF16)


def _mem_proj(mem2, g_mem, w_kv):
    m = mem2.shape[0]
    return pl.pallas_call(
        _mem_proj_body,
        grid=(m // PROJ_TILE,),
        in_specs=[
            pl.BlockSpec((PROJ_TILE, D_MODEL), lambda i: (i, 0)),
            pl.BlockSpec((1, D_MODEL), lambda i: (0, 0)),
            pl.BlockSpec((D_MODEL, 2 * X_W), lambda i: (0, 0)),
        ],
        out_specs=[pl.BlockSpec((PROJ_TILE, X_W), lambda i: (i, 0))] * 2,
        out_shape=[jax.ShapeDtypeStruct((m, X_W), BF16)] * 2,
        compiler_params=pltpu.CompilerParams(
            dimension_semantics=("arbitrary",), vmem_limit_bytes=VMEM_LIMIT),
        name="mem_proj",
    )(mem2, g_mem, w_kv)


def _pair_head_masks():
    r = lax.broadcasted_iota(jnp.int32, (LANES, BLK), 0)
    return [(r >= e * HEAD_DIM) & (r < (e + 1) * HEAD_DIM) for e in range(HEADS_PER_PAIR)]


def _causal_block_mask_t():
    kr = lax.broadcasted_iota(jnp.int32, (BLK, BLK), 0)
    qc = lax.broadcasted_iota(jnp.int32, (BLK, BLK), 1)
    return kr <= qc


CHAIN_ROWS = 64


def _col_reduce(x, combine, finish):
    rows = x.shape[0]
    acc = x[0:CHAIN_ROWS]
    for r in range(1, rows // CHAIN_ROWS):
        acc = combine(acc, x[r * CHAIN_ROWS:(r + 1) * CHAIN_ROWS])
    return finish(acc, axis=0, keepdims=True)


SAFE_EXP = 100


def _softmax_pv_t(st, vt, shifted):
    if shifted:
        st = st - _col_reduce(st, jnp.maximum, jnp.max)
    p = jnp.exp2(st).astype(BF16)
    ones = jnp.ones((BF16_ROWS, vt.shape[1]), BF16)
    out = _dot(jnp.concatenate([vt, ones], axis=0), p)
    l = out[HEAD_DIM:HEAD_DIM + 1]
    return out[0:HEAD_DIM] / l, l


def _denominators_safe(ls):
    ok = None
    for l in ls:
        good = (l > 2.0 ** -SAFE_EXP) & (l < 2.0 ** SAFE_EXP)
        ok = good if ok is None else ok & good
    return jnp.min(jnp.where(ok, 1.0, 0.0))


def _attend_guarded(attend_all):
    safe = _denominators_safe(attend_all(False))

    @pl.when(safe < 0.5)
    def _():
        attend_all(True)


def _store_pair(o_ref, gate_ref, p, halves):
    lanes = slice(p * LANES, (p + 1) * LANES)
    y = jnp.concatenate(halves, axis=0).T
    o_ref[:, lanes] = (y * gate_ref[:, lanes].astype(F32)).astype(o_ref.dtype)


def _qblock_call(body, c, name, in_specs, args, gate, acc, batch, scratch_shapes=()):
    out_tile = pl.BlockSpec((None, BLK, ATT_W), lambda b: (b, c, 0))
    return pl.pallas_call(
        body,
        grid=(batch,),
        in_specs=in_specs + [out_tile, pl.BlockSpec(memory_space=pl.ANY)],
        out_specs=out_tile,
        out_shape=jax.ShapeDtypeStruct(acc.shape, acc.dtype),
        input_output_aliases={len(args) + 1: 0},
        scratch_shapes=list(scratch_shapes),
        compiler_params=pltpu.CompilerParams(
            dimension_semantics=("arbitrary",), vmem_limit_bytes=VMEM_LIMIT),
        name=name,
    )(*args, gate, acc)


def _pair_pipeline(n_pairs, issue, consume):
    nxt = issue(0)
    for p in range(n_pairs):
        cur = nxt
        if p + 1 < n_pairs:
            nxt = issue(p + 1)
        consume(p, cur)


def _moba_select(qm, km, n_past):
    km16 = jnp.concatenate([km, jnp.zeros((BF16_ROWS - N_BLK, LANES), F32)], axis=0).astype(BF16)
    gate = _dot(km16, qm)
    blk = lax.broadcasted_iota(jnp.int32, (BF16_ROWS, BLK), 0)
    past = blk < n_past
    gate = jnp.where(past, gate, NEG_INF)
    rank = jnp.zeros((BF16_ROWS, BLK), F32)
    for jp in range(n_past):
        other = gate[jp:jp + 1, :]
        beats = (other > gate) | ((other == gate) & (blk > jp))
        rank = rank + jnp.where(beats, 1.0, 0.0)
    return jnp.where(past & (rank < MOBA_TOPK), 1.0, 0.0)


def _moba_block(c, qT_ref, k_ref, vT_ref, km_ref, gate_ref, acc_ref, o_ref):
    del acc_ref
    nk = (c + 1) * BLK
    causal = _causal_block_mask_t()
    head_masks = _pair_head_masks()
    gated = c > MOBA_TOPK

    def scores(h):
        p, e = divmod(h, HEADS_PER_PAIR)
        feats = slice(p * LANES, (p + 1) * LANES)
        q2 = qT_ref[feats, :]
        qm = jnp.where(head_masks[e], q2, jnp.zeros_like(q2))
        sel = _moba_select(qm, km_ref[:, feats], c) if gated else None
        return _dot(k_ref[:, feats], qm), sel

    def attend(h, st, sel, shifted):
        parts = []
        for j in range(c):
            blk = st[j * BLK:(j + 1) * BLK]
            if gated:
                blk = jnp.where(sel[j:j + 1, :] > 0.5, blk, NEG_INF)
            parts.append(blk)
        parts.append(jnp.where(causal, st[c * BLK:nk], NEG_INF))
        st = jnp.concatenate(parts, axis=0) if c else parts[0]
        return _softmax_pv_t(st, vT_ref[h * HEAD_DIM:(h + 1) * HEAD_DIM, :], shifted)

    def attend_all(shifted):
        denominators = []

        def consume(p, cur):
            outs = [attend(p * HEADS_PER_PAIR + e, *cur[e], shifted) for e in range(HEADS_PER_PAIR)]
            denominators.extend(l for _, l in outs)
            _store_pair(o_ref, gate_ref, p, [o for o, _ in outs])

        _pair_pipeline(
            N_PAIRS,
            lambda p: [scores(p * HEADS_PER_PAIR + e) for e in range(HEADS_PER_PAIR)],
            consume)
        return denominators

    _attend_guarded(attend_all)


def _moba(mqT, mk, mvT, kmean, gate, zeroed, batch):
    mk3 = mk.reshape(batch, SEQ, ATT_W)
    acc = zeroed.reshape(batch, SEQ, ATT_W)
    for c in range(N_BLK):
        nk = (c + 1) * BLK
        in_specs = [pl.BlockSpec((None, ATT_W, BLK), lambda b, c=c: (b, 0, c)),
                    pl.BlockSpec((None, nk, ATT_W), lambda b: (b, 0, 0)),
                    pl.BlockSpec((None, ATT_W, nk), lambda b: (b, 0, 0)),
                    pl.BlockSpec((None, N_BLK, ATT_W), lambda b: (b, 0, 0))]
        acc = _qblock_call(functools.partial(_moba_block, c), c, f"moba_q{c}", in_specs,
                           (mqT, mk3, mvT, kmean), gate.reshape(batch, SEQ, ATT_W), acc, batch)
    return acc.reshape(batch * SEQ, ATT_W)


def _key_to_f32(key):
    bits = key ^ ((key >> 31) & 0x7FFFFFFF)
    return lax.bitcast_convert_type(bits, F32)


def _count_ge(sc, thr):
    acc = jnp.zeros((CHAIN_ROWS, BLK), F32)
    for r in range(sc.shape[0] // CHAIN_ROWS):
        acc = acc + jnp.where(sc[r * CHAIN_ROWS:(r + 1) * CHAIN_ROWS] >= thr, 1.0, 0.0)
    return jnp.sum(acc, axis=0, keepdims=True)


POS_INF = float("inf")


def _dsa_topk_cap(sc, nk, cap_ref):
    kf = float(DSA_TOPK)
    cnt0 = _count_ge(sc, 0.0)
    nonneg = cnt0 >= kf
    t0 = jnp.where(nonneg, 0, INT_MIN).astype(jnp.int32)
    n0 = jnp.where(nonneg, cnt0, float(nk))

    def bit_step(it, state):
        t, n_ge = state
        cand = t | jnp.left_shift(jnp.int32(1), 30 - it)
        cnt = _count_ge(sc, _key_to_f32(cand))
        take = cnt >= kf
        return jnp.where(take, cand, t), jnp.where(take, cnt, n_ge)

    t, n_ge = lax.fori_loop(0, 31, bit_step, (t0, n0))
    t_val = _key_to_f32(t)
    ties_beyond_k = jnp.max(n_ge) > kf

    @pl.when(jnp.logical_not(ties_beyond_k))
    def _():
        cap_ref[...] = jnp.where(sc >= t_val, POS_INF, NEG_INF)

    @pl.when(ties_beyond_k)
    def _():
        t_next = _key_to_f32(t + 1)
        need = kf - _count_ge(sc, t_next)
        r = lax.broadcasted_iota(jnp.int32, (BLK, BLK), 0)
        c = lax.broadcasted_iota(jnp.int32, (BLK, BLK), 1)
        strict_lower = jnp.where(c < r, 1.0, 0.0).astype(BF16)
        carry = jnp.zeros((1, BLK), F32)
        for j in range(nk // BLK):
            rows = slice(j * BLK, (j + 1) * BLK)
            sc_j = sc[rows]
            gt = sc_j >= t_next
            eq = (sc_j >= t_val) & jnp.logical_not(gt)
            eq_f = jnp.where(eq, 1.0, 0.0)
            before = _dot(strict_lower, eq_f.astype(BF16)) + carry
            carry = carry + _col_reduce(eq_f, jnp.add, jnp.sum)
            cap_ref[rows, :] = jnp.where(gt | (eq & (before < need)), POS_INF, NEG_INF)

    return cap_ref[...]


def _dsa_block(c, iqT_ref, ik_ref, iwT_ref, qT_ref, k_ref, vT_ref, gate_ref, acc_ref, o_ref,
               cap_ref):
    del acc_ref
    nk = (c + 1) * BLK
    causal = _causal_block_mask_t()
    head_masks = _pair_head_masks()

    def masked_q_dots(keys, qT2):
        return [_dot(keys, jnp.where(m, qT2, jnp.zeros_like(qT2))) for m in head_masks]

    if c == 0:
        cap = jnp.where(causal, POS_INF, NEG_INF)
    else:
        wscale = IDX_HEADS ** -0.5
        acc = [jnp.zeros((nk, BLK), F32)]

        def accumulate(p, logits):
            for e in range(HEADS_PER_PAIR):
                w = iwT_ref[p * HEADS_PER_PAIR + e:p * HEADS_PER_PAIR + e + 1, :] * wscale
                acc[0] = acc[0] + jnp.maximum(logits[e], 0.0) * w

        _pair_pipeline(
            N_PAIRS,
            lambda p: masked_q_dots(ik_ref[...], iqT_ref[p * LANES:(p + 1) * LANES, :]),
            accumulate)
        own = jnp.where(causal, acc[0][c * BLK:nk], NEG_INF)
        sc = jnp.concatenate([acc[0][0:c * BLK], own], axis=0)
        cap = _dsa_topk_cap(sc, nk, cap_ref)

    def attend_all(shifted):
        denominators = []

        def consume(p, scores):
            outs = [_softmax_pv_t(
                jnp.minimum(scores[e], cap),
                vT_ref[p * LANES + e * HEAD_DIM:p * LANES + (e + 1) * HEAD_DIM, :], shifted)
                for e in range(HEADS_PER_PAIR)]
            denominators.extend(l for _, l in outs)
            _store_pair(o_ref, gate_ref, p, [o for o, _ in outs])

        _pair_pipeline(
            N_PAIRS,
            lambda p: masked_q_dots(k_ref[:, p * LANES:(p + 1) * LANES],
                                    qT_ref[p * LANES:(p + 1) * LANES, :]),
            consume)
        return denominators

    _attend_guarded(attend_all)


def _dsa(iqT, ik, iwT, dqT, dk, dvT, gate, zeroed, batch):
    ik3 = ik.reshape(batch, SEQ, LANES)
    dk3 = dk.reshape(batch, SEQ, ATT_W)
    acc = zeroed.reshape(batch, SEQ, ATT_W)
    for c in range(N_BLK):
        nk = (c + 1) * BLK
        qtile = lambda r, c=c: pl.BlockSpec((None, r, BLK), lambda b: (b, 0, c))
        keys = lambda w, nk=nk: pl.BlockSpec((None, nk, w), lambda b: (b, 0, 0))
        in_specs = [qtile(ATT_W), keys(LANES), qtile(BF16_ROWS), qtile(ATT_W), keys(ATT_W),
                    pl.BlockSpec((None, ATT_W, nk), lambda b: (b, 0, 0))]
        acc = _qblock_call(functools.partial(_dsa_block, c), c, f"dsa_q{c}", in_specs,
                           (iqT, ik3, iwT, dqT, dk3, dvT), gate.reshape(batch, SEQ, ATT_W), acc,
                           batch, scratch_shapes=[pltpu.VMEM((nk, BLK), F32)])
    return acc.reshape(batch * SEQ, ATT_W)


def _merge_body(x_ref, ya_ref, yb_ref, yc_ref, gates_ref, wa_ref, wb_ref, wc_ref, wo_ref, gf_ref,
                o_ref):
    u = jnp.zeros((PROJ_TILE, D_MODEL), F32)
    for n, (y_ref, w_ref) in enumerate(((ya_ref, wa_ref), (yb_ref, wb_ref), (yc_ref, wc_ref))):
        sl = slice(n * D_MODEL, (n + 1) * D_MODEL)
        u = u + gates_ref[:, sl].astype(F32) * _dot(y_ref[...], w_ref[...])
    y = x_ref[...] + _dot(u.astype(BF16), wo_ref[...])
    o_ref[...] = _rmsnorm(y, gf_ref[...])


def _merge(x2, ya, yb, yc, gates, wa, wb, wc, wo, g_final):
    tile = lambda w: pl.BlockSpec((PROJ_TILE, w), lambda i: (i, 0))
    const = lambda r, w: pl.BlockSpec((r, w), lambda i: (0, 0))
    m = x2.shape[0]
    return pl.pallas_call(
        _merge_body,
        grid=(m // PROJ_TILE,),
        in_specs=[tile(D_MODEL), tile(ATT_W), tile(ATT_W), tile(X_W), tile(N_BRANCH * D_MODEL),
                  const(ATT_W, D_MODEL), const(ATT_W, D_MODEL), const(X_W, D_MODEL),
                  const(D_MODEL, D_MODEL), const(1, D_MODEL)],
        out_specs=tile(D_MODEL),
        out_shape=jax.ShapeDtypeStruct((m, D_MODEL), F32),
        compiler_params=pltpu.CompilerParams(
            dimension_semantics=("arbitrary",), vmem_limit_bytes=VMEM_LIMIT),
        name="merge",
    )(x2, ya, yb, yc, gates, wa, wb, wc, wo, g_final)


def _pack_w_in(w_in):
    offs = [0]
    for s in IN_SIZES:
        offs.append(offs[-1] + s)
    col = lambda n: w_in[:, offs[n]:offs[n + 1]]
    w_cols = jnp.concatenate([col(1), col(5), col(9), col(9), col(3), col(7), col(11), col(12),
                              col(13)], axis=1)
    iw = jnp.pad(col(10), ((0, 0), (0, BF16_ROWS - IDX_HEADS)))
    w_rows = jnp.concatenate([col(0), col(4), col(8), col(2), col(6), iw], axis=1).astype(BF16).T
    return w_cols.astype(BF16), w_rows


def _rope_tables():
    inv = jnp.power(ROPE_THETA, -jnp.arange(HALF, dtype=F32) * 2.0 / HEAD_DIM)
    ang = jnp.arange(SEQ).astype(F32)[:, None] * inv[None, :]
    cos = jnp.cos(ang)
    sin = jnp.sin(ang)
    cos_lanes = jnp.tile(cos, (1, LANES // HALF))
    sin_lanes = jnp.tile(jnp.concatenate([-sin, sin], axis=1), (1, HEADS_PER_PAIR))
    return cos_lanes, sin_lanes, cos.T, sin.T


def _layer(x2, mem2, g_in, w_in, b_merge, g_mem, w_mem_kv, w_up_moba, w_up_dsa, w_up_cross, w_out,
           g_final, batch):
    w_cols, w_rows = _pack_w_in(w_in)
    xk, xv = _mem_proj(mem2, g_mem[None, :], w_mem_kv.astype(BF16))
    (mqT, dqT, iqT, mvT, dvT, iwT, mk, dk, ik, mgate, dgate, yc, gates, kmean, ya0, yb0) = _in_proj(
        x2, g_in[None, :], w_cols, w_rows, _rope_tables(), b_merge[None, :],
        xk.reshape(batch, MEM_LEN, X_W), xv.reshape(batch, MEM_LEN, X_W), batch)
    ya = _moba(mqT, mk, mvT, kmean.reshape(batch, N_BLK, ATT_W), mgate, ya0, batch)
    yb = _dsa(iqT, ik, iwT, dqT, dk, dvT, dgate, yb0, batch)
    return _merge(x2, ya, yb, yc, gates, w_up_moba.astype(BF16), w_up_dsa.astype(BF16),
                  w_up_cross.astype(BF16), w_out.astype(BF16), g_final)


def kernel(x, mem, g_in, w_in, b_merge, g_mem, w_mem_kv, w_up_moba, w_up_dsa, w_up_cross, w_out,
           g_final):
    batch, seq, d = x.shape
    assert seq == SEQ and d == D_MODEL and mem.shape[1] == MEM_LEN
    assert (batch * MEM_LEN) % PROJ_TILE == 0
    assert g_in.shape[0] == 1
    out = _layer(x.reshape(batch * seq, d), mem.reshape(batch * MEM_LEN, d), g_in[0], w_in[0],
                 b_merge[0], g_mem[0], w_mem_kv[0], w_up_moba[0], w_up_dsa[0], w_up_cross[0],
                 w_out[0], g_final[None, :], batch)
    return out.reshape(batch, seq, d)
```

```python
import functools

import jax
import jax.numpy as jnp
from jax import lax
from jax.experimental import pallas as pl
from jax.experimental.pallas import tpu as pltpu

D_MODEL = 1024
SEQ = 2048
HEAD_DIM = 64
HALF = HEAD_DIM // 2
N_HEADS = 8
ATT_W = N_HEADS * HEAD_DIM
BLK = 256
N_BLK = SEQ // BLK
PROJ_TILE = 2 * BLK
MOBA_TOPK = 3
DSA_TOPK = 256
IDX_HEADS = 8
MEM_LEN = 256
X_HEADS = 4
X_HEAD_DIM = 128
X_W = X_HEADS * X_HEAD_DIM
N_BRANCH = 3
ROPE_THETA = 10000.0
RMS_EPS = 1e-6
LANES = 128
HEADS_PER_PAIR = LANES // HEAD_DIM
N_PAIRS = ATT_W // LANES
BF16_ROWS = 16

IN_SIZES = (ATT_W, ATT_W, ATT_W, ATT_W, ATT_W, ATT_W, ATT_W, ATT_W,
            IDX_HEADS * HEAD_DIM, HEAD_DIM, IDX_HEADS, X_W, X_W, N_BRANCH * D_MODEL)

COL_MK, COL_DK, COL_IK = 0, 512, 1024
COL_MG, COL_DG, COL_XQ, COL_XG, COL_GL = 1152, 1664, 2176, 2688, 3200
N_COLS = COL_GL + N_BRANCH * D_MODEL
ROW_MQ, ROW_DQ, ROW_IQ, ROW_MV, ROW_DV, ROW_IW = 0, 512, 1024, 1536, 2048, 2560
N_ROWS = ROW_IW + BF16_ROWS

VMEM_LIMIT = 56 * 1024 * 1024

F32 = jnp.float32
BF16 = jnp.bfloat16
NEG_INF = float("-inf")
INT_MIN = -2 ** 31
LOG2E = 1.4426950408889634


def _dot(a, b):
    return jnp.dot(a, b, preferred_element_type=F32)


def _nt_dot(a, b):
    return lax.dot_general(a, b, (((1,), (1,)), ((), ())), preferred_element_type=F32)


def _rmsnorm(x, g):
    ms = jnp.mean(x * x, axis=-1, keepdims=True)
    return (x * lax.rsqrt(ms + RMS_EPS)) * g


def _softmax_pv(s, v):
    m = jnp.max(s, axis=1, keepdims=True)
    p = jnp.exp2(s - m)
    l = jnp.sum(p, axis=1, keepdims=True)
    return _dot(p.astype(BF16), v) / l


def _in_proj_body(x_ref, g_ref, wc_ref, wr_ref, cos_ref, sin_ref, cos_t_ref, sin_t_ref,
                  bm_ref, xk_ref, xv_ref,
                  mqT_ref, dqT_ref, iqT_ref, mvT_ref, dvT_ref, iwT_ref,
                  mk_ref, dk_ref, ik_ref, mg_ref, dg_ref, yc_ref, gl_ref, km_ref, ya0_ref, yb0_ref):
    ya0_ref[...] = jnp.zeros(ya0_ref.shape, ya0_ref.dtype)
    yb0_ref[...] = jnp.zeros(yb0_ref.shape, yb0_ref.dtype)
    h = _rmsnorm(x_ref[...], g_ref[...]).astype(BF16)
    qk_scale = HEAD_DIM ** -0.5
    att_scale = qk_scale * LOG2E

    def seg(off, width):
        return lambda: _dot(h, wc_ref[:, off:off + width])

    kept = {}

    def keep_silu_xg(z):
        kept["xg"] = jax.nn.silu(z)

    def cross_attention(z):
        xq = z.astype(BF16)
        xscale = X_HEAD_DIM ** -0.5 * LOG2E
        yc_parts = []
        for hd in range(X_HEADS):
            sl = slice(hd * X_HEAD_DIM, (hd + 1) * X_HEAD_DIM)
            s = _nt_dot(xq[:, sl], xk_ref[:, sl]) * xscale
            yc_parts.append(_softmax_pv(s, xv_ref[:, sl]))
        yc_ref[...] = (jnp.concatenate(yc_parts, axis=1) * kept["xg"]).astype(BF16)

    def store_merge_gate(c):
        sl = slice(c * D_MODEL, (c + 1) * D_MODEL)

        def epilogue(z):
            gl_ref[:, sl] = jax.nn.sigmoid(z + bm_ref[:, sl]).astype(BF16)
        return epilogue

    def store_silu(ref):
        def epilogue(z):
            ref[...] = jax.nn.silu(z).astype(BF16)
        return epilogue

    def store_feature_major(zt):
        cos_t = cos_t_ref[...]
        sin_t = sin_t_ref[...]
        for ref, row0, scale in ((mqT_ref, ROW_MQ, att_scale), (dqT_ref, ROW_DQ, att_scale),
                                 (iqT_ref, ROW_IQ, qk_scale)):
            for hd in range(N_HEADS):
                r = row0 + hd * HEAD_DIM
                x1 = zt[r:r + HALF]
                x2 = zt[r + HALF:r + HEAD_DIM]
                o = hd * HEAD_DIM
                ref[o:o + HALF, :] = ((x1 * cos_t - x2 * sin_t) * scale).astype(BF16)
                ref[o + HALF:o + HEAD_DIM, :] = ((x2 * cos_t + x1 * sin_t) * scale).astype(BF16)
        mvT_ref[...] = zt[ROW_MV:ROW_MV + ATT_W].astype(BF16)
        dvT_ref[...] = zt[ROW_DV:ROW_DV + ATT_W].astype(BF16)
        iwT_ref[...] = zt[ROW_IW:ROW_IW + BF16_ROWS]

    def store_roped(ref, mean_ref=None):
        def epilogue(z):
            cos = cos_ref[...]
            sin = sin_ref[...]
            lane = lax.broadcasted_iota(jnp.int32, (PROJ_TILE, LANES), 1)
            first_half = (lane & HALF) == 0
            for c in range(z.shape[1] // LANES):
                lanes = slice(c * LANES, (c + 1) * LANES)
                zc = z[:, lanes]
                partner = jnp.where(first_half, pltpu.roll(zc, LANES - HALF, 1),
                                    pltpu.roll(zc, HALF, 1))
                r = zc * cos + partner * sin
                if mean_ref is not None:
                    for blk in range(PROJ_TILE // BLK):
                        mean_ref[blk, :, lanes] = jnp.mean(r[blk * BLK:(blk + 1) * BLK], axis=0,
                                                           keepdims=True)
                ref[:, lanes] = r.astype(ref.dtype)
        return epilogue

    stages = [(seg(COL_XG, X_W), keep_silu_xg), (seg(COL_XQ, X_W), cross_attention)]
    stages += [(seg(COL_GL + c * D_MODEL, D_MODEL), store_merge_gate(c)) for c in range(N_BRANCH)]
    stages += [(seg(COL_MG, ATT_W), store_silu(mg_ref)), (seg(COL_DG, ATT_W), store_silu(dg_ref)),
               (lambda: _nt_dot(wr_ref[...], h), store_feature_major),
               (seg(COL_MK, ATT_W), store_roped(mk_ref, mean_ref=km_ref)),
               (seg(COL_DK, ATT_W), store_roped(dk_ref)),
               (seg(COL_IK, LANES), store_roped(ik_ref))]
    nxt = stages[0][0]()
    for i, (_, epilogue) in enumerate(stages):
        cur = nxt
        if i + 1 < len(stages):
            nxt = stages[i + 1][0]()
        epilogue(cur)


def _in_proj(x2, g_in, w_cols, w_rows, tables, b_merge, xk, xv, batch):
    m = x2.shape[0]
    n_tiles = m // PROJ_TILE
    per_seq = SEQ // PROJ_TILE
    row = lambda w: pl.BlockSpec((PROJ_TILE, w), lambda i: (i, 0))
    feat = lambda r: pl.BlockSpec((None, r, PROJ_TILE), lambda i: (i // per_seq, 0, i % per_seq))
    const = lambda shape: pl.BlockSpec(shape, lambda i: (0, 0), pipeline_mode=pl.Buffered(1))
    tm = lambda w, dt: jax.ShapeDtypeStruct((m, w), dt)
    fm = lambda r, dt: jax.ShapeDtypeStruct((batch, r, SEQ), dt)
    out_shape = [
        fm(ATT_W, BF16), fm(ATT_W, BF16), fm(ATT_W, BF16),
        fm(ATT_W, BF16), fm(ATT_W, BF16), fm(BF16_ROWS, F32),
        tm(ATT_W, BF16), tm(ATT_W, BF16), tm(LANES, BF16),
        tm(ATT_W, BF16), tm(ATT_W, BF16),
        tm(X_W, BF16),
        tm(N_BRANCH * D_MODEL, BF16),
        jax.ShapeDtypeStruct((m // BLK, 1, ATT_W), F32),
        tm(ATT_W, BF16), tm(ATT_W, BF16),
    ]
    out_specs = [feat(ATT_W)] * 5 + [feat(BF16_ROWS)] + [
        row(ATT_W), row(ATT_W), row(LANES), row(ATT_W), row(ATT_W), row(X_W),
        row(N_BRANCH * D_MODEL),
        pl.BlockSpec((PROJ_TILE // BLK, 1, ATT_W), lambda i: (i, 0, 0)),
        row(ATT_W), row(ATT_W)]
    mem = pl.BlockSpec((None, MEM_LEN, X_W), lambda i: (i // per_seq, 0, 0))
    cos, sin, cos_t, sin_t = tables
    return pl.pallas_call(
        _in_proj_body,
        grid=(n_tiles,),
        in_specs=[
            pl.BlockSpec((PROJ_TILE, D_MODEL), lambda i: (i, 0)),
            pl.BlockSpec((1, D_MODEL), lambda i: (0, 0)),
            const((D_MODEL, N_COLS)),
            const((N_ROWS, D_MODEL)),
            pl.BlockSpec((PROJ_TILE, LANES), lambda i: (i % per_seq, 0)),
            pl.BlockSpec((PROJ_TILE, LANES), lambda i: (i % per_seq, 0)),
            pl.BlockSpec((HALF, PROJ_TILE), lambda i: (0, i % per_seq)),
            pl.BlockSpec((HALF, PROJ_TILE), lambda i: (0, i % per_seq)),
            pl.BlockSpec((1, N_BRANCH * D_MODEL), lambda i: (0, 0)),
            mem, mem,
        ],
        out_specs=out_specs,
        out_shape=out_shape,
        compiler_params=pltpu.CompilerParams(
            dimension_semantics=("arbitrary",), vmem_limit_bytes=VMEM_LIMIT),
        name="in_proj",
    )(x2, g_in, w_cols, w_rows, cos, sin, cos_t, sin_t, b_merge, xk, xv)


def _mem_proj_body(m_ref, g_ref, w_ref, k_ref, v_ref):
    h = _rmsnorm(m_ref[...], g_ref[...]).astype(BF16)
    k_ref[...] = _dot(h, w_ref[:, 0:X_W]).astype(BF16)
    v_ref[...] = _dot(h, w_ref[:, X_W:2 * X_W]).astype(BF16)


def _mem_proj(mem2, g_mem, w_kv):
    m = mem2.shape[0]
    return pl.pallas_call(
        _mem_proj_body,
        grid=(m // PROJ_TILE,),
        in_specs=[
            pl.BlockSpec((PROJ_TILE, D_MODEL), lambda i: (i, 0)),
            pl.BlockSpec((1, D_MODEL), lambda i: (0, 0)),
            pl.BlockSpec((D_MODEL, 2 * X_W), lambda i: (0, 0)),
        ],
        out_specs=[pl.BlockSpec((PROJ_TILE, X_W), lambda i: (i, 0))] * 2,
        out_shape=[jax.ShapeDtypeStruct((m, X_W), BF16)] * 2,
        compiler_params=pltpu.CompilerParams(
            dimension_semantics=("arbitrary",), vmem_limit_bytes=VMEM_LIMIT),
        name="mem_proj",
    )(mem2, g_mem, w_kv)


def _pair_head_masks():
    r = lax.broadcasted_iota(jnp.int32, (LANES, BLK), 0)
    return [(r >= e * HEAD_DIM) & (r < (e + 1) * HEAD_DIM) for e in range(HEADS_PER_PAIR)]


def _causal_block_mask_t():
    kr = lax.broadcasted_iota(jnp.int32, (BLK, BLK), 0)
    qc = lax.broadcasted_iota(jnp.int32, (BLK, BLK), 1)
    return kr <= qc


CHAIN_ROWS = 64


def _col_reduce(x, combine, finish):
    rows = x.shape[0]
    acc = x[0:CHAIN_ROWS]
    for r in range(1, rows // CHAIN_ROWS):
        acc = combine(acc, x[r * CHAIN_ROWS:(r + 1) * CHAIN_ROWS])
    return finish(acc, axis=0, keepdims=True)


SAFE_EXP = 100


def _softmax_pv_t(st, vt, shifted):
    if shifted:
        st = st - _col_reduce(st, jnp.maximum, jnp.max)
    p = jnp.exp2(st).astype(BF16)
    ones = jnp.ones((BF16_ROWS, vt.shape[1]), BF16)
    out = _dot(jnp.concatenate([vt, ones], axis=0), p)
    l = out[HEAD_DIM:HEAD_DIM + 1]
    return out[0:HEAD_DIM] / l, l


def _denominators_safe(ls):
    ok = None
    for l in ls:
        good = (l > 2.0 ** -SAFE_EXP) & (l < 2.0 ** SAFE_EXP)
        ok = good if ok is None else ok & good
    return jnp.min(jnp.where(ok, 1.0, 0.0))


def _attend_guarded(attend_all):
    safe = _denominators_safe(attend_all(False))

    @pl.when(safe < 0.5)
    def _():
        attend_all(True)


def _store_pair(o_ref, gate_ref, p, halves):
    lanes = slice(p * LANES, (p + 1) * LANES)
    y = jnp.concatenate(halves, axis=0).T
    o_ref[:, lanes] = (y * gate_ref[:, lanes].astype(F32)).astype(o_ref.dtype)


def _pair_pipeline(n_pairs, issue, consume):
    nxt = issue(0)
    for p in range(n_pairs):
        cur = nxt
        if p + 1 < n_pairs:
            nxt = issue(p + 1)
        consume(p, cur)


def _moba_select(qm, km, n_past):
    km16 = jnp.concatenate([km, jnp.zeros((BF16_ROWS - N_BLK, LANES), F32)], axis=0).astype(BF16)
    gate = _dot(km16, qm)
    blk = lax.broadcasted_iota(jnp.int32, (BF16_ROWS, BLK), 0)
    past = blk < n_past
    gate = jnp.where(past, gate, NEG_INF)
    rank = jnp.zeros((BF16_ROWS, BLK), F32)
    for jp in range(n_past):
        other = gate[jp:jp + 1, :]
        beats = (other > gate) | ((other == gate) & (blk > jp))
        rank = rank + jnp.where(beats, 1.0, 0.0)
    return jnp.where(past & (rank < MOBA_TOPK), 1.0, 0.0)


def _moba_block(c, qT_ref, k_ref, vT_ref, km_ref, gate_ref, acc_ref, o_ref):
    del acc_ref
    nk = (c + 1) * BLK
    causal = _causal_block_mask_t()
    head_masks = _pair_head_masks()
    gated = c > MOBA_TOPK

    def scores(h):
        p, e = divmod(h, HEADS_PER_PAIR)
        feats = slice(p * LANES, (p + 1) * LANES)
        q2 = qT_ref[feats, :]
        qm = jnp.where(head_masks[e], q2, jnp.zeros_like(q2))
        sel = _moba_select(qm, km_ref[:, feats], c) if gated else None
        return _dot(k_ref[:, feats], qm), sel

    def attend(h, st, sel, shifted):
        parts = []
        for j in range(c):
            blk = st[j * BLK:(j + 1) * BLK]
            if gated:
                blk = jnp.where(sel[j:j + 1, :] > 0.5, blk, NEG_INF)
            parts.append(blk)
        parts.append(jnp.where(causal, st[c * BLK:nk], NEG_INF))
        st = jnp.concatenate(parts, axis=0) if c else parts[0]
        return _softmax_pv_t(st, vT_ref[h * HEAD_DIM:(h + 1) * HEAD_DIM, :], shifted)

    def attend_all(shifted):
        denominators = []

        def consume(p, cur):
            outs = [attend(p * HEADS_PER_PAIR + e, *cur[e], shifted) for e in range(HEADS_PER_PAIR)]
            denominators.extend(l for _, l in outs)
            _store_pair(o_ref, gate_ref, p, [o for o, _ in outs])

        _pair_pipeline(
            N_PAIRS,
            lambda p: [scores(p * HEADS_PER_PAIR + e) for e in range(HEADS_PER_PAIR)],
            consume)
        return denominators

    _attend_guarded(attend_all)


def _key_to_f32(key):
    bits = key ^ ((key >> 31) & 0x7FFFFFFF)
    return lax.bitcast_convert_type(bits, F32)


def _count_ge(sc, thr):
    acc = jnp.zeros((CHAIN_ROWS, BLK), F32)
    for r in range(sc.shape[0] // CHAIN_ROWS):
        acc = acc + jnp.where(sc[r * CHAIN_ROWS:(r + 1) * CHAIN_ROWS] >= thr, 1.0, 0.0)
    return jnp.sum(acc, axis=0, keepdims=True)


POS_INF = float("inf")


def _dsa_topk_cap(sc, nk, cap_ref):
    kf = float(DSA_TOPK)
    cnt0 = _count_ge(sc, 0.0)
    nonneg = cnt0 >= kf
    t0 = jnp.where(nonneg, 0, INT_MIN).astype(jnp.int32)
    n0 = jnp.where(nonneg, cnt0, float(nk))

    def bit_step(it, state):
        t, n_ge = state
        cand = t | jnp.left_shift(jnp.int32(1), 30 - it)
        cnt = _count_ge(sc, _key_to_f32(cand))
        take = cnt >= kf
        return jnp.where(take, cand, t), jnp.where(take, cnt, n_ge)

    t, n_ge = lax.fori_loop(0, 31, bit_step, (t0, n0))
    t_val = _key_to_f32(t)
    ties_beyond_k = jnp.max(n_ge) > kf

    @pl.when(jnp.logical_not(ties_beyond_k))
    def _():
        cap_ref[...] = jnp.where(sc >= t_val, POS_INF, NEG_INF)

    @pl.when(ties_beyond_k)
    def _():
        t_next = _key_to_f32(t + 1)
        need = kf - _count_ge(sc, t_next)
        r = lax.broadcasted_iota(jnp.int32, (BLK, BLK), 0)
        c = lax.broadcasted_iota(jnp.int32, (BLK, BLK), 1)
        strict_lower = jnp.where(c < r, 1.0, 0.0).astype(BF16)
        carry = jnp.zeros((1, BLK), F32)
        for j in range(nk // BLK):
            rows = slice(j * BLK, (j + 1) * BLK)
            sc_j = sc[rows]
            gt = sc_j >= t_next
            eq = (sc_j >= t_val) & jnp.logical_not(gt)
            eq_f = jnp.where(eq, 1.0, 0.0)
            before = _dot(strict_lower, eq_f.astype(BF16)) + carry
            carry = carry + _col_reduce(eq_f, jnp.add, jnp.sum)
            cap_ref[rows, :] = jnp.where(gt | (eq & (before < need)), POS_INF, NEG_INF)

    return cap_ref[...]


def _dsa_block(c, iqT_ref, ik_ref, iwT_ref, qT_ref, k_ref, vT_ref, gate_ref, acc_ref, o_ref,
               cap_ref):
    del acc_ref
    nk = (c + 1) * BLK
    causal = _causal_block_mask_t()
    head_masks = _pair_head_masks()

    def masked_q_dots(keys, qT2):
        return [_dot(keys, jnp.where(m, qT2, jnp.zeros_like(qT2))) for m in head_masks]

    if c == 0:
        cap = jnp.where(causal, POS_INF, NEG_INF)
    else:
        wscale = IDX_HEADS ** -0.5
        acc = [jnp.zeros((nk, BLK), F32)]

        def accumulate(p, logits):
            for e in range(HEADS_PER_PAIR):
                w = iwT_ref[p * HEADS_PER_PAIR + e:p * HEADS_PER_PAIR + e + 1, :] * wscale
                acc[0] = acc[0] + jnp.maximum(logits[e], 0.0) * w

        _pair_pipeline(
            N_PAIRS,
            lambda p: masked_q_dots(ik_ref[...], iqT_ref[p * LANES:(p + 1) * LANES, :]),
            accumulate)
        own = jnp.where(causal, acc[0][c * BLK:nk], NEG_INF)
        sc = jnp.concatenate([acc[0][0:c * BLK], own], axis=0)
        cap = _dsa_topk_cap(sc, nk, cap_ref)

    def attend_all(shifted):
        denominators = []

        def consume(p, scores):
            outs = [_softmax_pv_t(
                jnp.minimum(scores[e], cap),
                vT_ref[p * LANES + e * HEAD_DIM:p * LANES + (e + 1) * HEAD_DIM, :], shifted)
                for e in range(HEADS_PER_PAIR)]
            denominators.extend(l for _, l in outs)
            _store_pair(o_ref, gate_ref, p, [o for o, _ in outs])

        _pair_pipeline(
            N_PAIRS,
            lambda p: masked_q_dots(k_ref[:, p * LANES:(p + 1) * LANES],
                                    qT_ref[p * LANES:(p + 1) * LANES, :]),
            consume)
        return denominators

    _attend_guarded(attend_all)


def _mixers_block(c, mqT_ref, mk_ref, mvT_ref, km_ref, mgate_ref, iqT_ref, ik_ref, iwT_ref, dqT_ref,
                  dk_ref, dvT_ref, dgate_ref, macc_ref, dacc_ref, mo_ref, do_ref, cap_ref):
    _moba_block(c, mqT_ref, mk_ref, mvT_ref, km_ref, mgate_ref, macc_ref, mo_ref)
    _dsa_block(c, iqT_ref, ik_ref, iwT_ref, dqT_ref, dk_ref, dvT_ref, dgate_ref, dacc_ref, do_ref,
               cap_ref)


def _mixers(mqT, mk, mvT, kmean, mgate, iqT, ik, iwT, dqT, dk, dvT, dgate, m_zeroed, d_zeroed, batch):
    seq3 = lambda a, w: a.reshape(batch, SEQ, w)
    macc, dacc = seq3(m_zeroed, ATT_W), seq3(d_zeroed, ATT_W)
    args = (mqT, seq3(mk, ATT_W), mvT, kmean, seq3(mgate, ATT_W),
            iqT, seq3(ik, LANES), iwT, dqT, seq3(dk, ATT_W), dvT, seq3(dgate, ATT_W))
    for c in range(N_BLK):
        nk = (c + 1) * BLK
        qtile = lambda r, c=c: pl.BlockSpec((None, r, BLK), lambda b: (b, 0, c))
        keys = lambda w, nk=nk: pl.BlockSpec((None, nk, w), lambda b: (b, 0, 0))
        values = pl.BlockSpec((None, ATT_W, nk), lambda b: (b, 0, 0))
        out_tile = pl.BlockSpec((None, BLK, ATT_W), lambda b, c=c: (b, c, 0))
        acc_any = pl.BlockSpec(memory_space=pl.ANY)
        in_specs = [qtile(ATT_W), keys(ATT_W), values,
                    pl.BlockSpec((None, N_BLK, ATT_W), lambda b: (b, 0, 0)), out_tile,
                    qtile(ATT_W), keys(LANES), qtile(BF16_ROWS), qtile(ATT_W), keys(ATT_W), values,
                    out_tile, acc_any, acc_any]
        macc, dacc = pl.pallas_call(
            functools.partial(_mixers_block, c),
            grid=(batch,),
            in_specs=in_specs,
            out_specs=[out_tile, out_tile],
            out_shape=[jax.ShapeDtypeStruct(macc.shape, macc.dtype)] * 2,
            input_output_aliases={len(args): 0, len(args) + 1: 1},
            scratch_shapes=[pltpu.VMEM((nk, BLK), F32)],
            compiler_params=pltpu.CompilerParams(
                dimension_semantics=("arbitrary",), vmem_limit_bytes=VMEM_LIMIT),
            name=f"mixers_q{c}",
        )(*args, macc, dacc)
    return macc.reshape(batch * SEQ, ATT_W), dacc.reshape(batch * SEQ, ATT_W)


def _merge_body(x_ref, ya_ref, yb_ref, yc_ref, gates_ref, wa_ref, wb_ref, wc_ref, wo_ref, gf_ref,
                o_ref):
    u = jnp.zeros((PROJ_TILE, D_MODEL), F32)
    for n, (y_ref, w_ref) in enumerate(((ya_ref, wa_ref), (yb_ref, wb_ref), (yc_ref, wc_ref))):
        sl = slice(n * D_MODEL, (n + 1) * D_MODEL)
        u = u + gates_ref[:, sl].astype(F32) * _dot(y_ref[...], w_ref[...])
    y = x_ref[...] + _dot(u.astype(BF16), wo_ref[...])
    o_ref[...] = _rmsnorm(y, gf_ref[...])


def _merge(x2, ya, yb, yc, gates, wa, wb, wc, wo, g_final):
    tile = lambda w: pl.BlockSpec((PROJ_TILE, w), lambda i: (i, 0))
    const = lambda r, w: pl.BlockSpec((r, w), lambda i: (0, 0))
    m = x2.shape[0]
    return pl.pallas_call(
        _merge_body,
        grid=(m // PROJ_TILE,),
        in_specs=[tile(D_MODEL), tile(ATT_W), tile(ATT_W), tile(X_W), tile(N_BRANCH * D_MODEL),
                  const(ATT_W, D_MODEL), const(ATT_W, D_MODEL), const(X_W, D_MODEL),
                  const(D_MODEL, D_MODEL), const(1, D_MODEL)],
        out_specs=tile(D_MODEL),
        out_shape=jax.ShapeDtypeStruct((m, D_MODEL), F32),
        compiler_params=pltpu.CompilerParams(
            dimension_semantics=("arbitrary",), vmem_limit_bytes=VMEM_LIMIT),
        name="merge",
    )(x2, ya, yb, yc, gates, wa, wb, wc, wo, g_final)


def _pack_w_in(w_in):
    offs = [0]
    for s in IN_SIZES:
        offs.append(offs[-1] + s)
    col = lambda n: w_in[:, offs[n]:offs[n + 1]]
    w_cols = jnp.concatenate([col(1), col(5), col(9), col(9), col(3), col(7), col(11), col(12),
                              col(13)], axis=1)
    iw = jnp.pad(col(10), ((0, 0), (0, BF16_ROWS - IDX_HEADS)))
    w_rows = jnp.concatenate([col(0), col(4), col(8), col(2), col(6), iw], axis=1).astype(BF16).T
    return w_cols.astype(BF16), w_rows


def _rope_tables():
    inv = jnp.power(ROPE_THETA, -jnp.arange(HALF, dtype=F32) * 2.0 / HEAD_DIM)
    ang = jnp.arange(SEQ).astype(F32)[:, None] * inv[None, :]
    cos = jnp.cos(ang)
    sin = jnp.sin(ang)
    cos_lanes = jnp.tile(cos, (1, LANES // HALF))
    sin_lanes = jnp.tile(jnp.concatenate([-sin, sin], axis=1), (1, HEADS_PER_PAIR))
    return cos_lanes, sin_lanes, cos.T, sin.T


def _layer(x2, mem2, g_in, w_in, b_merge, g_mem, w_mem_kv, w_up_moba, w_up_dsa, w_up_cross, w_out,
           g_final, batch):
    w_cols, w_rows = _pack_w_in(w_in)
    xk, xv = _mem_proj(mem2, g_mem[None, :], w_mem_kv.astype(BF16))
    (mqT, dqT, iqT, mvT, dvT, iwT, mk, dk, ik, mgate, dgate, yc, gates, kmean, ya0, yb0) = _in_proj(
        x2, g_in[None, :], w_cols, w_rows, _rope_tables(), b_merge[None, :],
        xk.reshape(batch, MEM_LEN, X_W), xv.reshape(batch, MEM_LEN, X_W), batch)
    ya, yb = _mixers(mqT, mk, mvT, kmean.reshape(batch, N_BLK, ATT_W), mgate, iqT, ik, iwT, dqT, dk,
                     dvT, dgate, ya0, yb0, batch)
    return _merge(x2, ya, yb, yc, gates, w_up_moba.astype(BF16), w_up_dsa.astype(BF16),
                  w_up_cross.astype(BF16), w_out.astype(BF16), g_final)


def kernel(x, mem, g_in, w_in, b_merge, g_mem, w_mem_kv, w_up_moba, w_up_dsa, w_up_cross, w_out,
           g_final):
    batch, seq, d = x.shape
    assert seq == SEQ and d == D_MODEL and mem.shape[1] == MEM_LEN
    assert (batch * MEM_LEN) % PROJ_TILE == 0
    assert g_in.shape[0] == 1
    out = _layer(x.reshape(batch * seq, d), mem.reshape(batch * MEM_LEN, d), g_in[0], w_in[0],
                 b_merge[0], g_mem[0], w_mem_kv[0], w_up_moba[0], w_up_dsa[0], w_up_cross[0],
                 w_out[0], g_final[None, :], batch)
    return out.reshape(batch, seq, d)
```

```python
import functools

import jax
import jax.numpy as jnp
from jax import lax
from jax.experimental import pallas as pl
from jax.experimental.pallas import tpu as pltpu

D_MODEL = 1024
SEQ = 2048
HEAD_DIM = 64
HALF = HEAD_DIM // 2
N_HEADS = 8
ATT_W = N_HEADS * HEAD_DIM
BLK = 256
N_BLK = SEQ // BLK
PROJ_TILE = 2 * BLK
MOBA_TOPK = 3
DSA_TOPK = 256
IDX_HEADS = 8
MEM_LEN = 256
X_HEADS = 4
X_HEAD_DIM = 128
X_W = X_HEADS * X_HEAD_DIM
N_BRANCH = 3
ROPE_THETA = 10000.0
RMS_EPS = 1e-6
LANES = 128
HEADS_PER_PAIR = LANES // HEAD_DIM
N_PAIRS = ATT_W // LANES
BF16_ROWS = 16

IN_SIZES = (ATT_W, ATT_W, ATT_W, ATT_W, ATT_W, ATT_W, ATT_W, ATT_W,
            IDX_HEADS * HEAD_DIM, HEAD_DIM, IDX_HEADS, X_W, X_W, N_BRANCH * D_MODEL)

COL_MK, COL_DK, COL_IK = 0, 512, 1024
COL_MG, COL_DG, COL_XQ, COL_XG, COL_GL = 1152, 1664, 2176, 2688, 3200
N_COLS = COL_GL + N_BRANCH * D_MODEL
ROW_MQ, ROW_DQ, ROW_IQ, ROW_MV, ROW_DV, ROW_IW = 0, 512, 1024, 1536, 2048, 2560
N_ROWS = ROW_IW + BF16_ROWS

VMEM_LIMIT = 56 * 1024 * 1024

F32 = jnp.float32
BF16 = jnp.bfloat16
NEG_INF = float("-inf")
INT_MIN = -2 ** 31
LOG2E = 1.4426950408889634


def _dot(a, b):
    return jnp.dot(a, b, preferred_element_type=F32)


def _nt_dot(a, b):
    return lax.dot_general(a, b, (((1,), (1,)), ((), ())), preferred_element_type=F32)


def _rmsnorm(x, g):
    ms = jnp.mean(x * x, axis=-1, keepdims=True)
    return (x * lax.rsqrt(ms + RMS_EPS)) * g


def _softmax_pv(s, v):
    m = jnp.max(s, axis=1, keepdims=True)
    p = jnp.exp2(s - m)
    l = jnp.sum(p, axis=1, keepdims=True)
    return _dot(p.astype(BF16), v) / l


def _in_proj_body(x_ref, g_ref, wc_ref, wr_ref, cos_ref, sin_ref, cos_t_ref, sin_t_ref,
                  bm_ref, xk_ref, xv_ref,
                  mqT_ref, dqT_ref, iqT_ref, mvT_ref, dvT_ref, iwT_ref,
                  mk_ref, dk_ref, ik_ref, mg_ref, dg_ref, yc_ref, gl_ref, km_ref, ya0_ref, yb0_ref):
    ya0_ref[...] = jnp.zeros(ya0_ref.shape, ya0_ref.dtype)
    yb0_ref[...] = jnp.zeros(yb0_ref.shape, yb0_ref.dtype)
    h = _rmsnorm(x_ref[...], g_ref[...]).astype(BF16)
    qk_scale = HEAD_DIM ** -0.5
    att_scale = qk_scale * LOG2E

    def seg(off, width):
        return lambda: _dot(h, wc_ref[:, off:off + width])

    kept = {}

    def keep_silu_xg(z):
        kept["xg"] = jax.nn.silu(z)

    def cross_attention(z):
        xq = z.astype(BF16)
        xscale = X_HEAD_DIM ** -0.5 * LOG2E
        yc_parts = []
        for hd in range(X_HEADS):
            sl = slice(hd * X_HEAD_DIM, (hd + 1) * X_HEAD_DIM)
            s = _nt_dot(xq[:, sl], xk_ref[:, sl]) * xscale
            yc_parts.append(_softmax_pv(s, xv_ref[:, sl]))
        yc_ref[...] = (jnp.concatenate(yc_parts, axis=1) * kept["xg"]).astype(BF16)

    def store_merge_gate(c):
        sl = slice(c * D_MODEL, (c + 1) * D_MODEL)

        def epilogue(z):
            gl_ref[:, sl] = jax.nn.sigmoid(z + bm_ref[:, sl]).astype(BF16)
        return epilogue

    def store_silu(ref):
        def epilogue(z):
            ref[...] = jax.nn.silu(z).astype(BF16)
        return epilogue

    def store_feature_major(zt):
        cos_t = cos_t_ref[...]
        sin_t = sin_t_ref[...]
        for ref, row0, scale in ((mqT_ref, ROW_MQ, att_scale), (dqT_ref, ROW_DQ, att_scale),
                                 (iqT_ref, ROW_IQ, qk_scale)):
            for hd in range(N_HEADS):
                r = row0 + hd * HEAD_DIM
                x1 = zt[r:r + HALF]
                x2 = zt[r + HALF:r + HEAD_DIM]
                o = hd * HEAD_DIM
                ref[o:o + HALF, :] = ((x1 * cos_t - x2 * sin_t) * scale).astype(BF16)
                ref[o + HALF:o + HEAD_DIM, :] = ((x2 * cos_t + x1 * sin_t) * scale).astype(BF16)
        mvT_ref[...] = zt[ROW_MV:ROW_MV + ATT_W].astype(BF16)
        dvT_ref[...] = zt[ROW_DV:ROW_DV + ATT_W].astype(BF16)
        iwT_ref[...] = zt[ROW_IW:ROW_IW + BF16_ROWS]

    def store_roped(ref, mean_ref=None):
        def epilogue(z):
            cos = cos_ref[...]
            sin = sin_ref[...]
            lane = lax.broadcasted_iota(jnp.int32, (PROJ_TILE, LANES), 1)
            first_half = (lane & HALF) == 0
            for c in range(z.shape[1] // LANES):
                lanes = slice(c * LANES, (c + 1) * LANES)
                zc = z[:, lanes]
                partner = jnp.where(first_half, pltpu.roll(zc, LANES - HALF, 1),
                                    pltpu.roll(zc, HALF, 1))
                r = zc * cos + partner * sin
                if mean_ref is not None:
                    for blk in range(PROJ_TILE // BLK):
                        mean_ref[blk, :, lanes] = jnp.mean(r[blk * BLK:(blk + 1) * BLK], axis=0,
                                                           keepdims=True)
                ref[:, lanes] = r.astype(ref.dtype)
        return epilogue

    stages = [(seg(COL_XG, X_W), keep_silu_xg), (seg(COL_XQ, X_W), cross_attention)]
    stages += [(seg(COL_GL + c * D_MODEL, D_MODEL), store_merge_gate(c)) for c in range(N_BRANCH)]
    stages += [(seg(COL_MG, ATT_W), store_silu(mg_ref)), (seg(COL_DG, ATT_W), store_silu(dg_ref)),
               (lambda: _nt_dot(wr_ref[...], h), store_feature_major),
               (seg(COL_MK, ATT_W), store_roped(mk_ref, mean_ref=km_ref)),
               (seg(COL_DK, ATT_W), store_roped(dk_ref)),
               (seg(COL_IK, LANES), store_roped(ik_ref))]
    nxt = stages[0][0]()
    for i, (_, epilogue) in enumerate(stages):
        cur = nxt
        if i + 1 < len(stages):
            nxt = stages[i + 1][0]()
        epilogue(cur)


def _in_proj(x2, g_in, w_cols, w_rows, tables, b_merge, xk, xv, batch):
    m = x2.shape[0]
    n_tiles = m // PROJ_TILE
    per_seq = SEQ // PROJ_TILE
    row = lambda w: pl.BlockSpec((PROJ_TILE, w), lambda i: (i, 0))
    feat = lambda r: pl.BlockSpec((None, r, PROJ_TILE), lambda i: (i // per_seq, 0, i % per_seq))
    const = lambda shape: pl.BlockSpec(shape, lambda i: (0, 0), pipeline_mode=pl.Buffered(1))
    tm = lambda w, dt: jax.ShapeDtypeStruct((m, w), dt)
    fm = lambda r, dt: jax.ShapeDtypeStruct((batch, r, SEQ), dt)
    out_shape = [
        fm(ATT_W, BF16), fm(ATT_W, BF16), fm(ATT_W, BF16),
        fm(ATT_W, BF16), fm(ATT_W, BF16), fm(BF16_ROWS, F32),
        tm(ATT_W, BF16), tm(ATT_W, BF16), tm(LANES, BF16),
        tm(ATT_W, BF16), tm(ATT_W, BF16),
        tm(X_W, BF16),
        tm(N_BRANCH * D_MODEL, BF16),
        jax.ShapeDtypeStruct((m // BLK, 1, ATT_W), F32),
        tm(ATT_W, BF16), tm(ATT_W, BF16),
    ]
    out_specs = [feat(ATT_W)] * 5 + [feat(BF16_ROWS)] + [
        row(ATT_W), row(ATT_W), row(LANES), row(ATT_W), row(ATT_W), row(X_W),
        row(N_BRANCH * D_MODEL),
        pl.BlockSpec((PROJ_TILE // BLK, 1, ATT_W), lambda i: (i, 0, 0)),
        row(ATT_W), row(ATT_W)]
    mem = pl.BlockSpec((None, MEM_LEN, X_W), lambda i: (i // per_seq, 0, 0))
    cos, sin, cos_t, sin_t = tables
    return pl.pallas_call(
        _in_proj_body,
        grid=(n_tiles,),
        in_specs=[
            pl.BlockSpec((PROJ_TILE, D_MODEL), lambda i: (i, 0)),
            pl.BlockSpec((1, D_MODEL), lambda i: (0, 0)),
            const((D_MODEL, N_COLS)),
            const((N_ROWS, D_MODEL)),
            pl.BlockSpec((PROJ_TILE, LANES), lambda i: (i % per_seq, 0)),
            pl.BlockSpec((PROJ_TILE, LANES), lambda i: (i % per_seq, 0)),
            pl.BlockSpec((HALF, PROJ_TILE), lambda i: (0, i % per_seq)),
            pl.BlockSpec((HALF, PROJ_TILE), lambda i: (0, i % per_seq)),
            pl.BlockSpec((1, N_BRANCH * D_MODEL), lambda i: (0, 0)),
            mem, mem,
        ],
        out_specs=out_specs,
        out_shape=out_shape,
        compiler_params=pltpu.CompilerParams(
            dimension_semantics=("arbitrary",), vmem_limit_bytes=VMEM_LIMIT),
        name="in_proj",
    )(x2, g_in, w_cols, w_rows, cos, sin, cos_t, sin_t, b_merge, xk, xv)


def _mem_proj_body(m_ref, g_ref, w_ref, k_ref, v_ref):
    h = _rmsnorm(m_ref[...], g_ref[...]).astype(BF16)
    k_ref[...] = _dot(h, w_ref[:, 0:X_W]).astype(BF16)
    v_ref[...] = _dot(h, w_ref[:, X_W:2 * X_W]).astype(BF16)


def _mem_proj(mem2, g_mem, w_kv):
    m = mem2.shape[0]
    return pl.pallas_call(
        _mem_proj_body,
        grid=(m // PROJ_TILE,),
        in_specs=[
            pl.BlockSpec((PROJ_TILE, D_MODEL), lambda i: (i, 0)),
            pl.BlockSpec((1, D_MODEL), lambda i: (0, 0)),
            pl.BlockSpec((D_MODEL, 2 * X_W), lambda i: (0, 0)),
        ],
        out_specs=[pl.BlockSpec((PROJ_TILE, X_W), lambda i: (i, 0))] * 2,
        out_shape=[jax.ShapeDtypeStruct((m, X_W), BF16)] * 2,
        compiler_params=pltpu.CompilerParams(
            dimension_semantics=("arbitrary",), vmem_limit_bytes=VMEM_LIMIT),
        name="mem_proj",
    )(mem2, g_mem, w_kv)


def _pair_head_masks():
    r = lax.broadcasted_iota(jnp.int32, (LANES, BLK), 0)
    return [(r >= e * HEAD_DIM) & (r < (e + 1) * HEAD_DIM) for e in range(HEADS_PER_PAIR)]


def _causal_block_mask_t():
    kr = lax.broadcasted_iota(jnp.int32, (BLK, BLK), 0)
    qc = lax.broadcasted_iota(jnp.int32, (BLK, BLK), 1)
    return kr <= qc


CHAIN_ROWS = 64


def _col_reduce(x, combine, finish):
    rows = x.shape[0]
    acc = x[0:CHAIN_ROWS]
    for r in range(1, rows // CHAIN_ROWS):
        acc = combine(acc, x[r * CHAIN_ROWS:(r + 1) * CHAIN_ROWS])
    return finish(acc, axis=0, keepdims=True)


SAFE_EXP = 100


def _softmax_pv_t(st, vt, shifted):
    if shifted:
        st = st - _col_reduce(st, jnp.maximum, jnp.max)
    p = jnp.exp2(st).astype(BF16)
    ones = jnp.ones((BF16_ROWS, vt.shape[1]), BF16)
    out = _dot(jnp.concatenate([vt, ones], axis=0), p)
    l = out[HEAD_DIM:HEAD_DIM + 1]
    return out[0:HEAD_DIM] / l, l


def _denominators_safe(ls):
    ok = None
    for l in ls:
        good = (l > 2.0 ** -SAFE_EXP) & (l < 2.0 ** SAFE_EXP)
        ok = good if ok is None else ok & good
    return jnp.min(jnp.where(ok, 1.0, 0.0))


def _attend_guarded(attend_all):
    safe = _denominators_safe(attend_all(False))

    @pl.when(safe < 0.5)
    def _():
        attend_all(True)


def _store_pair(o_ref, gate_ref, p, halves):
    lanes = slice(p * LANES, (p + 1) * LANES)
    y = jnp.concatenate(halves, axis=0).T
    o_ref[:, lanes] = (y * gate_ref[:, lanes].astype(F32)).astype(o_ref.dtype)


def _pair_pipeline(n_pairs, issue, consume):
    nxt = issue(0)
    for p in range(n_pairs):
        cur = nxt
        if p + 1 < n_pairs:
            nxt = issue(p + 1)
        consume(p, cur)


def _moba_select(qm, km, n_past):
    km16 = jnp.concatenate([km, jnp.zeros((BF16_ROWS - N_BLK, LANES), F32)], axis=0).astype(BF16)
    gate = _dot(km16, qm)
    blk = lax.broadcasted_iota(jnp.int32, (BF16_ROWS, BLK), 0)
    past = blk < n_past
    gate = jnp.where(past, gate, NEG_INF)
    rank = jnp.zeros((BF16_ROWS, BLK), F32)
    for jp in range(n_past):
        other = gate[jp:jp + 1, :]
        beats = (other > gate) | ((other == gate) & (blk > jp))
        rank = rank + jnp.where(beats, 1.0, 0.0)
    return jnp.where(past & (rank < MOBA_TOPK), 1.0, 0.0)


def _moba_block(c, qT_ref, k_ref, vT_ref, km_ref, gate_ref, acc_ref, o_ref):
    del acc_ref
    nk = (c + 1) * BLK
    causal = _causal_block_mask_t()
    head_masks = _pair_head_masks()
    gated = c > MOBA_TOPK

    def scores(h):
        p, e = divmod(h, HEADS_PER_PAIR)
        feats = slice(p * LANES, (p + 1) * LANES)
        q2 = qT_ref[feats, :]
        qm = jnp.where(head_masks[e], q2, jnp.zeros_like(q2))
        sel = _moba_select(qm, km_ref[:, feats], c) if gated else None
        return _dot(k_ref[:, feats], qm), sel

    def attend(h, st, sel, shifted):
        parts = []
        for j in range(c):
            blk = st[j * BLK:(j + 1) * BLK]
            if gated:
                blk = jnp.where(sel[j:j + 1, :] > 0.5, blk, NEG_INF)
            parts.append(blk)
        parts.append(jnp.where(causal, st[c * BLK:nk], NEG_INF))
        st = jnp.concatenate(parts, axis=0) if c else parts[0]
        return _softmax_pv_t(st, vT_ref[h * HEAD_DIM:(h + 1) * HEAD_DIM, :], shifted)

    def attend_all(shifted):
        denominators = []

        def consume(p, cur):
            outs = [attend(p * HEADS_PER_PAIR + e, *cur[e], shifted) for e in range(HEADS_PER_PAIR)]
            denominators.extend(l for _, l in outs)
            _store_pair(o_ref, gate_ref, p, [o for o, _ in outs])

        _pair_pipeline(
            N_PAIRS,
            lambda p: [scores(p * HEADS_PER_PAIR + e) for e in range(HEADS_PER_PAIR)],
            consume)
        return denominators

    _attend_guarded(attend_all)


def _key_to_f32(key):
    bits = key ^ ((key >> 31) & 0x7FFFFFFF)
    return lax.bitcast_convert_type(bits, F32)


def _count_ge(sc, thr):
    acc = jnp.zeros((CHAIN_ROWS, BLK), F32)
    for r in range(sc.shape[0] // CHAIN_ROWS):
        acc = acc + jnp.where(sc[r * CHAIN_ROWS:(r + 1) * CHAIN_ROWS] >= thr, 1.0, 0.0)
    return jnp.sum(acc, axis=0, keepdims=True)


POS_INF = float("inf")


def _dsa_topk_cap(sc, nk, cap_ref):
    kf = float(DSA_TOPK)
    cnt0 = _count_ge(sc, 0.0)
    nonneg = cnt0 >= kf
    t0 = jnp.where(nonneg, 0, INT_MIN).astype(jnp.int32)
    n0 = jnp.where(nonneg, cnt0, float(nk))

    def bit_step(it, state):
        t, n_ge = state
        cand = t | jnp.left_shift(jnp.int32(1), 30 - it)
        cnt = _count_ge(sc, _key_to_f32(cand))
        take = cnt >= kf
        return jnp.where(take, cand, t), jnp.where(take, cnt, n_ge)

    t, n_ge = lax.fori_loop(0, 31, bit_step, (t0, n0))
    t_val = _key_to_f32(t)
    ties_beyond_k = jnp.max(n_ge) > kf

    @pl.when(jnp.logical_not(ties_beyond_k))
    def _():
        cap_ref[...] = jnp.where(sc >= t_val, POS_INF, NEG_INF)

    @pl.when(ties_beyond_k)
    def _():
        t_next = _key_to_f32(t + 1)
        need = kf - _count_ge(sc, t_next)
        r = lax.broadcasted_iota(jnp.int32, (BLK, BLK), 0)
        c = lax.broadcasted_iota(jnp.int32, (BLK, BLK), 1)
        strict_lower = jnp.where(c < r, 1.0, 0.0).astype(BF16)
        carry = jnp.zeros((1, BLK), F32)
        for j in range(nk // BLK):
            rows = slice(j * BLK, (j + 1) * BLK)
            sc_j = sc[rows]
            gt = sc_j >= t_next
            eq = (sc_j >= t_val) & jnp.logical_not(gt)
            eq_f = jnp.where(eq, 1.0, 0.0)
            before = _dot(strict_lower, eq_f.astype(BF16)) + carry
            carry = carry + _col_reduce(eq_f, jnp.add, jnp.sum)
            cap_ref[rows, :] = jnp.where(gt | (eq & (before < need)), POS_INF, NEG_INF)

    return cap_ref[...]


def _dsa_block(c, iqT_ref, ik_ref, iwT_ref, qT_ref, k_ref, vT_ref, gate_ref, acc_ref, o_ref,
               cap_ref):
    del acc_ref
    nk = (c + 1) * BLK
    causal = _causal_block_mask_t()
    head_masks = _pair_head_masks()

    def masked_q_dots(keys, qT2):
        return [_dot(keys, jnp.where(m, qT2, jnp.zeros_like(qT2))) for m in head_masks]

    if c == 0:
        cap = jnp.where(causal, POS_INF, NEG_INF)
    else:
        wscale = IDX_HEADS ** -0.5
        acc = [jnp.zeros((nk, BLK), F32)]

        def accumulate(p, logits):
            for e in range(HEADS_PER_PAIR):
                w = iwT_ref[p * HEADS_PER_PAIR + e:p * HEADS_PER_PAIR + e + 1, :] * wscale
                acc[0] = acc[0] + jnp.maximum(logits[e], 0.0) * w

        _pair_pipeline(
            N_PAIRS,
            lambda p: masked_q_dots(ik_ref[...], iqT_ref[p * LANES:(p + 1) * LANES, :]),
            accumulate)
        own = jnp.where(causal, acc[0][c * BLK:nk], NEG_INF)
        sc = jnp.concatenate([acc[0][0:c * BLK], own], axis=0)
        cap = _dsa_topk_cap(sc, nk, cap_ref)

    def attend_all(shifted):
        denominators = []

        def consume(p, scores):
            outs = [_softmax_pv_t(
                jnp.minimum(scores[e], cap),
                vT_ref[p * LANES + e * HEAD_DIM:p * LANES + (e + 1) * HEAD_DIM, :], shifted)
                for e in range(HEADS_PER_PAIR)]
            denominators.extend(l for _, l in outs)
            _store_pair(o_ref, gate_ref, p, [o for o, _ in outs])

        _pair_pipeline(
            N_PAIRS,
            lambda p: masked_q_dots(k_ref[:, p * LANES:(p + 1) * LANES],
                                    qT_ref[p * LANES:(p + 1) * LANES, :]),
            consume)
        return denominators

    _attend_guarded(attend_all)


def _mixers_block(c, mqT_ref, mk_ref, mvT_ref, km_ref, mgate_ref, iqT_ref, ik_ref, iwT_ref, dqT_ref,
                  dk_ref, dvT_ref, dgate_ref, macc_ref, dacc_ref, mo_ref, do_ref, cap_ref):
    _moba_block(c, mqT_ref, mk_ref, mvT_ref, km_ref, mgate_ref, macc_ref, mo_ref)
    _dsa_block(c, iqT_ref, ik_ref, iwT_ref, dqT_ref, dk_ref, dvT_ref, dgate_ref, dacc_ref, do_ref,
               cap_ref)


def _mixers(mqT, mk, mvT, kmean, mgate, iqT, ik, iwT, dqT, dk, dvT, dgate, m_zeroed, d_zeroed, batch):
    seq3 = lambda a, w: a.reshape(batch, SEQ, w)
    macc, dacc = seq3(m_zeroed, ATT_W), seq3(d_zeroed, ATT_W)
    args = (mqT, seq3(mk, ATT_W), mvT, kmean, seq3(mgate, ATT_W),
            iqT, seq3(ik, LANES), iwT, dqT, seq3(dk, ATT_W), dvT, seq3(dgate, ATT_W))
    for c in range(N_BLK):
        nk = (c + 1) * BLK
        qtile = lambda r, c=c: pl.BlockSpec((None, r, BLK), lambda b: (b, 0, c))
        keys = lambda w, nk=nk: pl.BlockSpec((None, nk, w), lambda b: (b, 0, 0))
        values = pl.BlockSpec((None, ATT_W, nk), lambda b: (b, 0, 0))
        out_tile = pl.BlockSpec((None, BLK, ATT_W), lambda b, c=c: (b, c, 0))
        acc_any = pl.BlockSpec(memory_space=pl.ANY)
        in_specs = [qtile(ATT_W), keys(ATT_W), values,
                    pl.BlockSpec((None, N_BLK, ATT_W), lambda b: (b, 0, 0)), out_tile,
                    qtile(ATT_W), keys(LANES), qtile(BF16_ROWS), qtile(ATT_W), keys(ATT_W), values,
                    out_tile, acc_any, acc_any]
        macc, dacc = pl.pallas_call(
            functools.partial(_mixers_block, c),
            grid=(batch,),
            in_specs=in_specs,
            out_specs=[out_tile, out_tile],
            out_shape=[jax.ShapeDtypeStruct(macc.shape, macc.dtype)] * 2,
            input_output_aliases={len(args): 0, len(args) + 1: 1},
            scratch_shapes=[pltpu.VMEM((nk, BLK), F32)],
            compiler_params=pltpu.CompilerParams(
                dimension_semantics=("arbitrary",), vmem_limit_bytes=VMEM_LIMIT),
            name=f"mixers_q{c}",
        )(*args, macc, dacc)
    return macc.reshape(batch * SEQ, ATT_W), dacc.reshape(batch * SEQ, ATT_W)


def _merge_body(x_ref, ya_ref, yb_ref, yc_ref, gates_ref, wa_ref, wb_ref, wc_ref, wo_ref, gf_ref,
                o_ref):
    def residual(rows):
        u = jnp.zeros((PROJ_TILE, D_MODEL), F32)
        for n, (y_ref, w_ref) in enumerate(((ya_ref, wa_ref), (yb_ref, wb_ref), (yc_ref, wc_ref))):
            sl = slice(n * D_MODEL, (n + 1) * D_MODEL)
            u = u + gates_ref[rows, sl].astype(F32) * _dot(y_ref[rows, :], w_ref[...])
        return x_ref[rows, :] + _dot(u.astype(BF16), wo_ref[...])

    halves = [slice(r * PROJ_TILE, (r + 1) * PROJ_TILE) for r in range(MERGE_TILE // PROJ_TILE)]
    ys = [residual(rows) for rows in halves]
    for rows, y in zip(halves, ys):
        o_ref[rows, :] = _rmsnorm(y, gf_ref[...])


MERGE_TILE = 2 * PROJ_TILE


def _merge(x2, ya, yb, yc, gates, wa, wb, wc, wo, g_final):
    tile = lambda w: pl.BlockSpec((MERGE_TILE, w), lambda i: (i, 0))
    const = lambda r, w: pl.BlockSpec((r, w), lambda i: (0, 0))
    m = x2.shape[0]
    return pl.pallas_call(
        _merge_body,
        grid=(m // MERGE_TILE,),
        in_specs=[tile(D_MODEL), tile(ATT_W), tile(ATT_W), tile(X_W), tile(N_BRANCH * D_MODEL),
                  const(ATT_W, D_MODEL), const(ATT_W, D_MODEL), const(X_W, D_MODEL),
                  const(D_MODEL, D_MODEL), const(1, D_MODEL)],
        out_specs=tile(D_MODEL),
        out_shape=jax.ShapeDtypeStruct((m, D_MODEL), F32),
        compiler_params=pltpu.CompilerParams(
            dimension_semantics=("arbitrary",), vmem_limit_bytes=VMEM_LIMIT),
        name="merge",
    )(x2, ya, yb, yc, gates, wa, wb, wc, wo, g_final)


def _pack_w_in(w_in):
    offs = [0]
    for s in IN_SIZES:
        offs.append(offs[-1] + s)
    col = lambda n: w_in[:, offs[n]:offs[n + 1]]
    w_cols = jnp.concatenate([col(1), col(5), col(9), col(9), col(3), col(7), col(11), col(12),
                              col(13)], axis=1)
    iw = jnp.pad(col(10), ((0, 0), (0, BF16_ROWS - IDX_HEADS)))
    w_rows = jnp.concatenate([col(0), col(4), col(8), col(2), col(6), iw], axis=1).astype(BF16).T
    return w_cols.astype(BF16), w_rows


def _rope_tables():
    inv = jnp.power(ROPE_THETA, -jnp.arange(HALF, dtype=F32) * 2.0 / HEAD_DIM)
    ang = jnp.arange(SEQ).astype(F32)[:, None] * inv[None, :]
    cos = jnp.cos(ang)
    sin = jnp.sin(ang)
    cos_lanes = jnp.tile(cos, (1, LANES // HALF))
    sin_lanes = jnp.tile(jnp.concatenate([-sin, sin], axis=1), (1, HEADS_PER_PAIR))
    return cos_lanes, sin_lanes, cos.T, sin.T


def _layer(x2, mem2, g_in, w_in, b_merge, g_mem, w_mem_kv, w_up_moba, w_up_dsa, w_up_cross, w_out,
           g_final, batch):
    w_cols, w_rows = _pack_w_in(w_in)
    xk, xv = _mem_proj(mem2, g_mem[None, :], w_mem_kv.astype(BF16))
    (mqT, dqT, iqT, mvT, dvT, iwT, mk, dk, ik, mgate, dgate, yc, gates, kmean, ya0, yb0) = _in_proj(
        x2, g_in[None, :], w_cols, w_rows, _rope_tables(), b_merge[None, :],
        xk.reshape(batch, MEM_LEN, X_W), xv.reshape(batch, MEM_LEN, X_W), batch)
    ya, yb = _mixers(mqT, mk, mvT, kmean.reshape(batch, N_BLK, ATT_W), mgate, iqT, ik, iwT, dqT, dk,
                     dvT, dgate, ya0, yb0, batch)
    return _merge(x2, ya, yb, yc, gates, w_up_moba.astype(BF16), w_up_dsa.astype(BF16),
                  w_up_cross.astype(BF16), w_out.astype(BF16), g_final)


def kernel(x, mem, g_in, w_in, b_merge, g_mem, w_mem_kv, w_up_moba, w_up_dsa, w_up_cross, w_out,
           g_final):
    batch, seq, d = x.shape
    assert seq == SEQ and d == D_MODEL and mem.shape[1] == MEM_LEN
    assert (batch * MEM_LEN) % PROJ_TILE == 0
    assert g_in.shape[0] == 1
    out = _layer(x.reshape(batch * seq, d), mem.reshape(batch * MEM_LEN, d), g_in[0], w_in[0],
                 b_merge[0], g_mem[0], w_mem_kv[0], w_up_moba[0], w_up_dsa[0], w_up_cross[0],
                 w_out[0], g_final[None, :], batch)
    return out.reshape(batch, seq, d)
```

```python
import functools

import jax
import jax.numpy as jnp
from jax import lax
from jax.experimental import pallas as pl
from jax.experimental.pallas import tpu as pltpu

D_MODEL = 1024
SEQ = 2048
HEAD_DIM = 64
HALF = HEAD_DIM // 2
N_HEADS = 8
ATT_W = N_HEADS * HEAD_DIM
BLK = 256
N_BLK = SEQ // BLK
PROJ_TILE = 2 * BLK
MOBA_TOPK = 3
DSA_TOPK = 256
IDX_HEADS = 8
MEM_LEN = 256
X_HEADS = 4
X_HEAD_DIM = 128
X_W = X_HEADS * X_HEAD_DIM
N_BRANCH = 3
ROPE_THETA = 10000.0
RMS_EPS = 1e-6
LANES = 128
HEADS_PER_PAIR = LANES // HEAD_DIM
N_PAIRS = ATT_W // LANES
BF16_ROWS = 16

IN_SIZES = (ATT_W, ATT_W, ATT_W, ATT_W, ATT_W, ATT_W, ATT_W, ATT_W,
            IDX_HEADS * HEAD_DIM, HEAD_DIM, IDX_HEADS, X_W, X_W, N_BRANCH * D_MODEL)

COL_MK, COL_DK, COL_IK = 0, 512, 1024
COL_MG, COL_DG, COL_XQ, COL_XG, COL_GL = 1152, 1664, 2176, 2688, 3200
N_COLS = COL_GL + N_BRANCH * D_MODEL
ROW_MQ, ROW_DQ, ROW_IQ, ROW_MV, ROW_DV, ROW_IW = 0, 512, 1024, 1536, 2048, 2560
N_ROWS = ROW_IW + BF16_ROWS

VMEM_LIMIT = 56 * 1024 * 1024

F32 = jnp.float32
BF16 = jnp.bfloat16
NEG_INF = float("-inf")
INT_MIN = -2 ** 31
LOG2E = 1.4426950408889634


def _dot(a, b):
    return jnp.dot(a, b, preferred_element_type=F32)


def _nt_dot(a, b):
    return lax.dot_general(a, b, (((1,), (1,)), ((), ())), preferred_element_type=F32)


def _rmsnorm(x, g):
    ms = jnp.mean(x * x, axis=-1, keepdims=True)
    return (x * lax.rsqrt(ms + RMS_EPS)) * g


def _softmax_pv(s, v):
    m = jnp.max(s, axis=1, keepdims=True)
    p = jnp.exp2(s - m)
    l = jnp.sum(p, axis=1, keepdims=True)
    return _dot(p.astype(BF16), v) / l


def _in_proj_body(x_ref, g_ref, wc_ref, wr_ref, cos_ref, sin_ref, cos_t_ref, sin_t_ref,
                  bm_ref, xk_ref, xv_ref,
                  mqT_ref, dqT_ref, iqT_ref, mvT_ref, dvT_ref, iwT_ref,
                  mk_ref, dk_ref, ik_ref, mg_ref, dg_ref, yc_ref, gl_ref, km_ref, ya0_ref, yb0_ref):
    ya0_ref[...] = jnp.zeros(ya0_ref.shape, ya0_ref.dtype)
    yb0_ref[...] = jnp.zeros(yb0_ref.shape, yb0_ref.dtype)
    h = _rmsnorm(x_ref[...], g_ref[...]).astype(BF16)
    qk_scale = HEAD_DIM ** -0.5
    att_scale = qk_scale * LOG2E

    def seg(off, width):
        return lambda: _dot(h, wc_ref[:, off:off + width])

    kept = {}

    def keep_silu_xg(z):
        kept["xg"] = jax.nn.silu(z)

    def cross_attention(z):
        xq = z.astype(BF16)
        xscale = X_HEAD_DIM ** -0.5 * LOG2E
        yc_parts = []
        for hd in range(X_HEADS):
            sl = slice(hd * X_HEAD_DIM, (hd + 1) * X_HEAD_DIM)
            s = _nt_dot(xq[:, sl], xk_ref[:, sl]) * xscale
            yc_parts.append(_softmax_pv(s, xv_ref[:, sl]))
        yc_ref[...] = (jnp.concatenate(yc_parts, axis=1) * kept["xg"]).astype(BF16)

    def store_merge_gate(c):
        sl = slice(c * D_MODEL, (c + 1) * D_MODEL)

        def epilogue(z):
            gl_ref[:, sl] = jax.nn.sigmoid(z + bm_ref[:, sl]).astype(BF16)
        return epilogue

    def store_silu(ref):
        def epilogue(z):
            ref[...] = jax.nn.silu(z).astype(BF16)
        return epilogue

    def store_feature_major(zt):
        cos_t = cos_t_ref[...]
        sin_t = sin_t_ref[...]
        for ref, row0, scale in ((mqT_ref, ROW_MQ, att_scale), (dqT_ref, ROW_DQ, att_scale),
                                 (iqT_ref, ROW_IQ, qk_scale)):
            for hd in range(N_HEADS):
                r = row0 + hd * HEAD_DIM
                x1 = zt[r:r + HALF]
                x2 = zt[r + HALF:r + HEAD_DIM]
                o = hd * HEAD_DIM
                ref[o:o + HALF, :] = ((x1 * cos_t - x2 * sin_t) * scale).astype(BF16)
                ref[o + HALF:o + HEAD_DIM, :] = ((x2 * cos_t + x1 * sin_t) * scale).astype(BF16)
        mvT_ref[...] = zt[ROW_MV:ROW_MV + ATT_W].astype(BF16)
        dvT_ref[...] = zt[ROW_DV:ROW_DV + ATT_W].astype(BF16)
        iwT_ref[...] = zt[ROW_IW:ROW_IW + BF16_ROWS]

    def store_roped(ref, mean_ref=None):
        def epilogue(z):
            cos = cos_ref[...]
            sin = sin_ref[...]
            lane = lax.broadcasted_iota(jnp.int32, (PROJ_TILE, LANES), 1)
            first_half = (lane & HALF) == 0
            for c in range(z.shape[1] // LANES):
                lanes = slice(c * LANES, (c + 1) * LANES)
                zc = z[:, lanes]
                partner = jnp.where(first_half, pltpu.roll(zc, LANES - HALF, 1),
                                    pltpu.roll(zc, HALF, 1))
                r = zc * cos + partner * sin
                if mean_ref is not None:
                    for blk in range(PROJ_TILE // BLK):
                        mean_ref[blk, :, lanes] = jnp.mean(r[blk * BLK:(blk + 1) * BLK], axis=0,
                                                           keepdims=True)
                ref[:, lanes] = r.astype(ref.dtype)
        return epilogue

    stages = [(seg(COL_XG, X_W), keep_silu_xg), (seg(COL_XQ, X_W), cross_attention)]
    stages += [(seg(COL_GL + c * D_MODEL, D_MODEL), store_merge_gate(c)) for c in range(N_BRANCH)]
    stages += [(seg(COL_MG, ATT_W), store_silu(mg_ref)), (seg(COL_DG, ATT_W), store_silu(dg_ref)),
               (lambda: _nt_dot(wr_ref[...], h), store_feature_major),
               (seg(COL_MK, ATT_W), store_roped(mk_ref, mean_ref=km_ref)),
               (seg(COL_DK, ATT_W), store_roped(dk_ref)),
               (seg(COL_IK, LANES), store_roped(ik_ref))]
    nxt = stages[0][0]()
    for i, (_, epilogue) in enumerate(stages):
        cur = nxt
        if i + 1 < len(stages):
            nxt = stages[i + 1][0]()
        epilogue(cur)


def _in_proj(x2, g_in, w_cols, w_rows, tables, b_merge, xk, xv, batch):
    m = x2.shape[0]
    n_tiles = m // PROJ_TILE
    per_seq = SEQ // PROJ_TILE
    row = lambda w: pl.BlockSpec((PROJ_TILE, w), lambda i: (i, 0))
    feat = lambda r: pl.BlockSpec((None, r, PROJ_TILE), lambda i: (i // per_seq, 0, i % per_seq))
    const = lambda shape: pl.BlockSpec(shape, lambda i: (0, 0), pipeline_mode=pl.Buffered(1))
    tm = lambda w, dt: jax.ShapeDtypeStruct((m, w), dt)
    fm = lambda r, dt: jax.ShapeDtypeStruct((batch, r, SEQ), dt)
    out_shape = [
        fm(ATT_W, BF16), fm(ATT_W, BF16), fm(ATT_W, BF16),
        fm(ATT_W, BF16), fm(ATT_W, BF16), fm(BF16_ROWS, F32),
        tm(ATT_W, BF16), tm(ATT_W, BF16), tm(LANES, BF16),
        tm(ATT_W, BF16), tm(ATT_W, BF16),
        tm(X_W, BF16),
        tm(N_BRANCH * D_MODEL, BF16),
        jax.ShapeDtypeStruct((m // BLK, 1, ATT_W), F32),
        tm(ATT_W, BF16), tm(ATT_W, BF16),
    ]
    out_specs = [feat(ATT_W)] * 5 + [feat(BF16_ROWS)] + [
        row(ATT_W), row(ATT_W), row(LANES), row(ATT_W), row(ATT_W), row(X_W),
        row(N_BRANCH * D_MODEL),
        pl.BlockSpec((PROJ_TILE // BLK, 1, ATT_W), lambda i: (i, 0, 0)),
        row(ATT_W), row(ATT_W)]
    mem = pl.BlockSpec((None, MEM_LEN, X_W), lambda i: (i // per_seq, 0, 0))
    cos, sin, cos_t, sin_t = tables
    return pl.pallas_call(
        _in_proj_body,
        grid=(n_tiles,),
        in_specs=[
            pl.BlockSpec((PROJ_TILE, D_MODEL), lambda i: (i, 0)),
            pl.BlockSpec((1, D_MODEL), lambda i: (0, 0)),
            const((D_MODEL, N_COLS)),
            const((N_ROWS, D_MODEL)),
            pl.BlockSpec((PROJ_TILE, LANES), lambda i: (i % per_seq, 0)),
            pl.BlockSpec((PROJ_TILE, LANES), lambda i: (i % per_seq, 0)),
            pl.BlockSpec((HALF, PROJ_TILE), lambda i: (0, i % per_seq)),
            pl.BlockSpec((HALF, PROJ_TILE), lambda i: (0, i % per_seq)),
            pl.BlockSpec((1, N_BRANCH * D_MODEL), lambda i: (0, 0)),
            mem, mem,
        ],
        out_specs=out_specs,
        out_shape=out_shape,
        compiler_params=pltpu.CompilerParams(
            dimension_semantics=("arbitrary",), vmem_limit_bytes=VMEM_LIMIT),
        name="in_proj",
    )(x2, g_in, w_cols, w_rows, cos, sin, cos_t, sin_t, b_merge, xk, xv)


def _mem_proj_body(m_ref, g_ref, w_ref, k_ref, v_ref):
    h = _rmsnorm(m_ref[...], g_ref[...]).astype(BF16)
    k_ref[...] = _dot(h, w_ref[:, 0:X_W]).astype(BF16)
    v_ref[...] = _dot(h, w_ref[:, X_W:2 * X_W]).astype(BF16)


def _mem_proj(mem2, g_mem, w_kv):
    m = mem2.shape[0]
    return pl.pallas_call(
        _mem_proj_body,
        grid=(m // PROJ_TILE,),
        in_specs=[
            pl.BlockSpec((PROJ_TILE, D_MODEL), lambda i: (i, 0)),
            pl.BlockSpec((1, D_MODEL), lambda i: (0, 0)),
            pl.BlockSpec((D_MODEL, 2 * X_W), lambda i: (0, 0)),
        ],
        out_specs=[pl.BlockSpec((PROJ_TILE, X_W), lambda i: (i, 0))] * 2,
        out_shape=[jax.ShapeDtypeStruct((m, X_W), BF16)] * 2,
        compiler_params=pltpu.CompilerParams(
            dimension_semantics=("arbitrary",), vmem_limit_bytes=VMEM_LIMIT),
        name="mem_proj",
    )(mem2, g_mem, w_kv)


def _pair_head_masks():
    r = lax.broadcasted_iota(jnp.int32, (LANES, BLK), 0)
    return [(r >= e * HEAD_DIM) & (r < (e + 1) * HEAD_DIM) for e in range(HEADS_PER_PAIR)]


def _causal_block_mask_t():
    kr = lax.broadcasted_iota(jnp.int32, (BLK, BLK), 0)
    qc = lax.broadcasted_iota(jnp.int32, (BLK, BLK), 1)
    return kr <= qc


CHAIN_ROWS = 64


def _col_reduce(x, combine, finish):
    rows = x.shape[0]
    acc = x[0:CHAIN_ROWS]
    for r in range(1, rows // CHAIN_ROWS):
        acc = combine(acc, x[r * CHAIN_ROWS:(r + 1) * CHAIN_ROWS])
    return finish(acc, axis=0, keepdims=True)


SAFE_EXP = 100


def _softmax_pv_t(st, vt, shifted):
    if shifted:
        st = st - _col_reduce(st, jnp.maximum, jnp.max)
    p = jnp.exp2(st).astype(BF16)
    ones = jnp.ones((BF16_ROWS, vt.shape[1]), BF16)
    out = _dot(jnp.concatenate([vt, ones], axis=0), p)
    l = out[HEAD_DIM:HEAD_DIM + 1]
    return out[0:HEAD_DIM] / l, l


def _denominators_safe(ls):
    ok = None
    for l in ls:
        good = (l > 2.0 ** -SAFE_EXP) & (l < 2.0 ** SAFE_EXP)
        ok = good if ok is None else ok & good
    return jnp.min(jnp.where(ok, 1.0, 0.0))


def _attend_guarded(attend_all):
    safe = _denominators_safe(attend_all(False))

    @pl.when(safe < 0.5)
    def _():
        attend_all(True)


def _store_pair(o_ref, gate_ref, p, halves):
    lanes = slice(p * LANES, (p + 1) * LANES)
    y = jnp.concatenate(halves, axis=0).T
    o_ref[:, lanes] = (y * gate_ref[:, lanes].astype(F32)).astype(o_ref.dtype)


def _pair_pipeline(n_pairs, issue, consume):
    nxt = issue(0)
    for p in range(n_pairs):
        cur = nxt
        if p + 1 < n_pairs:
            nxt = issue(p + 1)
        consume(p, cur)


def _moba_select(qm, km, n_past):
    km16 = jnp.concatenate([km, jnp.zeros((BF16_ROWS - N_BLK, LANES), F32)], axis=0).astype(BF16)
    gate = _dot(km16, qm)
    blk = lax.broadcasted_iota(jnp.int32, (BF16_ROWS, BLK), 0)
    past = blk < n_past
    gate = jnp.where(past, gate, NEG_INF)
    rank = jnp.zeros((BF16_ROWS, BLK), F32)
    for jp in range(n_past):
        other = gate[jp:jp + 1, :]
        beats = (other > gate) | ((other == gate) & (blk > jp))
        rank = rank + jnp.where(beats, 1.0, 0.0)
    return jnp.where(past & (rank < MOBA_TOPK), 1.0, 0.0)


def _moba_block(c, qT_ref, k_ref, vT_ref, km_ref, gate_ref, acc_ref, o_ref):
    del acc_ref
    nk = (c + 1) * BLK
    causal = _causal_block_mask_t()
    head_masks = _pair_head_masks()
    gated = c > MOBA_TOPK

    def scores(h):
        p, e = divmod(h, HEADS_PER_PAIR)
        feats = slice(p * LANES, (p + 1) * LANES)
        q2 = qT_ref[feats, :]
        qm = jnp.where(head_masks[e], q2, jnp.zeros_like(q2))
        sel = _moba_select(qm, km_ref[:, feats], c) if gated else None
        return _dot(k_ref[:, feats], qm), sel

    def attend(h, st, sel, shifted):
        parts = []
        for j in range(c):
            blk = st[j * BLK:(j + 1) * BLK]
            if gated:
                blk = jnp.where(sel[j:j + 1, :] > 0.5, blk, NEG_INF)
            parts.append(blk)
        parts.append(jnp.where(causal, st[c * BLK:nk], NEG_INF))
        st = jnp.concatenate(parts, axis=0) if c else parts[0]
        return _softmax_pv_t(st, vT_ref[h * HEAD_DIM:(h + 1) * HEAD_DIM, :], shifted)

    def attend_all(shifted):
        denominators = []

        def consume(p, cur):
            outs = [attend(p * HEADS_PER_PAIR + e, *cur[e], shifted) for e in range(HEADS_PER_PAIR)]
            denominators.extend(l for _, l in outs)
            _store_pair(o_ref, gate_ref, p, [o for o, _ in outs])

        _pair_pipeline(
            N_PAIRS,
            lambda p: [scores(p * HEADS_PER_PAIR + e) for e in range(HEADS_PER_PAIR)],
            consume)
        return denominators

    _attend_guarded(attend_all)


def _key_to_f32(key):
    bits = key ^ ((key >> 31) & 0x7FFFFFFF)
    return lax.bitcast_convert_type(bits, F32)


def _count_ge(sc, thr):
    acc = jnp.zeros((CHAIN_ROWS, BLK), F32)
    for r in range(sc.shape[0] // CHAIN_ROWS):
        acc = acc + jnp.where(sc[r * CHAIN_ROWS:(r + 1) * CHAIN_ROWS] >= thr, 1.0, 0.0)
    return jnp.sum(acc, axis=0, keepdims=True)


POS_INF = float("inf")


def _dsa_topk_cap(sc, nk, cap_ref):
    kf = float(DSA_TOPK)
    cnt0 = _count_ge(sc, 0.0)
    nonneg = cnt0 >= kf
    t0 = jnp.where(nonneg, 0, INT_MIN).astype(jnp.int32)
    n0 = jnp.where(nonneg, cnt0, float(nk))

    def bit_step(it, state):
        t, n_ge = state
        cand = t | jnp.left_shift(jnp.int32(1), 30 - it)
        cnt = _count_ge(sc, _key_to_f32(cand))
        take = cnt >= kf
        return jnp.where(take, cand, t), jnp.where(take, cnt, n_ge)

    t, n_ge = lax.fori_loop(0, 31, bit_step, (t0, n0))
    t_val = _key_to_f32(t)
    ties_beyond_k = jnp.max(n_ge) > kf

    @pl.when(jnp.logical_not(ties_beyond_k))
    def _():
        cap_ref[...] = jnp.where(sc >= t_val, POS_INF, NEG_INF)

    @pl.when(ties_beyond_k)
    def _():
        t_next = _key_to_f32(t + 1)
        need = kf - _count_ge(sc, t_next)
        r = lax.broadcasted_iota(jnp.int32, (BLK, BLK), 0)
        c = lax.broadcasted_iota(jnp.int32, (BLK, BLK), 1)
        strict_lower = jnp.where(c < r, 1.0, 0.0).astype(BF16)
        carry = jnp.zeros((1, BLK), F32)
        for j in range(nk // BLK):
            rows = slice(j * BLK, (j + 1) * BLK)
            sc_j = sc[rows]
            gt = sc_j >= t_next
            eq = (sc_j >= t_val) & jnp.logical_not(gt)
            eq_f = jnp.where(eq, 1.0, 0.0)
            before = _dot(strict_lower, eq_f.astype(BF16)) + carry
            carry = carry + _col_reduce(eq_f, jnp.add, jnp.sum)
            cap_ref[rows, :] = jnp.where(gt | (eq & (before < need)), POS_INF, NEG_INF)

    return cap_ref[...]


def _dsa_block(c, iqT_ref, ik_ref, iwT_ref, qT_ref, k_ref, vT_ref, gate_ref, acc_ref, o_ref,
               cap_ref):
    del acc_ref
    nk = (c + 1) * BLK
    causal = _causal_block_mask_t()
    head_masks = _pair_head_masks()

    def masked_q_dots(keys, qT2):
        return [_dot(keys, jnp.where(m, qT2, jnp.zeros_like(qT2))) for m in head_masks]

    if c == 0:
        cap = jnp.where(causal, POS_INF, NEG_INF)
    else:
        wscale = IDX_HEADS ** -0.5
        acc = [jnp.zeros((nk, BLK), F32)]

        def accumulate(p, logits):
            for e in range(HEADS_PER_PAIR):
                w = iwT_ref[p * HEADS_PER_PAIR + e:p * HEADS_PER_PAIR + e + 1, :] * wscale
                acc[0] = acc[0] + jnp.maximum(logits[e], 0.0) * w

        _pair_pipeline(
            N_PAIRS,
            lambda p: masked_q_dots(ik_ref[...], iqT_ref[p * LANES:(p + 1) * LANES, :]),
            accumulate)
        own = jnp.where(causal, acc[0][c * BLK:nk], NEG_INF)
        sc = jnp.concatenate([acc[0][0:c * BLK], own], axis=0)
        cap = _dsa_topk_cap(sc, nk, cap_ref)

    def attend_all(shifted):
        denominators = []

        def consume(p, scores):
            outs = [_softmax_pv_t(
                jnp.minimum(scores[e], cap),
                vT_ref[p * LANES + e * HEAD_DIM:p * LANES + (e + 1) * HEAD_DIM, :], shifted)
                for e in range(HEADS_PER_PAIR)]
            denominators.extend(l for _, l in outs)
            _store_pair(o_ref, gate_ref, p, [o for o, _ in outs])

        _pair_pipeline(
            N_PAIRS,
            lambda p: masked_q_dots(k_ref[:, p * LANES:(p + 1) * LANES],
                                    qT_ref[p * LANES:(p + 1) * LANES, :]),
            consume)
        return denominators

    _attend_guarded(attend_all)


def _mixers_block(c, mqT_ref, mk_ref, mvT_ref, km_ref, mgate_ref, iqT_ref, ik_ref, iwT_ref, dqT_ref,
                  dk_ref, dvT_ref, dgate_ref, macc_ref, dacc_ref, mo_ref, do_ref, cap_ref):
    _moba_block(c, mqT_ref, mk_ref, mvT_ref, km_ref, mgate_ref, macc_ref, mo_ref)
    _dsa_block(c, iqT_ref, ik_ref, iwT_ref, dqT_ref, dk_ref, dvT_ref, dgate_ref, dacc_ref, do_ref,
               cap_ref)


def _mixers(mqT, mk, mvT, kmean, mgate, iqT, ik, iwT, dqT, dk, dvT, dgate, m_zeroed, d_zeroed, batch):
    seq3 = lambda a, w: a.reshape(batch, SEQ, w)
    macc, dacc = seq3(m_zeroed, ATT_W), seq3(d_zeroed, ATT_W)
    args = (mqT, seq3(mk, ATT_W), mvT, kmean, seq3(mgate, ATT_W),
            iqT, seq3(ik, LANES), iwT, dqT, seq3(dk, ATT_W), dvT, seq3(dgate, ATT_W))
    for c in range(N_BLK):
        nk = (c + 1) * BLK
        qtile = lambda r, c=c: pl.BlockSpec((None, r, BLK), lambda b: (b, 0, c))
        keys = lambda w, nk=nk: pl.BlockSpec((None, nk, w), lambda b: (b, 0, 0))
        values = pl.BlockSpec((None, ATT_W, nk), lambda b: (b, 0, 0))
        out_tile = pl.BlockSpec((None, BLK, ATT_W), lambda b, c=c: (b, c, 0))
        acc_any = pl.BlockSpec(memory_space=pl.ANY)
        in_specs = [qtile(ATT_W), keys(ATT_W), values,
                    pl.BlockSpec((None, N_BLK, ATT_W), lambda b: (b, 0, 0)), out_tile,
                    qtile(ATT_W), keys(LANES), qtile(BF16_ROWS), qtile(ATT_W), keys(ATT_W), values,
                    out_tile, acc_any, acc_any]
        macc, dacc = pl.pallas_call(
            functools.partial(_mixers_block, c),
            grid=(batch,),
            in_specs=in_specs,
            out_specs=[out_tile, out_tile],
            out_shape=[jax.ShapeDtypeStruct(macc.shape, macc.dtype)] * 2,
            input_output_aliases={len(args): 0, len(args) + 1: 1},
            scratch_shapes=[pltpu.VMEM((nk, BLK), F32)],
            compiler_params=pltpu.CompilerParams(
                dimension_semantics=("arbitrary",), vmem_limit_bytes=VMEM_LIMIT),
            name=f"mixers_q{c}",
        )(*args, macc, dacc)
    return macc.reshape(batch * SEQ, ATT_W), dacc.reshape(batch * SEQ, ATT_W)


def _merge_body(x_ref, ya_ref, yb_ref, yc_ref, gates_ref, wa_ref, wb_ref, wc_ref, wo_ref, gf_ref,
                o_ref):
    u = jnp.zeros((MERGE_TILE, D_MODEL), F32)
    for n, (y_ref, w_ref) in enumerate(((ya_ref, wa_ref), (yb_ref, wb_ref), (yc_ref, wc_ref))):
        sl = slice(n * D_MODEL, (n + 1) * D_MODEL)
        u = u + gates_ref[:, sl].astype(F32) * _dot(y_ref[...], w_ref[...])
    y = x_ref[...] + _dot(u.astype(BF16), wo_ref[...])
    o_ref[...] = _rmsnorm(y, gf_ref[...])


MERGE_TILE = 2 * PROJ_TILE


def _merge(x2, ya, yb, yc, gates, wa, wb, wc, wo, g_final):
    tile = lambda w: pl.BlockSpec((MERGE_TILE, w), lambda i: (i, 0))
    const = lambda r, w: pl.BlockSpec((r, w), lambda i: (0, 0))
    m = x2.shape[0]
    return pl.pallas_call(
        _merge_body,
        grid=(m // MERGE_TILE,),
        in_specs=[tile(D_MODEL), tile(ATT_W), tile(ATT_W), tile(X_W), tile(N_BRANCH * D_MODEL),
                  const(ATT_W, D_MODEL), const(ATT_W, D_MODEL), const(X_W, D_MODEL),
                  const(D_MODEL, D_MODEL), const(1, D_MODEL)],
        out_specs=tile(D_MODEL),
        out_shape=jax.ShapeDtypeStruct((m, D_MODEL), F32),
        compiler_params=pltpu.CompilerParams(
            dimension_semantics=("arbitrary",), vmem_limit_bytes=VMEM_LIMIT),
        name="merge",
    )(x2, ya, yb, yc, gates, wa, wb, wc, wo, g_final)


def _pack_w_in(w_in):
    offs = [0]
    for s in IN_SIZES:
        offs.append(offs[-1] + s)
    col = lambda n: w_in[:, offs[n]:offs[n + 1]]
    w_cols = jnp.concatenate([col(1), col(5), col(9), col(9), col(3), col(7), col(11), col(12),
                              col(13)], axis=1)
    iw = jnp.pad(col(10), ((0, 0), (0, BF16_ROWS - IDX_HEADS)))
    w_rows = jnp.concatenate([col(0), col(4), col(8), col(2), col(6), iw], axis=1).astype(BF16).T
    return w_cols.astype(BF16), w_rows


def _rope_tables():
    inv = jnp.power(ROPE_THETA, -jnp.arange(HALF, dtype=F32) * 2.0 / HEAD_DIM)
    ang = jnp.arange(SEQ).astype(F32)[:, None] * inv[None, :]
    cos = jnp.cos(ang)
    sin = jnp.sin(ang)
    cos_lanes = jnp.tile(cos, (1, LANES // HALF))
    sin_lanes = jnp.tile(jnp.concatenate([-sin, sin], axis=1), (1, HEADS_PER_PAIR))
    return cos_lanes, sin_lanes, cos.T, sin.T


def _layer(x2, mem2, g_in, w_in, b_merge, g_mem, w_mem_kv, w_up_moba, w_up_dsa, w_up_cross, w_out,
           g_final, batch):
    w_cols, w_rows = _pack_w_in(w_in)
    xk, xv = _mem_proj(mem2, g_mem[None, :], w_mem_kv.astype(BF16))
    (mqT, dqT, iqT, mvT, dvT, iwT, mk, dk, ik, mgate, dgate, yc, gates, kmean, ya0, yb0) = _in_proj(
        x2, g_in[None, :], w_cols, w_rows, _rope_tables(), b_merge[None, :],
        xk.reshape(batch, MEM_LEN, X_W), xv.reshape(batch, MEM_LEN, X_W), batch)
    ya, yb = _mixers(mqT, mk, mvT, kmean.reshape(batch, N_BLK, ATT_W), mgate, iqT, ik, iwT, dqT, dk,
                     dvT, dgate, ya0, yb0, batch)
    return _merge(x2, ya, yb, yc, gates, w_up_moba.astype(BF16), w_up_dsa.astype(BF16),
                  w_up_cross.astype(BF16), w_out.astype(BF16), g_final)


def kernel(x, mem, g_in, w_in, b_merge, g_mem, w_mem_kv, w_up_moba, w_up_dsa, w_up_cross, w_out,
           g_final):
    batch, seq, d = x.shape
    assert seq == SEQ and d == D_MODEL and mem.shape[1] == MEM_LEN
    assert (batch * MEM_LEN) % PROJ_TILE == 0
    assert g_in.shape[0] == 1
    out = _layer(x.reshape(batch * seq, d), mem.reshape(batch * MEM_LEN, d), g_in[0], w_in[0],
                 b_merge[0], g_mem[0], w_mem_kv[0], w_up_moba[0], w_up_dsa[0], w_up_cross[0],
                 w_out[0], g_final[None, :], batch)
    return out.reshape(batch, seq, d)
```
